```python
import math
import numpy as np
import jax
import jax.numpy as jnp
from jax import lax

D_MODEL = 1024
BATCH = 4
SEQ = 4096
DEPTH = 1
DEC_BATCH = 32
DEC_SEQ = 1
PAST_LEN = 16384
PAGE_SIZE = 128

HEAD_DIM = 64
N_HEADS = D_MODEL // HEAD_DIM
N_KV = 4
HPG = N_HEADS // N_KV
CMP_STRIDE = 16
CMP_BLOCK = 2 * CMP_STRIDE
CMP_HIDDEN = HEAD_DIM
SEL_BLOCK = 64
N_SELECT = 16
WINDOW = 512
Q_BLOCK = 128
S5_WIDTH = D_MODEL // 2
S5_CH = 16
S5_GROUPS = S5_WIDTH // S5_CH
S5_STATE = 64
D_FF = 4 * D_MODEL
D_Q = N_HEADS * HEAD_DIM
D_KV = N_KV * HEAD_DIM
D_IN = D_Q + 6 * D_KV + 3 * N_HEADS + S5_WIDTH + 2 * D_MODEL
SPLITS = [D_Q, D_Q + 6 * D_KV, D_Q + 6 * D_KV + 3 * N_HEADS, D_Q + 6 * D_KV + 3 * N_HEADS + S5_WIDTH]
EPS = 1e-6
NEG = -1e30
FORCE = 1e4

WIN_CACHE = min(WINDOW, PAST_LEN)
N_PAGES = PAST_LEN // PAGE_SIZE
N_PHYS = DEC_BATCH * N_PAGES + max(1, (DEC_BATCH * N_PAGES) // 4)

kernel_name = 'nsa_s5_gated_hybrid_step'


def rmsnorm(x, g):
    x32 = x.astype(jnp.float32)
    y = x32 * lax.rsqrt(jnp.mean(x32 * x32, axis=-1, keepdims=True) + EPS)
    return (y * g.astype(jnp.float32)).astype(x.dtype)


def masked_softmax(s, mask):
    s = jnp.where(mask, s.astype(jnp.float32), NEG)
    return jnp.where(mask, jax.nn.softmax(s, axis=-1), 0.0)


def split_proj(h, w_in):
    b, t = h.shape[0], h.shape[1]
    z = jnp.einsum('btd,de->bte', h, w_in)
    q, kv, gn, u, gm = jnp.split(z, SPLITS, axis=-1)
    q = q.reshape(b, t, N_KV, HPG, HEAD_DIM)
    kv = kv.reshape(b, t, 3, 2, N_KV, HEAD_DIM)
    gn = gn.reshape(b, t, N_KV, HPG, 3)
    return q, kv, gn, u, gm


def compress(k, pe, w1, w2):
    b, t = k.shape[0], k.shape[1]
    nch = t // CMP_STRIDE
    ch = k[:, : nch * CMP_STRIDE].reshape(b, nch, CMP_STRIDE, N_KV, HEAD_DIM)
    first = jnp.einsum('bnsgd,sdh->bngh', ch, w1[:CMP_STRIDE])
    second = jnp.einsum('bnsgd,sdh->bngh', ch, w1[CMP_STRIDE:])
    pos = jnp.einsum('sd,sdh->h', pe, w1)
    pre = first[:, :-1] + second[:, 1:] + pos
    return jnp.einsum('bngh,hd->bngd', jax.nn.gelu(pre), w2)


def sel_blocks(kv):
    b, t = kv.shape[0], kv.shape[1]
    nsel = -(-t // SEL_BLOCK)
    kv = jnp.pad(kv, ((0, 0), (0, nsel * SEL_BLOCK - t), (0, 0), (0, 0), (0, 0)))
    kv = kv.reshape(b, nsel, SEL_BLOCK, 2, N_KV, HEAD_DIM)
    return (jnp.transpose(kv[:, :, :, 0], (0, 3, 1, 2, 4)),
            jnp.transpose(kv[:, :, :, 1], (0, 3, 1, 2, 4)))


def nsa_attend(q, gates, qpos, kc, vc, ks, vs, kw, vw, kw_pos):
    scale = HEAD_DIM ** -0.5
    nc = kc.shape[1]
    c_end = jnp.arange(nc) * CMP_STRIDE + CMP_BLOCK - 1
    s = jnp.einsum('bqghd,bngd->bqghn', q, kc) * scale
    p = masked_softmax(s, (c_end[None, :] <= qpos[:, None])[None, :, None, None, :])
    o_cmp = jnp.einsum('bqghn,bngd->bqghd', p.astype(vc.dtype), vc)
    imp = p.sum(axis=3)
    nsel = ks.shape[2]
    j = np.arange(nsel)
    lo = np.clip((j * SEL_BLOCK - CMP_BLOCK) // CMP_STRIDE + 1, 0, nc)
    hi = np.clip((j * SEL_BLOCK + SEL_BLOCK - 1) // CMP_STRIDE + 1, 0, nc)
    cs = jnp.concatenate([jnp.zeros_like(imp[..., :1]), jnp.cumsum(imp, axis=-1)], axis=-1)
    imp_sel = cs[..., hi] - cs[..., lo]
    jj = jnp.arange(nsel)
    cur = qpos // SEL_BLOCK
    forced = (jj[None, :] == 0) | (jj[None, :] == cur[:, None]) | (jj[None, :] == cur[:, None] - 1)
    valid = jj[None, :] * SEL_BLOCK <= qpos[:, None]
    score = jnp.where(valid[None, :, None, :],
                      jnp.where(forced[None, :, None, :], FORCE, imp_sel), -1.0)
    k_sel = min(N_SELECT, nsel)
    _, idx = lax.top_k(score, k_sel)
    bi = jnp.arange(q.shape[0])[:, None, None, None]
    gi = jnp.arange(N_KV)[None, None, :, None]
    k_g = ks[bi, gi, idx]
    v_g = vs[bi, gi, idx]
    s = jnp.einsum('bqghd,bqgksd->bqghks', q, k_g) * scale
    tok = idx[..., None] * SEL_BLOCK + jnp.arange(SEL_BLOCK)
    mask = (tok <= qpos[None, :, None, None, None])[:, :, :, None]
    shp = s.shape
    p = masked_softmax(s.reshape(shp[:4] + (-1,)), mask.reshape(mask.shape[:4] + (-1,)))
    o_slc = jnp.einsum('bqghks,bqgksd->bqghd', p.reshape(shp).astype(v_g.dtype), v_g)
    s = jnp.einsum('bqghd,bwgd->bqghw', q, kw) * scale
    rel = qpos[:, None] - kw_pos[None, :]
    wmask = (rel >= 0) & (rel < WINDOW) & (kw_pos[None, :] >= 0)
    p = masked_softmax(s, wmask[None, :, None, None, :])
    o_win = jnp.einsum('bqghw,bwgd->bqghd', p.astype(vw.dtype), vw)
    g = jax.nn.sigmoid(gates)
    return g[..., 0:1] * o_cmp + g[..., 1:2] * o_slc + g[..., 2:3] * o_win


def s5(u, h0, w):
    b, t = u.shape[0], u.shape[1]
    f32 = jnp.float32
    uc = u.astype(f32).reshape(b, t, S5_GROUPS, S5_CH)
    lam = lax.complex(w['s5_a_re'].astype(f32), w['s5_a_im'].astype(f32))
    dt = jnp.exp(w['s5_log_dt'].astype(f32))[:, None]
    a_bar = jnp.exp(lam * dt)
    b_c = lax.complex(w['s5_b_re'].astype(f32), w['s5_b_im'].astype(f32))
    b_bar = ((a_bar - 1.0) / lam)[..., None] * b_c
    c_c = lax.complex(w['s5_c_re'].astype(f32), w['s5_c_im'].astype(f32))
    bu = jnp.einsum('gnc,btgc->btgn', b_bar, uc.astype(jnp.complex64))
    a = jnp.broadcast_to(a_bar, bu.shape)

    def combine(e1, e2):
        a1, b1 = e1
        a2, b2 = e2
        return a1 * a2, a2 * b1 + b2

    a_cum, b_cum = lax.associative_scan(combine, (a, bu), axis=1)
    h = b_cum + a_cum * h0[:, None]
    y = jnp.einsum('gcn,btgn->btgc', c_c, h).real + w['s5_d'].astype(f32).reshape(S5_GROUPS, S5_CH) * uc
    return y.reshape(b, t, S5_WIDTH).astype(u.dtype), h[:, -1]


def merge_and_mlp(x, o_nsa, y_s5, gm, w):
    b, t = x.shape[0], x.shape[1]
    z = jax.nn.gelu(y_s5)
    o_s5 = z * jax.nn.sigmoid(z @ w['s5_w_glu'] + w['s5_b_glu'])
    g_a, g_b = jnp.split(gm, 2, axis=-1)
    merged = (jax.nn.sigmoid(g_a) * (o_nsa.reshape(b, t, D_Q) @ w['w_branch_nsa'])
              + jax.nn.sigmoid(g_b) * (o_s5 @ w['w_branch_s5']))
    x = x + rmsnorm(merged @ w['w_out'], w['norm_mix_post'])
    hm = rmsnorm(x, w['norm_mlp_pre'])
    f = jnp.square(jax.nn.relu(hm @ w['w_mlp_up'])) @ w['w_mlp_down']
    return x + rmsnorm(f, w['norm_mlp_post'])


def stack_state(h, dtype):
    return jnp.stack([h.real, h.imag], axis=-1).astype(dtype)


def prompt_layer(x, w):
    b, t = x.shape[0], x.shape[1]
    h = rmsnorm(x, w['norm_mix_pre'])
    q, kv, gn, u, gm = split_proj(h, w['w_in'])
    kc = compress(kv[:, :, 0, 0], w['cmp_pe_k'], w['cmp_w1_k'], w['cmp_w2_k'])
    vc = compress(kv[:, :, 0, 1], w['cmp_pe_v'], w['cmp_w1_v'], w['cmp_w2_v'])
    ks, vs = sel_blocks(kv[:, :, 1])
    kw_pad = jnp.pad(kv[:, :, 2], ((0, 0), (WINDOW, 0), (0, 0), (0, 0), (0, 0)))
    nqb = t // Q_BLOCK
    q_b = q.reshape(b, nqb, Q_BLOCK, N_KV, HPG, HEAD_DIM).swapaxes(0, 1)
    g_b = gn.reshape(b, nqb, Q_BLOCK, N_KV, HPG, 3).swapaxes(0, 1)
    starts = jnp.arange(nqb, dtype=jnp.int32) * Q_BLOCK

    def block(args):
        qb, gb, s0 = args
        qpos = s0 + jnp.arange(Q_BLOCK, dtype=jnp.int32)
        kvw = lax.dynamic_slice_in_dim(kw_pad, s0, WINDOW + Q_BLOCK, axis=1)
        kw_pos = s0 - WINDOW + jnp.arange(WINDOW + Q_BLOCK, dtype=jnp.int32)
        return nsa_attend(qb, gb, qpos, kc, vc, ks, vs, kvw[:, :, 0], kvw[:, :, 1], kw_pos)

    o = lax.map(block, (q_b, g_b, starts))
    o = o.swapaxes(0, 1).reshape(b, t, N_KV, HPG, HEAD_DIM)
    y_s5, h_last = s5(u, jnp.zeros((b, S5_GROUPS, S5_STATE), jnp.complex64), w)
    y = merge_and_mlp(x, o, y_s5, gm, w)
    return y, kv[:, :, 0], kv[:, :, 1], kw_pad[:, -WIN_CACHE:], stack_state(h_last, x.dtype)


def gather_pages(cache, page_table):
    g = cache[page_table]
    return g.reshape(page_table.shape[0], page_table.shape[1] * PAGE_SIZE, 2, N_KV, HEAD_DIM)


def sample_layer(x, cache_kv_cmp, cache_kv_slc, state_kv_win, state_s5, page_table, w):
    t = x.shape[1]
    past = page_table.shape[1] * PAGE_SIZE
    h = rmsnorm(x, w['norm_mix_pre'])
    q, kv, gn, u, gm = split_proj(h, w['w_in'])
    kv_cmp = jnp.concatenate([gather_pages(cache_kv_cmp, page_table), kv[:, :, 0].astype(cache_kv_cmp.dtype)], axis=1)
    kc = compress(kv_cmp[:, :, 0], w['cmp_pe_k'], w['cmp_w1_k'], w['cmp_w2_k'])
    vc = compress(kv_cmp[:, :, 1], w['cmp_pe_v'], w['cmp_w1_v'], w['cmp_w2_v'])
    kv_slc = jnp.concatenate([gather_pages(cache_kv_slc, page_table), kv[:, :, 1].astype(cache_kv_slc.dtype)], axis=1)
    ks, vs = sel_blocks(kv_slc)
    kvw = jnp.concatenate([state_kv_win, kv[:, :, 2].astype(state_kv_win.dtype)], axis=1)
    win_rows = state_kv_win.shape[1]
    kw_pos = past - win_rows + jnp.arange(win_rows + t, dtype=jnp.int32)
    qpos = past + jnp.arange(t, dtype=jnp.int32)
    o = nsa_attend(q, gn, qpos, kc, vc, ks, vs, kvw[:, :, 0], kvw[:, :, 1], kw_pos)
    st = state_s5.astype(jnp.float32)
    y_s5, h_last = s5(u, lax.complex(st[..., 0], st[..., 1]), w)
    y = merge_and_mlp(x, o, y_s5, gm, w)
    return y, kv[:, :, 0], kv[:, :, 1], kvw[:, -win_rows:], stack_state(h_last, state_s5.dtype)


def setup_inputs(seed: int = 0) -> dict:
    key = jax.random.key(seed)
    ks = jax.random.split(key, 40)
    nrm = jax.random.normal
    f32 = jnp.float32
    page_table = jax.random.permutation(ks[6], N_PHYS)[: DEC_BATCH * N_PAGES].reshape(DEC_BATCH, N_PAGES).astype(jnp.int32)
    a_im = math.pi * jnp.arange(S5_STATE, dtype=f32)[None, :] + 0.01 * nrm(ks[19], (S5_GROUPS, S5_STATE), f32)
    return {
        'x_prompt': nrm(ks[0], (BATCH, SEQ, D_MODEL), f32),
        'x_sample': nrm(ks[1], (DEC_BATCH, DEC_SEQ, D_MODEL), f32),
        'cache_kv_cmp': nrm(ks[2], (N_PHYS, PAGE_SIZE, 2, N_KV, HEAD_DIM), f32),
        'cache_kv_slc': nrm(ks[3], (N_PHYS, PAGE_SIZE, 2, N_KV, HEAD_DIM), f32),
        'state_kv_win': nrm(ks[4], (DEC_BATCH, WIN_CACHE, 2, N_KV, HEAD_DIM), f32),
        'state_s5': 0.1 * nrm(ks[5], (DEC_BATCH, S5_GROUPS, S5_STATE, 2), f32),
        'page_table': page_table,
        'norm_mix_pre': 1.0 + 0.05 * nrm(ks[7], (D_MODEL,), f32),
        'norm_mix_post': 1.0 + 0.05 * nrm(ks[8], (D_MODEL,), f32),
        'norm_mlp_pre': 1.0 + 0.05 * nrm(ks[9], (D_MODEL,), f32),
        'norm_mlp_post': 1.0 + 0.05 * nrm(ks[10], (D_MODEL,), f32),
        'w_in': nrm(ks[11], (D_MODEL, D_IN), f32) * D_MODEL ** -0.5,
        'cmp_pe_k': 0.1 * nrm(ks[12], (CMP_BLOCK, HEAD_DIM), f32),
        'cmp_w1_k': nrm(ks[13], (CMP_BLOCK, HEAD_DIM, CMP_HIDDEN), f32) * (CMP_BLOCK * HEAD_DIM) ** -0.5,
        'cmp_w2_k': nrm(ks[14], (CMP_HIDDEN, HEAD_DIM), f32) * CMP_HIDDEN ** -0.5,
        'cmp_pe_v': 0.1 * nrm(ks[15], (CMP_BLOCK, HEAD_DIM), f32),
        'cmp_w1_v': nrm(ks[16], (CMP_BLOCK, HEAD_DIM, CMP_HIDDEN), f32) * (CMP_BLOCK * HEAD_DIM) ** -0.5,
        'cmp_w2_v': nrm(ks[17], (CMP_HIDDEN, HEAD_DIM), f32) * CMP_HIDDEN ** -0.5,
        's5_a_re': -0.5 + 0.01 * nrm(ks[18], (S5_GROUPS, S5_STATE), f32),
        's5_a_im': a_im,
        's5_log_dt': jax.random.uniform(ks[20], (S5_GROUPS,), f32, math.log(0.001), math.log(0.1)),
        's5_b_re': nrm(ks[21], (S5_GROUPS, S5_STATE, S5_CH), f32) * (2 * S5_CH) ** -0.5,
        's5_b_im': nrm(ks[22], (S5_GROUPS, S5_STATE, S5_CH), f32) * (2 * S5_CH) ** -0.5,
        's5_c_re': nrm(ks[23], (S5_GROUPS, S5_CH, S5_STATE), f32) * S5_STATE ** -0.5,
        's5_c_im': nrm(ks[24], (S5_GROUPS, S5_CH, S5_STATE), f32) * S5_STATE ** -0.5,
        's5_d': nrm(ks[25], (S5_WIDTH,), f32),
        's5_w_glu': nrm(ks[26], (S5_WIDTH, S5_WIDTH), f32) * S5_WIDTH ** -0.5,
        's5_b_glu': 0.01 * nrm(ks[27], (S5_WIDTH,), f32),
        'w_branch_nsa': nrm(ks[28], (D_Q, D_MODEL), f32) * D_Q ** -0.5,
        'w_branch_s5': nrm(ks[29], (S5_WIDTH, D_MODEL), f32) * S5_WIDTH ** -0.5,
        'w_out': nrm(ks[30], (D_MODEL, D_MODEL), f32) * D_MODEL ** -0.5,
        'w_mlp_up': nrm(ks[31], (D_MODEL, D_FF), f32) * D_MODEL ** -0.5,
        'w_mlp_down': nrm(ks[32], (D_FF, D_MODEL), f32) * D_FF ** -0.5,
    }


def reference(x_prompt, x_sample, cache_kv_cmp, cache_kv_slc, state_kv_win, state_s5, page_table,
              norm_mix_pre, norm_mix_post, norm_mlp_pre, norm_mlp_post, w_in,
              cmp_pe_k, cmp_w1_k, cmp_w2_k, cmp_pe_v, cmp_w1_v, cmp_w2_v,
              s5_a_re, s5_a_im, s5_log_dt, s5_b_re, s5_b_im, s5_c_re, s5_c_im, s5_d, s5_w_glu, s5_b_glu,
              w_branch_nsa, w_branch_s5, w_out, w_mlp_up, w_mlp_down):
    w = {
        'norm_mix_pre': norm_mix_pre, 'norm_mix_post': norm_mix_post,
        'norm_mlp_pre': norm_mlp_pre, 'norm_mlp_post': norm_mlp_post, 'w_in': w_in,
        'cmp_pe_k': cmp_pe_k, 'cmp_w1_k': cmp_w1_k, 'cmp_w2_k': cmp_w2_k,
        'cmp_pe_v': cmp_pe_v, 'cmp_w1_v': cmp_w1_v, 'cmp_w2_v': cmp_w2_v,
        's5_a_re': s5_a_re, 's5_a_im': s5_a_im, 's5_log_dt': s5_log_dt,
        's5_b_re': s5_b_re, 's5_b_im': s5_b_im, 's5_c_re': s5_c_re, 's5_c_im': s5_c_im,
        's5_d': s5_d, 's5_w_glu': s5_w_glu, 's5_b_glu': s5_b_glu,
        'w_branch_nsa': w_branch_nsa, 'w_branch_s5': w_branch_s5, 'w_out': w_out,
        'w_mlp_up': w_mlp_up, 'w_mlp_down': w_mlp_down,
    }
    y_p = x_prompt
    y_s = x_sample
    for _ in range(DEPTH):
        y_p, kvc_p, kvs_p, win_p, s5_p = prompt_layer(y_p, w)
        y_s, kvc_s, kvs_s, win_s, s5_s = sample_layer(y_s, cache_kv_cmp, cache_kv_slc, state_kv_win, state_s5, page_table, w)
    return (y_p, y_s, kvc_p, kvs_p, win_p, s5_p, kvc_s, kvs_s, win_s, s5_s)
```

```python
import functools
import math

import numpy as np
import jax
import jax.numpy as jnp
from jax import lax
from jax.experimental import pallas as pl
from jax.experimental.pallas import tpu as pltpu

F32 = jnp.float32
BF16 = jnp.bfloat16

D_MODEL = 1024
HEAD_DIM = 64
N_HEADS = 16
N_KV = 4
HPG = 4
CMP_STRIDE = 16
CMP_BLOCK = 32
SEL_BLOCK = 64
N_SELECT = 16
WINDOW = 512
Q_BLOCK = 128
S5_WIDTH = 512
S5_CH = 16
S5_GROUPS = 32
S5_STATE = 64
D_FF = 4096
D_Q = 1024
D_KV = 256
PAGE_SIZE = 128
EPS = 1e-6
NEG = -1e30
FORCE = 1e4
LANES = 128
VMEM_LIMIT = 56 * 1024 * 1024
KV_CHUNK = 256
WIN_KEYS = WINDOW + Q_BLOCK


def _cparams(sem):
    return pltpu.CompilerParams(dimension_semantics=sem, vmem_limit_bytes=VMEM_LIMIT)


def _gelu(x):
    return 0.5 * x * (1.0 + jnp.tanh(math.sqrt(2.0 / math.pi) * (x + 0.044715 * (x * x * x))))


def _sigmoid(x):
    return 1.0 / (1.0 + jnp.exp(-x))


def _rms(x, g):
    ms = jnp.mean(x * x, axis=-1, keepdims=True)
    return (x * lax.rsqrt(ms + EPS)) * g


def _split3(x):
    hi = x.astype(BF16)
    r1 = x - hi.astype(F32)
    mid = r1.astype(BF16)
    lo = (r1 - mid.astype(F32)).astype(BF16)
    return hi, mid, lo


def _inproj_prompt_kernel(seq_len, x_ref, g_ref, wr_ref, wt_ref,
                          kvc_ref, kvs_ref, kvw_ref, u_ref, gm_ref, kas_ref, kaw_ref,
                          qt_ref, vts_ref, vtw_ref, gnt_ref):
    tm = x_ref.shape[0]
    hb = _rms(x_ref[...], g_ref[...]).astype(BF16)

    def rowdot(lo, hi):
        return jnp.dot(hb, wr_ref[:, lo:hi], preferred_element_type=F32)

    kvc_ref[...] = rowdot(0, 512)
    kvs_ref[...] = rowdot(512, 1024)
    kvw_ref[...] = rowdot(1024, 1536)
    u_ref[...] = rowdot(1536, 2048)
    gm_ref[...] = rowdot(2048, 4096)
    row = pl.program_id(0) * tm + lax.broadcasted_iota(jnp.int32, (tm, LANES), 0)
    blk = lax.rem(row, seq_len) // SEL_BLOCK
    lane = lax.broadcasted_iota(jnp.int32, (tm, LANES), 1)
    onehot = jnp.where(lane - HEAD_DIM == blk, 1.0, 0.0)
    zs = rowdot(4096, 4608)
    zw = rowdot(4608, 5120)
    for g in range(N_KV):
        kas_ref[g] = (zs[:, g * LANES:(g + 1) * LANES] + onehot).astype(BF16)
        kaw_ref[g] = zw[:, g * LANES:(g + 1) * LANES].astype(BF16)
    zt = lax.dot_general(wt_ref[...], hb, (((1,), (1,)), ((), ())), preferred_element_type=F32)
    for c in range(tm // LANES):
        sl = slice(c * LANES, (c + 1) * LANES)
        qt_ref[c] = (zt[0:1024, sl] * (HEAD_DIM ** -0.5)).astype(BF16)
        vts_ref[c] = zt[1024:1280, sl].astype(BF16)
        vtw_ref[c] = zt[1280:1536, sl].astype(BF16)
        gnt_ref[c] = zt[1536:1600, sl]


def _inproj_weights(w_in):
    wq, wkv, wgn, wu, wgm = (w_in[:, :1024], w_in[:, 1024:2560], w_in[:, 2560:2608],
                             w_in[:, 2608:3120], w_in[:, 3120:])
    wkv6 = wkv.reshape(D_MODEL, 3, 2, N_KV, HEAD_DIM)
    zpad = jnp.zeros((D_MODEL, N_KV, HEAD_DIM), F32)
    kaug_s = jnp.concatenate([wkv6[:, 1, 0], zpad], axis=-1).reshape(D_MODEL, N_KV * LANES)
    kaug_w = jnp.concatenate([wkv6[:, 2, 0], zpad], axis=-1).reshape(D_MODEL, N_KV * LANES)
    w_row = jnp.concatenate([wkv, wu, wgm, kaug_s, kaug_w], axis=1).astype(BF16)
    gn_rows = jnp.pad(wgn.T.reshape(N_KV, HPG * 3, D_MODEL), ((0, 0), (0, 4), (0, 0))).reshape(64, D_MODEL)
    w_t = jnp.concatenate([wq.T, wkv6[:, 1, 1].reshape(D_MODEL, D_KV).T,
                           wkv6[:, 2, 1].reshape(D_MODEL, D_KV).T, gn_rows], axis=0).astype(BF16)
    return w_row, w_t


def _inproj_prompt(x2, g_pre, w_row, w_t, seq_len, tm=256):
    n = x2.shape[0]
    nc = n // LANES
    cpt = tm // LANES
    row = lambda w: pl.BlockSpec((tm, w), lambda i: (i, 0))
    out_shape = (
        jax.ShapeDtypeStruct((n, 512), F32), jax.ShapeDtypeStruct((n, 512), F32), jax.ShapeDtypeStruct((n, 512), F32),
        jax.ShapeDtypeStruct((n, 512), F32), jax.ShapeDtypeStruct((n, 2048), F32),
        jax.ShapeDtypeStruct((N_KV, n, LANES), BF16), jax.ShapeDtypeStruct((N_KV, n, LANES), BF16),
        jax.ShapeDtypeStruct((nc, 1024, LANES), BF16), jax.ShapeDtypeStruct((nc, 256, LANES), BF16),
        jax.ShapeDtypeStruct((nc, 256, LANES), BF16), jax.ShapeDtypeStruct((nc, 64, LANES), F32),
    )
    out_specs = (
        row(512), row(512), row(512), row(512), row(2048),
        pl.BlockSpec((N_KV, tm, LANES), lambda i: (0, i, 0)), pl.BlockSpec((N_KV, tm, LANES), lambda i: (0, i, 0)),
        pl.BlockSpec((cpt, 1024, LANES), lambda i: (i, 0, 0)), pl.BlockSpec((cpt, 256, LANES), lambda i: (i, 0, 0)),
        pl.BlockSpec((cpt, 256, LANES), lambda i: (i, 0, 0)), pl.BlockSpec((cpt, 64, LANES), lambda i: (i, 0, 0)),
    )
    return pl.pallas_call(
        functools.partial(_inproj_prompt_kernel, seq_len),
        grid=(n // tm,),
        in_specs=[row(D_MODEL), pl.BlockSpec((1, D_MODEL), lambda i: (0, 0)),
                  pl.BlockSpec(w_row.shape, lambda i: (0, 0)), pl.BlockSpec(w_t.shape, lambda i: (0, 0))],
        out_specs=out_specs, out_shape=out_shape,
        compiler_params=_cparams(("parallel",)), name="inproj_prompt",
    )(x2, g_pre.reshape(1, D_MODEL), w_row, w_t)


def _cmp_weights(w1_k, w1_v):
    eye2 = jnp.eye(2, dtype=F32)
    out = []
    for w1 in (w1_k, w1_v):
        w = w1.reshape(2, CMP_STRIDE, HEAD_DIM, HEAD_DIM)
        big = jnp.einsum('fsdh,ij->sidfjh', w, eye2).reshape(CMP_STRIDE * 2 * HEAD_DIM, 2 * 2 * HEAD_DIM)
        out += [big, big]
    return jnp.stack(out).astype(BF16)


def _cmp_y_kernel(n_in, *refs):
    x_refs, w_ref, y_ref = refs[:n_in], refs[n_in], refs[n_in + 1]
    if n_in == 1:
        x = x_refs[0][...]
    else:
        x = jnp.concatenate([r[0] for r in x_refs], axis=0)
    for combo in range(4):
        kv, gp = combo // 2, combo % 2
        base = kv * 256 + gp * LANES
        xg = jnp.concatenate([x[:, s * 512 + base: s * 512 + base + LANES] for s in range(CMP_STRIDE)], axis=1)
        y = jnp.dot(xg.astype(BF16), w_ref[combo], preferred_element_type=F32)
        y_ref[0, :, combo * 256:(combo + 1) * 256] = y


def _cmp_y_prompt(kvc, bsz, seq_len, wcmp):
    nch = seq_len // CMP_STRIDE
    xc = kvc.reshape(bsz * nch, CMP_STRIDE * 512)
    rb = min(nch, 256)
    per = nch // rb
    return pl.pallas_call(
        functools.partial(_cmp_y_kernel, 1),
        grid=(bsz, per),
        in_specs=[pl.BlockSpec((rb, CMP_STRIDE * 512), lambda b, j: (b * per + j, 0)),
                  pl.BlockSpec(wcmp.shape, lambda b, j: (0, 0, 0))],
        out_specs=pl.BlockSpec((1, rb, 1024), lambda b, j: (b, j, 0)),
        out_shape=jax.ShapeDtypeStruct((bsz, nch, 1024), F32),
        compiler_params=_cparams(("parallel", "parallel")), name="cmp_y_prompt",
    )(xc, wcmp)


def _cmp_combine_kernel(y_ref, pe_ref, w1f_ref, w2k_ref, w2vt_ref, kc_ref, vct_ref):
    r = y_ref.shape[1]
    pos = jnp.dot(pe_ref[...], w1f_ref[...], preferred_element_type=F32,
                  precision=lax.Precision.HIGHEST)
    for combo in range(4):
        kv, gp = combo // 2, combo % 2
        first = y_ref[0, :, combo * 256: combo * 256 + LANES]
        second = pltpu.roll(y_ref[0, :, combo * 256 + LANES: combo * 256 + 2 * LANES], r - 1, 0)
        p1 = pos[0:1, kv * HEAD_DIM:(kv + 1) * HEAD_DIM]
        pre = first + second + jnp.concatenate([p1, p1], axis=1)
        act = _gelu(pre).astype(BF16)
        if kv == 0:
            kc = jnp.dot(act, w2k_ref[...], preferred_element_type=F32)
            kc_ref[0, 2 * gp] = kc[:, 0:LANES].astype(BF16)
            kc_ref[0, 2 * gp + 1] = kc[:, LANES:2 * LANES].astype(BF16)
        else:
            vct = lax.dot_general(w2vt_ref[...], act, (((1,), (1,)), ((), ())), preferred_element_type=F32)
            vct_ref[0, 2 * gp] = vct[0:HEAD_DIM].astype(BF16)
            vct_ref[0, 2 * gp + 1] = vct[HEAD_DIM:2 * HEAD_DIM].astype(BF16)


def _cmp_combine(y, pe_k, w1_k, w2_k, pe_v, w1_v, w2_v):
    s, r, _ = y.shape
    pe = jnp.concatenate([pe_k.reshape(1, -1), pe_v.reshape(1, -1)], axis=1)
    pe8 = jnp.pad(pe, ((0, 7), (0, 0)))
    z = jnp.zeros((CMP_BLOCK * HEAD_DIM, HEAD_DIM), F32)
    w1f = jnp.concatenate([jnp.concatenate([w1_k.reshape(-1, HEAD_DIM), z], axis=1),
                           jnp.concatenate([z, w1_v.reshape(-1, HEAD_DIM)], axis=1)], axis=0)
    z64 = jnp.zeros((HEAD_DIM, HEAD_DIM), F32)
    w2k = jnp.concatenate([jnp.concatenate([w2_k, z64, z64, z64], axis=1),
                           jnp.concatenate([z64, z64, w2_k, z64], axis=1)], axis=0).astype(BF16)
    w2vt = jnp.concatenate([jnp.concatenate([w2_v.T, z64], axis=1),
                            jnp.concatenate([z64, w2_v.T], axis=1)], axis=0).astype(BF16)
    full = lambda a: pl.BlockSpec(a.shape, lambda i: (0,) * a.ndim)
    return pl.pallas_call(
        _cmp_combine_kernel,
        grid=(s,),
        in_specs=[pl.BlockSpec((1, r, 1024), lambda i: (i, 0, 0)), full(pe8), full(w1f), full(w2k), full(w2vt)],
        out_specs=(pl.BlockSpec((1, N_KV, r, LANES), lambda i: (i, 0, 0, 0)),
                   pl.BlockSpec((1, N_KV, HEAD_DIM, r), lambda i: (i, 0, 0, 0))),
        out_shape=(jax.ShapeDtypeStruct((s, N_KV, r, LANES), BF16),
                   jax.ShapeDtypeStruct((s, N_KV, HEAD_DIM, r), BF16)),
        compiler_params=_cparams(("parallel",)), name="cmp_combine",
    )(y, pe8, w1f, w2k, w2vt)


def _sel_matrix(nsel, nc, ncp):
    j = np.arange(nsel)
    lo = np.clip((j * SEL_BLOCK - CMP_BLOCK) // CMP_STRIDE + 1, 0, nc)
    hi = np.clip((j * SEL_BLOCK + SEL_BLOCK - 1) // CMP_STRIDE + 1, 0, nc)
    n = np.arange(ncp)
    return ((n[None, :] >= lo[:, None]) & (n[None, :] < hi[:, None])).astype(np.float32)


def _rank_select(score, k_sel):
    nj, nl = score.shape
    sub = 8
    tiles = [score[v * sub:(v + 1) * sub] for v in range(nj // sub)]
    cnts = [jnp.zeros((sub, nl), F32) for _ in tiles]
    jloc = lax.broadcasted_iota(jnp.int32, (sub, nl), 0)
    for i in range(nj):
        bi = jnp.broadcast_to(score[i:i + 1, :], (sub, nl))
        for v, t in enumerate(tiles):
            if v * sub > i:
                inc = jnp.where(bi >= t, 1.0, 0.0)
            elif v * sub + sub - 1 < i:
                inc = jnp.where(bi > t, 1.0, 0.0)
            else:
                inc = jnp.where(jloc > i - v * sub, jnp.where(bi >= t, 1.0, 0.0), jnp.where(bi > t, 1.0, 0.0))
            cnts[v] = cnts[v] + inc
    cnt = jnp.concatenate(cnts, axis=0)
    return jnp.where(cnt < k_sel, 1.0, 0.0)


def _nsa_prompt_kernel(nc_valid, k_sel, qt_ref, gnt_ref, kc_ref, vct_ref, kas_ref, vts_ref, kaw_ref, vtw_ref,
                       a_ref, o_ref):
    qb = pl.program_id(2)
    nlane = HPG * Q_BLOCK
    qt = qt_ref[0]
    qpos = qb * Q_BLOCK + lax.rem(lax.broadcasted_iota(jnp.int32, (1, nlane), 1), Q_BLOCK)

    rhs_q = jnp.concatenate([qt[h * HEAD_DIM:(h + 1) * HEAD_DIM] for h in range(HPG)], axis=1)
    kc = kc_ref[0, 0][:, 0:HEAD_DIM]
    ncp = kc.shape[0]
    sc = jnp.dot(kc, rhs_q, preferred_element_type=F32)
    nrow = lax.broadcasted_iota(jnp.int32, (ncp, nlane), 0)
    cmask = (nrow * CMP_STRIDE + (CMP_BLOCK - 1) <= qpos) & (nrow < nc_valid)
    sc = jnp.where(cmask, sc, NEG)
    e = jnp.where(cmask, jnp.exp(sc - jnp.max(sc, axis=0, keepdims=True)), 0.0)
    den = jnp.sum(e, axis=0, keepdims=True)
    p = e / jnp.maximum(den, 1e-30)
    o_c = jnp.dot(vct_ref[0, 0], p.astype(BF16), preferred_element_type=F32)
    imp = p[:, 0:Q_BLOCK]
    for h in range(1, HPG):
        imp = imp + p[:, h * Q_BLOCK:(h + 1) * Q_BLOCK]
    a = a_ref[...]
    imp_sel = sum(jnp.dot(a, part, preferred_element_type=F32) for part in _split3(imp))

    nsel = imp_sel.shape[0]
    jrow = lax.broadcasted_iota(jnp.int32, (nsel, Q_BLOCK), 0)
    qp1 = qb * Q_BLOCK + lax.broadcasted_iota(jnp.int32, (nsel, Q_BLOCK), 1)
    cur = qp1 // SEL_BLOCK
    forced = (jrow == 0) | (jrow == cur) | (jrow == cur - 1)
    score = jnp.where(jrow <= cur, jnp.where(forced, FORCE, imp_sel), -1.0)
    sel = _rank_select(score, k_sel)
    mq = ((sel - 1.0) * 1e30).astype(BF16)
    if nsel < HEAD_DIM:
        mq = jnp.concatenate([mq, jnp.zeros((HEAD_DIM - nsel, Q_BLOCK), BF16)], axis=0)
    rhs_aug = jnp.concatenate(
        [jnp.concatenate([qt[h * HEAD_DIM:(h + 1) * HEAD_DIM], mq], axis=0) for h in range(HPG)], axis=1)

    def sweep(c, carry, causal):
        m, l, acc = carry
        start = pl.multiple_of(c * KV_CHUNK, KV_CHUNK)
        s = jnp.dot(kas_ref[0, pl.ds(start, KV_CHUNK), :], rhs_aug, preferred_element_type=F32)
        if causal:
            tok = start + lax.broadcasted_iota(jnp.int32, (KV_CHUNK, nlane), 0)
            s = jnp.where(tok <= qpos, s, NEG)
        m_new = jnp.maximum(m, jnp.max(s, axis=0, keepdims=True))
        alpha = jnp.exp(m - m_new)
        pr = jnp.exp(s - m_new)
        l = l * alpha + jnp.sum(pr, axis=0, keepdims=True)
        vt = jnp.concatenate([vts_ref[2 * c], vts_ref[2 * c + 1]], axis=1)
        acc = acc * alpha + jnp.dot(vt, pr.astype(BF16), preferred_element_type=F32)
        return m_new, l, acc

    init = (jnp.full((1, nlane), NEG, F32), jnp.zeros((1, nlane), F32), jnp.zeros((HEAD_DIM, nlane), F32))
    n_full = qb // 2
    carry = lax.fori_loop(0, n_full, lambda c, cr: sweep(c, cr, False), init)
    m, l, acc = sweep(n_full, carry, True)
    o_s = acc / l

    c0 = jnp.maximum(qb - WINDOW // Q_BLOCK, 0)
    wstart = pl.multiple_of(c0 * Q_BLOCK, Q_BLOCK)
    sw = jnp.dot(kaw_ref[0, pl.ds(wstart, WIN_KEYS), :], rhs_aug, preferred_element_type=F32)
    rel = qpos - (wstart + lax.broadcasted_iota(jnp.int32, (WIN_KEYS, nlane), 0))
    wmask = (rel >= 0) & (rel < WINDOW)
    sw = jnp.where(wmask, sw, NEG)
    ew = jnp.where(wmask, jnp.exp(sw - jnp.max(sw, axis=0, keepdims=True)), 0.0)
    lw = jnp.sum(ew, axis=0, keepdims=True)
    vtw = jnp.concatenate([vtw_ref[c0 + i] for i in range(WIN_KEYS // Q_BLOCK)], axis=1)
    o_w = jnp.dot(vtw, ew.astype(BF16), preferred_element_type=F32) / lw

    gate = _sigmoid(gnt_ref[0])
    outs = []
    for h in range(HPG):
        sl = slice(h * Q_BLOCK, (h + 1) * Q_BLOCK)
        outs.append(gate[3 * h:3 * h + 1] * o_c[:, sl] + gate[3 * h + 1:3 * h + 2] * o_s[:, sl]
                    + gate[3 * h + 2:3 * h + 3] * o_w[:, sl])
    o_t = jnp.concatenate(outs, axis=0)
    o_ref[...] = o_t.T.astype(BF16)


def _nsa_prompt(qt, gnt, kc, vct, kas, vts, kaw, vtw, bsz, seq_len):
    nq = seq_len // Q_BLOCK
    nsel = seq_len // SEL_BLOCK
    nc_valid = seq_len // CMP_STRIDE - 1
    ncp = kc.shape[2]
    k_sel = min(N_SELECT, nsel)
    a = jnp.asarray(_sel_matrix(nsel, nc_valid, ncp), BF16)
    n = bsz * seq_len
    return pl.pallas_call(
        functools.partial(_nsa_prompt_kernel, nc_valid, k_sel),
        grid=(bsz, N_KV, nq),
        in_specs=[
            pl.BlockSpec((1, HPG * HEAD_DIM, LANES), lambda b, g, i: (b * nq + i, g, 0)),
            pl.BlockSpec((1, 16, LANES), lambda b, g, i: (b * nq + i, g, 0)),
            pl.BlockSpec((1, 1, ncp, LANES), lambda b, g, i: (b, g, 0, 0)),
            pl.BlockSpec((1, 1, HEAD_DIM, ncp), lambda b, g, i: (b, g, 0, 0)),
            pl.BlockSpec((1, seq_len, LANES), lambda b, g, i: (g, b, 0)),
            pl.BlockSpec((nq, HEAD_DIM, LANES), lambda b, g, i: (b, g, 0)),
            pl.BlockSpec((1, seq_len, LANES), lambda b, g, i: (g, b, 0)),
            pl.BlockSpec((nq, HEAD_DIM, LANES), lambda b, g, i: (b, g, 0)),
            pl.BlockSpec(a.shape, lambda b, g, i: (0, 0)),
        ],
        out_specs=pl.BlockSpec((Q_BLOCK, HPG * HEAD_DIM), lambda b, g, i: (b * nq + i, g)),
        out_shape=jax.ShapeDtypeStruct((n, D_Q), BF16),
        compiler_params=_cparams(("parallel", "parallel", "arbitrary")), name="nsa_prompt",
    )(qt, gnt, kc, vct, kas, vts, kaw, vtw, a)


S5_L = 16
S5_W = S5_L * S5_CH
S5_P = 2 * S5_STATE


def _s5_prep_kernel(are_ref, aim_ref, ldt_ref, bre_ref, bim_ref, cre_ref, cim_ref,
                    tg_ref, sg_ref, ogt_ref, misc_ref, bs_ref, oct_ref):
    are, aim = are_ref[0], aim_ref[0]
    dt = jnp.exp(ldt_ref[0])
    mag = jnp.exp(are * dt)
    ar, ai = mag * jnp.cos(aim * dt), mag * jnp.sin(aim * dt)
    den = are * are + aim * aim
    fr = ((ar - 1.0) * are + ai * aim) / den
    fi = (ai * are - (ar - 1.0) * aim) / den
    bre, bim = bre_ref[0], bim_ref[0]
    cre, cim = cre_ref[0], cim_ref[0]
    br, bi = fr * bre - fi * bim, fr * bim + fi * bre
    lo16 = lax.broadcasted_iota(jnp.int32, (S5_CH, S5_P), 1) < S5_STATE
    lo1 = lax.broadcasted_iota(jnp.int32, (1, S5_P), 1) < S5_STATE
    pr, pi = [jnp.ones_like(ar)], [jnp.zeros_like(ar)]
    for _ in range(S5_L):
        pr.append(pr[-1] * ar - pi[-1] * ai)
        pi.append(pr[-2] * ai + pi[-1] * ar)
    cpr = [cre * pr[k] - cim * pi[k] for k in range(S5_L + 1)]
    cpi = [cre * pi[k] + cim * pr[k] for k in range(S5_L + 1)]
    rpack = jnp.concatenate([jnp.where(lo16, cpr[k], cpi[k]) for k in range(S5_L)], axis=0)
    bpack = jnp.where(lo16, br, -bi)
    krow = lax.dot_general(bpack, rpack, (((1,), (1,)), ((), ())), preferred_element_type=F32,
                           precision=lax.Precision.HIGHEST)
    lane = lax.broadcasted_iota(jnp.int32, (S5_CH, S5_W), 1)
    for j in range(S5_L):
        shifted = krow if j == 0 else pltpu.roll(krow, j * S5_CH, 1)
        tg_ref[0, j * S5_CH:(j + 1) * S5_CH, :] = jnp.where(lane >= j * S5_CH, shifted, 0.0).astype(BF16)
        k = S5_L - 1 - j
        sblk = jnp.where(lo16, pr[k] * br - pi[k] * bi, pr[k] * bi + pi[k] * br)
        sg_ref[0, j * S5_CH:(j + 1) * S5_CH, :] = sblk.astype(BF16)
        if j == S5_L - 1:
            bs_ref[0] = sblk
        ogt_ref[0, j * S5_CH:(j + 1) * S5_CH, :] = jnp.where(lo16, cpr[j + 1], -cpi[j + 1]).astype(BF16)
    oct_ref[0] = jnp.where(lo16, cre, -cim)
    misc_ref[0] = jnp.concatenate([
        pr[S5_L], jnp.where(lo1, -pi[S5_L], pi[S5_L]), ar, jnp.where(lo1, -ai, ai),
        jnp.zeros((4, S5_P), F32)], axis=0)


def _s5_prep(a_re, a_im, log_dt, b_re, b_im, c_re, c_im):
    g = S5_GROUPS
    dup = lambda a: jnp.concatenate([a, a], axis=-1)
    are, aim = dup(a_re).reshape(g, 1, S5_P), dup(a_im).reshape(g, 1, S5_P)
    ldt = jnp.broadcast_to(log_dt.reshape(g, 1, 1), (g, 1, S5_P))
    bre, bim = dup(jnp.swapaxes(b_re, 1, 2)), dup(jnp.swapaxes(b_im, 1, 2))
    cre, cim = dup(c_re), dup(c_im)
    v1 = pl.BlockSpec((1, 1, S5_P), lambda i: (i, 0, 0))
    v16 = pl.BlockSpec((1, S5_CH, S5_P), lambda i: (i, 0, 0))
    return pl.pallas_call(
        _s5_prep_kernel, grid=(g,),
        in_specs=[v1, v1, v1, v16, v16, v16, v16],
        out_specs=(pl.BlockSpec((1, S5_W, S5_W), lambda i: (i, 0, 0)), pl.BlockSpec((1, S5_W, S5_P), lambda i: (i, 0, 0)),
                   pl.BlockSpec((1, S5_W, S5_P), lambda i: (i, 0, 0)), pl.BlockSpec((1, 8, S5_P), lambda i: (i, 0, 0)),
                   v16, v16),
        out_shape=(jax.ShapeDtypeStruct((g, S5_W, S5_W), BF16), jax.ShapeDtypeStruct((g, S5_W, S5_P), BF16),
                   jax.ShapeDtypeStruct((g, S5_W, S5_P), BF16), jax.ShapeDtypeStruct((g, 8, S5_P), F32),
                   jax.ShapeDtypeStruct((g, S5_CH, S5_P), F32), jax.ShapeDtypeStruct((g, S5_CH, S5_P), F32)),
        compiler_params=_cparams(("parallel",)), name="s5_prep",
    )(are, aim, ldt, bre, bim, cre, cim)


def _s5_sum_kernel(u_ref, sg_ref, s_ref):
    s_ref[0] = jnp.dot(u_ref[0].astype(BF16), sg_ref[0], preferred_element_type=F32)


def _s5_scan_kernel(s_ref, a1_ref, a2_ref, h_ref, last_ref, carry):
    @pl.when(pl.program_id(0) == 0)
    def _():
        carry[...] = jnp.zeros_like(carry)

    a1, a2 = a1_ref[...], a2_ref[...]

    def step(c, h):
        h_ref[c] = h
        return a1 * h + a2 * pltpu.roll(h, S5_STATE, 1) + s_ref[c]

    h = lax.fori_loop(0, s_ref.shape[0], step, carry[...])
    carry[...] = h
    last_ref[...] = h


def _s5_out_kernel(u_ref, h_ref, tg_ref, ogt_ref, d_ref, y_ref):
    u = u_ref[0]
    y = jnp.dot(u.astype(BF16), tg_ref[0], preferred_element_type=F32)
    y = y + lax.dot_general(h_ref[0].astype(BF16), ogt_ref[0], (((1,), (1,)), ((), ())), preferred_element_type=F32)
    y_ref[0] = y + d_ref[0] * u


def _s5_prompt(u, ops, s5_d, bsz, seq_len):
    tg, sg, ogt, misc = ops[0], ops[1], ops[2], ops[3]
    g, nch = S5_GROUPS, seq_len // S5_L
    rows = bsz * nch
    ug = u.reshape(bsz, nch, S5_L, g, S5_CH).transpose(3, 0, 1, 2, 4).reshape(g, rows, S5_W)
    gspec = lambda r, c: pl.BlockSpec((1, r, c), lambda i: (i, 0, 0))
    ssum = pl.pallas_call(
        _s5_sum_kernel, grid=(g,), in_specs=[gspec(rows, S5_W), gspec(S5_W, S5_P)],
        out_specs=gspec(rows, S5_P), out_shape=jax.ShapeDtypeStruct((g, rows, S5_P), F32),
        compiler_params=_cparams(("parallel",)), name="s5_sum",
    )(ug, sg)
    s_cm = ssum.reshape(g, bsz, nch, S5_P).transpose(2, 1, 0, 3).reshape(nch, bsz * g, S5_P)
    a1 = jnp.tile(misc[:, 0, :], (bsz, 1))
    a2 = jnp.tile(misc[:, 1, :], (bsz, 1))
    cb = min(nch, 32)
    hs, last = pl.pallas_call(
        _s5_scan_kernel, grid=(nch // cb,),
        in_specs=[pl.BlockSpec((cb, bsz * g, S5_P), lambda i: (i, 0, 0)),
                  pl.BlockSpec((bsz * g, S5_P), lambda i: (0, 0)), pl.BlockSpec((bsz * g, S5_P), lambda i: (0, 0))],
        out_specs=(pl.BlockSpec((cb, bsz * g, S5_P), lambda i: (i, 0, 0)), pl.BlockSpec((bsz * g, S5_P), lambda i: (0, 0))),
        out_shape=(jax.ShapeDtypeStruct((nch, bsz * g, S5_P), F32), jax.ShapeDtypeStruct((bsz * g, S5_P), F32)),
        scratch_shapes=[pltpu.VMEM((bsz * g, S5_P), F32)],
        compiler_params=_cparams(("arbitrary",)), name="s5_scan",
    )(s_cm, a1, a2)
    h_g = hs.reshape(nch, bsz, g, S5_P).transpose(2, 1, 0, 3).reshape(g, rows, S5_P)
    dvec = jnp.tile(s5_d.reshape(g, 1, S5_CH), (1, 1, S5_L))
    yg = pl.pallas_call(
        _s5_out_kernel, grid=(g,),
        in_specs=[gspec(rows, S5_W), gspec(rows, S5_P), gspec(S5_W, S5_W), gspec(S5_W, S5_P), gspec(1, S5_W)],
        out_specs=gspec(rows, S5_W), out_shape=jax.ShapeDtypeStruct((g, rows, S5_W), F32),
        compiler_params=_cparams(("parallel",)), name="s5_out",
    )(ug, h_g, tg, ogt, dvec)
    y = yg.reshape(g, bsz, nch, S5_L, S5_CH).transpose(1, 2, 3, 0, 4).reshape(bsz * seq_len, S5_WIDTH)
    state = last.reshape(bsz, g, 2, S5_STATE).transpose(0, 1, 3, 2)
    return y, state


def _s5_sample_kernel(u_ref, h0_ref, bs_ref, oct_ref, misc_ref, d_ref, y_ref, h1_ref):
    u, h0 = u_ref[0], h0_ref[0]
    hi = lax.Precision.HIGHEST
    bu = jnp.dot(u, bs_ref[0], preferred_element_type=F32, precision=hi)
    h1 = misc_ref[0, 2:3] * h0 + misc_ref[0, 3:4] * pltpu.roll(h0, S5_STATE, 1) + bu
    h1_ref[0] = h1
    y = lax.dot_general(h1, oct_ref[0], (((1,), (1,)), ((), ())), preferred_element_type=F32, precision=hi)
    y_ref[0] = y + d_ref[0] * u


def _s5_sample(u, state, ops, s5_d):
    misc, bs, oct_ = ops[3], ops[4], ops[5]
    s, g = u.shape[0], S5_GROUPS
    ug = u.reshape(s, g, S5_CH).transpose(1, 0, 2)
    h0 = state.astype(F32).transpose(1, 0, 3, 2).reshape(g, s, S5_P)
    gspec = lambda r, c: pl.BlockSpec((1, r, c), lambda i: (i, 0, 0))
    y, h1 = pl.pallas_call(
        _s5_sample_kernel, grid=(g,),
        in_specs=[gspec(s, S5_CH), gspec(s, S5_P), gspec(S5_CH, S5_P), gspec(S5_CH, S5_P), gspec(8, S5_P), gspec(1, S5_CH)],
        out_specs=(gspec(s, S5_CH), gspec(s, S5_P)),
        out_shape=(jax.ShapeDtypeStruct((g, s, S5_CH), F32), jax.ShapeDtypeStruct((g, s, S5_P), F32)),
        compiler_params=_cparams(("parallel",)), name="s5_sample",
    )(ug, h0, bs, oct_, misc, s5_d.reshape(g, 1, S5_CH))
    return (y.transpose(1, 0, 2).reshape(s, S5_WIDTH),
            h1.reshape(g, s, 2, S5_STATE).transpose(1, 0, 3, 2))


def _merge_mlp_kernel(x_ref, o_ref, ys_ref, gm_ref, wglu_ref, bglu_ref, wbn_ref, wbs_ref, wout_ref, wup_ref, wdn_ref,
                      npost_ref, nmpre_ref, nmpost_ref, out_ref):
    dot = lambda a, w_ref: jnp.dot(a.astype(BF16), w_ref[...], preferred_element_type=F32)
    z = _gelu(ys_ref[...])
    o_s5 = z * _sigmoid(dot(z, wglu_ref) + bglu_ref[...])
    merged = (_sigmoid(gm_ref[:, 0:D_MODEL]) * dot(o_ref[...], wbn_ref)
              + _sigmoid(gm_ref[:, D_MODEL:2 * D_MODEL]) * dot(o_s5, wbs_ref))
    x1 = x_ref[...] + _rms(dot(merged, wout_ref), npost_ref[...])
    hm = _rms(x1, nmpre_ref[...])
    up = jnp.maximum(dot(hm, wup_ref), 0.0)
    f = dot(up * up, wdn_ref)
    out_ref[...] = x1 + _rms(f, nmpost_ref[...])


def _merge_mlp(x2, o_nsa, y_s5, gm, wts, tm=256):
    n = x2.shape[0]
    tm = min(tm, n)
    row = lambda w: pl.BlockSpec((tm, w), lambda i: (i, 0))
    const = lambda a: pl.BlockSpec(a.shape, lambda i: (0, 0), pipeline_mode=pl.Buffered(1))
    return pl.pallas_call(
        _merge_mlp_kernel, grid=(n // tm,),
        in_specs=[row(D_MODEL), row(D_Q), row(S5_WIDTH), row(2 * D_MODEL)] + [const(a) for a in wts],
        out_specs=row(D_MODEL), out_shape=jax.ShapeDtypeStruct((n, D_MODEL), F32),
        compiler_params=_cparams(("parallel",)), name="merge_mlp",
    )(x2, o_nsa, y_s5, gm, *wts)


def _merge_weights(s5_w_glu, s5_b_glu, w_branch_nsa, w_branch_s5, w_out, w_mlp_up, w_mlp_down,
                   norm_mix_post, norm_mlp_pre, norm_mlp_post):
    r = lambda v: v.reshape(1, -1).astype(F32)
    b = lambda w: w.astype(BF16)
    return (b(s5_w_glu), r(s5_b_glu), b(w_branch_nsa), b(w_branch_s5), b(w_out), b(w_mlp_up), b(w_mlp_down),
            r(norm_mix_post), r(norm_mlp_pre), r(norm_mlp_post))


def _inproj_sample_kernel(x_ref, g_ref, wr_ref, wt_ref, kv_ref, u_ref, gm_ref, qg_ref):
    hb = _rms(x_ref[...], g_ref[...]).astype(BF16)
    kv_ref[...] = jnp.dot(hb, wr_ref[:, 0:1536], preferred_element_type=F32)
    u_ref[...] = jnp.dot(hb, wr_ref[:, 1536:2048], preferred_element_type=F32)
    gm_ref[...] = jnp.dot(hb, wr_ref[:, 2048:4096], preferred_element_type=F32)
    rows = jnp.concatenate([wt_ref[0:1024, :], wt_ref[1536:1600, :]], axis=0)
    qg_ref[...] = lax.dot_general(hb, rows, (((1,), (1,)), ((), ())), preferred_element_type=F32)


def _inproj_sample(x2, g_pre, w_row, w_t):
    s = x2.shape[0]
    full = lambda a: pl.BlockSpec(a.shape, lambda i: (0, 0))
    o = lambda w: pl.BlockSpec((s, w), lambda i: (0, 0))
    return pl.pallas_call(
        _inproj_sample_kernel, grid=(1,),
        in_specs=[o(D_MODEL), pl.BlockSpec((1, D_MODEL), lambda i: (0, 0)), full(w_row), full(w_t)],
        out_specs=(o(1536), o(512), o(2048), o(1088)),
        out_shape=(jax.ShapeDtypeStruct((s, 1536), F32), jax.ShapeDtypeStruct((s, 512), F32),
                   jax.ShapeDtypeStruct((s, 2048), F32), jax.ShapeDtypeStruct((s, 1088), F32)),
        compiler_params=_cparams(("arbitrary",)), name="inproj_sample",
    )(x2, g_pre.reshape(1, D_MODEL), w_row, w_t)


CMP_PAGES_PER_STEP = 16


def _cmp_y_sample(cache_cmp, page_table, wcmp):
    n_phys = cache_cmp.shape[0]
    s, n_pages = page_table.shape
    cpp = PAGE_SIZE // CMP_STRIDE
    xc = cache_cmp.reshape(n_phys, cpp, CMP_STRIDE * 512)
    pps = CMP_PAGES_PER_STEP
    steps = n_pages // pps

    def page_spec(r):
        return pl.BlockSpec((1, cpp, CMP_STRIDE * 512), lambda b, j, pt: (pt[b, j * pps + r], 0, 0))

    grid_spec = pltpu.PrefetchScalarGridSpec(
        num_scalar_prefetch=1, grid=(s, steps),
        in_specs=[page_spec(r) for r in range(pps)] + [pl.BlockSpec(wcmp.shape, lambda b, j, pt: (0, 0, 0))],
        out_specs=pl.BlockSpec((1, pps * cpp, 1024), lambda b, j, pt: (b, j, 0)),
    )
    body = functools.partial(_cmp_y_kernel, pps)
    return pl.pallas_call(
        lambda pt, *refs: body(*refs), grid_spec=grid_spec,
        out_shape=jax.ShapeDtypeStruct((s, n_pages * cpp, 1024), F32),
        compiler_params=_cparams(("parallel", "parallel")), name="cmp_y_sample",
    )(page_table, *([xc] * pps), wcmp)


def _nsa_sample_cmp_kernel(nc_valid, qbd_ref, kc_ref, vct_ref, a_ref, oc_ref, isel_ref):
    ncp = kc_ref.shape[2]
    sc = jnp.dot(kc_ref[0, 0], qbd_ref[0, 0], preferred_element_type=F32)
    for g in range(1, N_KV):
        sc = sc + jnp.dot(kc_ref[0, g], qbd_ref[0, g], preferred_element_type=F32)
    nrow = lax.broadcasted_iota(jnp.int32, (ncp, LANES), 0)
    cmask = nrow < nc_valid
    sc = jnp.where(cmask, sc, NEG)
    e = jnp.where(cmask, jnp.exp(sc - jnp.max(sc, axis=0, keepdims=True)), 0.0)
    p = e / jnp.sum(e, axis=0, keepdims=True)
    pb = p.astype(BF16)
    for g in range(N_KV):
        oc_ref[0, g] = jnp.dot(vct_ref[0, g], pb, preferred_element_type=F32)
    a = a_ref[...]
    r = sum(jnp.dot(a, part, preferred_element_type=F32) for part in _split3(p))
    tot = r
    for h in range(1, HPG):
        tot = tot + pltpu.roll(r, LANES - h * N_KV, 1)
    isel_ref[0] = tot


def _topk_sample_kernel(nsel, cur, k_sel, isel_ref, tri_ref, idx_ref):
    jp = isel_ref.shape[0]
    jrow = lax.broadcasted_iota(jnp.int32, (jp, LANES), 0)
    forced = (jrow == 0) | (jrow == cur) | (jrow == cur - 1)
    score = jnp.where(jrow <= cur, jnp.where(forced, FORCE, isel_ref[...]), -1.0)
    score = jnp.where(jrow < nsel, score, -2.0)
    sel = _rank_select(score, k_sel)
    rank = jnp.dot(tri_ref[...], sel.astype(BF16), preferred_element_type=F32)
    jf = jrow.astype(F32)
    rows = [jnp.sum(jnp.where((sel > 0.5) & (rank == float(r + 1)), jf, 0.0), axis=0, keepdims=True)
            for r in range(k_sel)]
    idx_ref[...] = jnp.concatenate(rows, axis=0).astype(jnp.int32)


def _nsa_sample_attn_kernel(n_cache, tbl_ref, q_ref, k0_ref, k1_ref, k2_ref, k3_ref, kvs_ref, win_ref, kvw_ref,
                            oc_ref, gate_ref, o_ref, wout_ref, m_sc, l_sc, acc_sc):
    r = pl.program_id(1)
    blocks = (k0_ref, k1_ref, k2_ref, k3_ref)

    @pl.when(r == 0)
    def _():
        m_sc[...] = jnp.full(m_sc.shape, NEG, F32)
        l_sc[...] = jnp.zeros(l_sc.shape, F32)
        acc_sc[...] = jnp.zeros(acc_sc.shape, F32)

    nt = (((1,), (1,)), ((), ()))
    for g in range(N_KV):
        col = (g // 2) * LANES
        q = q_ref[0, g]
        kb = blocks[g][0, :, col:col + LANES].astype(BF16)
        vb = blocks[g][0, :, 256 + col:256 + col + LANES].astype(BF16)
        s = lax.dot_general(q, kb, nt, preferred_element_type=F32)
        m_old = m_sc[g]
        m_new = jnp.maximum(m_old, jnp.max(s, axis=1, keepdims=True))
        alpha = jnp.exp(m_old - m_new)
        p = jnp.exp(s - m_new)
        l_sc[g] = l_sc[g] * alpha + jnp.sum(p, axis=1, keepdims=True)
        acc_sc[g] = acc_sc[g] * alpha + jnp.dot(p.astype(BF16), vb, preferred_element_type=F32)
        m_sc[g] = m_new

    @pl.when(r == n_cache - 1)
    def _():
        win = win_ref[0]
        kvw_new = kvw_ref[0]
        kvs_new = kvs_ref[0]
        nwin = win.shape[0]
        shifted = pltpu.roll(win, nwin - 1, 0)
        last = lax.broadcasted_iota(jnp.int32, win.shape, 0) == nwin - 1
        wout_ref[0] = jnp.where(last, kvw_new, shifted)
        for g in range(N_KV):
            col = (g // 2) * LANES
            q = q_ref[0, g]
            qf = q.astype(F32)
            kn = kvs_new[:, col:col + LANES].astype(BF16).astype(F32)
            vn = kvs_new[:, 256 + col:256 + col + LANES].astype(BF16).astype(F32)
            s_new = jnp.sum(qf * kn, axis=1, keepdims=True)
            m_old = m_sc[g]
            m_new = jnp.maximum(m_old, s_new)
            alpha = jnp.exp(m_old - m_new)
            p_new = jnp.exp(s_new - m_new)
            l_s = l_sc[g] * alpha + p_new
            o_s = (acc_sc[g] * alpha + p_new.astype(BF16).astype(F32) * vn) / l_s
            kw = win[:, col:col + LANES].astype(BF16)
            vw = win[:, 256 + col:256 + col + LANES].astype(BF16)
            sw = lax.dot_general(q, kw, nt, preferred_element_type=F32)
            keep = lax.broadcasted_iota(jnp.int32, sw.shape, 1) >= 1
            sw = jnp.where(keep, sw, NEG)
            kwn = kvw_new[:, col:col + LANES].astype(BF16).astype(F32)
            vwn = kvw_new[:, 256 + col:256 + col + LANES].astype(BF16).astype(F32)
            sw_new = jnp.sum(qf * kwn, axis=1, keepdims=True)
            mw = jnp.maximum(jnp.max(sw, axis=1, keepdims=True), sw_new)
            pw = jnp.where(keep, jnp.exp(sw - mw), 0.0)
            pw_new = jnp.exp(sw_new - mw)
            lw = jnp.sum(pw, axis=1, keepdims=True) + pw_new
            o_w = (jnp.dot(pw.astype(BF16), vw, preferred_element_type=F32)
                   + pw_new.astype(BF16).astype(F32) * vwn) / lw
            gate = _sigmoid(gate_ref[0, g])
            o_ref[0, g] = gate[0] * oc_ref[0, g] + gate[1] * o_s + gate[2] * o_w


def sample_attention(x_sample, cache_kv_cmp, cache_kv_slc, state_kv_win, page_table, norm_mix_pre, w_row, w_t,
                     wcmp, cmp_pe_k, cmp_w1_k, cmp_w2_k, cmp_pe_v, cmp_w1_v, cmp_w2_v):
    s = x_sample.shape[0]
    n_pages = page_table.shape[1]
    past = n_pages * PAGE_SIZE
    assert past % SEL_BLOCK == 0 and x_sample.shape[1] == 1
    kv, u, gm, qg = _inproj_sample(x_sample.reshape(s, D_MODEL), norm_mix_pre, w_row, w_t)
    q = qg[:, 0:D_Q].reshape(s, N_KV, HPG, HEAD_DIM) * (HEAD_DIM ** -0.5)
    gn = qg[:, D_Q:].reshape(s, N_KV, 16)[:, :, 0:12].reshape(s, N_KV, HPG, 3)
    kvc_new, kvs_new, kvw_new = kv[:, 0:512], kv[:, 512:1024], kv[:, 1024:1536]

    y = _cmp_y_sample(cache_kv_cmp, page_table, wcmp)
    kc, vct = _cmp_combine(y, cmp_pe_k, cmp_w1_k, cmp_w2_k, cmp_pe_v, cmp_w1_v, cmp_w2_v)
    ncp = kc.shape[2]
    nc_valid = (past + 1) // CMP_STRIDE - 1
    nsel = -(-(past + 1) // SEL_BLOCK)
    jp = -(-nsel // 8) * 8
    cur = past // SEL_BLOCK
    k_sel = min(N_SELECT, nsel)
    assert ncp >= nc_valid and (nc_valid - 1) * CMP_STRIDE + CMP_BLOCK - 1 <= past

    half = (jnp.arange(N_KV) % 2)[:, None] * HEAD_DIM + jnp.arange(HEAD_DIM)[None, :]
    qb16 = q.astype(BF16)
    qbd = jnp.zeros((s, N_KV, LANES, LANES), BF16)
    gidx = jnp.arange(N_KV)
    lane_gh = gidx[:, None] + jnp.arange(HPG)[None, :] * N_KV
    qbd = qbd.at[:, gidx[:, None, None], jnp.arange(HEAD_DIM)[None, None, :], lane_gh[:, :, None]].set(qb16)
    qrow = jnp.zeros((s, N_KV, 8, LANES), BF16)
    qrow = qrow.at[:, gidx[:, None, None], jnp.arange(HPG)[None, :, None], half[:, None, :]].set(qb16)

    a = jnp.asarray(np.pad(_sel_matrix(nsel, nc_valid, ncp), ((0, jp - nsel), (0, 0))), BF16)
    oc_t, isel = pl.pallas_call(
        functools.partial(_nsa_sample_cmp_kernel, nc_valid), grid=(s,),
        in_specs=[pl.BlockSpec((1, N_KV, LANES, LANES), lambda b: (b, 0, 0, 0)),
                  pl.BlockSpec((1, N_KV, ncp, LANES), lambda b: (b, 0, 0, 0)),
                  pl.BlockSpec((1, N_KV, HEAD_DIM, ncp), lambda b: (b, 0, 0, 0)),
                  pl.BlockSpec(a.shape, lambda b: (0, 0))],
        out_specs=(pl.BlockSpec((1, N_KV, HEAD_DIM, LANES), lambda b: (b, 0, 0, 0)),
                   pl.BlockSpec((1, jp, LANES), lambda b: (b, 0, 0))),
        out_shape=(jax.ShapeDtypeStruct((s, N_KV, HEAD_DIM, LANES), F32), jax.ShapeDtypeStruct((s, jp, LANES), F32)),
        compiler_params=_cparams(("parallel",)), name="nsa_sample_cmp",
    )(qbd, kc, vct, a)

    assert s * N_KV == LANES
    isel_t = isel[:, :, 0:N_KV].transpose(1, 0, 2).reshape(jp, s * N_KV)
    tri = jnp.asarray(np.tril(np.ones((jp, jp), np.float32)), BF16)
    idx = pl.pallas_call(
        functools.partial(_topk_sample_kernel, nsel, cur, k_sel), grid=(1,),
        in_specs=[pl.BlockSpec((jp, LANES), lambda i: (0, 0)), pl.BlockSpec((jp, jp), lambda i: (0, 0))],
        out_specs=pl.BlockSpec((k_sel, LANES), lambda i: (0, 0)),
        out_shape=jax.ShapeDtypeStruct((k_sel, LANES), jnp.int32),
        compiler_params=_cparams(("arbitrary",)), name="topk_sample",
    )(isel_t, tri)
    n_cache = k_sel - 1
    blk = idx[0:n_cache].T.reshape(s, N_KV, n_cache)
    page = jnp.take_along_axis(page_table, (blk // 2).reshape(s, -1), axis=1).reshape(s, N_KV, n_cache)
    tbl = (page * 2 + blk % 2).astype(jnp.int32).reshape(s * N_KV * n_cache)

    slc_blocks = cache_kv_slc.reshape(cache_kv_slc.shape[0] * 2, SEL_BLOCK, 512)
    oc_row = jnp.zeros((s, N_KV, 8, LANES), F32)
    oc_g = oc_t[:, gidx[:, None, None], jnp.arange(HEAD_DIM)[None, None, :], lane_gh[:, :, None]]
    oc_row = oc_row.at[:, gidx[:, None, None], jnp.arange(HPG)[None, :, None], half[:, None, :]].set(oc_g)
    gate_in = jnp.zeros((s, N_KV, 3, 8, LANES), F32)
    gate_in = gate_in.at[:, :, :, 0:HPG, :].set(
        jnp.broadcast_to(gn.transpose(0, 1, 3, 2)[..., None], (s, N_KV, 3, HPG, LANES)))
    wrows = state_kv_win.shape[1]
    win2 = state_kv_win.reshape(s, wrows, 512)

    def blk_spec(g):
        return pl.BlockSpec((1, SEL_BLOCK, 512), lambda b, r, t: (t[(b * N_KV + g) * n_cache + r], 0, 0))

    per_b = lambda shape: pl.BlockSpec((1,) + shape, lambda b, r, t: (b,) + (0,) * len(shape))
    grid_spec = pltpu.PrefetchScalarGridSpec(
        num_scalar_prefetch=1, grid=(s, n_cache),
        in_specs=[per_b((N_KV, 8, LANES))] + [blk_spec(g) for g in range(N_KV)]
                 + [per_b((1, 512)), per_b((wrows, 512)), per_b((1, 512)), per_b((N_KV, 8, LANES)),
                    per_b((N_KV, 3, 8, LANES))],
        out_specs=(per_b((N_KV, 8, LANES)), per_b((wrows, 512))),
        scratch_shapes=[pltpu.VMEM((N_KV, 8, 1), F32), pltpu.VMEM((N_KV, 8, 1), F32), pltpu.VMEM((N_KV, 8, LANES), F32)],
    )
    o_row, win_out = pl.pallas_call(
        functools.partial(_nsa_sample_attn_kernel, n_cache), grid_spec=grid_spec,
        out_shape=(jax.ShapeDtypeStruct((s, N_KV, 8, LANES), F32), jax.ShapeDtypeStruct((s, wrows, 512), F32)),
        compiler_params=_cparams(("parallel", "arbitrary")), name="nsa_sample_attn",
    )(tbl, qrow, slc_blocks, slc_blocks, slc_blocks, slc_blocks, kvs_new.reshape(s, 1, 512), win2,
      kvw_new.reshape(s, 1, 512), oc_row, gate_in)
    o = o_row[:, gidx[:, None, None], jnp.arange(HPG)[None, :, None], half[:, None, :]]
    return o.reshape(s, D_Q).astype(BF16), kvc_new, kvs_new, win_out, u, gm


def prompt_attention(x_prompt, norm_mix_pre, w_row, w_t, wcmp, cmp_pe_k, cmp_w1_k, cmp_w2_k, cmp_pe_v, cmp_w1_v, cmp_w2_v):
    bsz, seq_len, _ = x_prompt.shape
    assert seq_len % (2 * Q_BLOCK) == 0 and WIN_KEYS <= seq_len <= SEL_BLOCK * HEAD_DIM
    n = bsz * seq_len
    (kvc, kvs, kvw, u, gm, kas, kaw, qt, vts, vtw, gnt) = _inproj_prompt(
        x_prompt.reshape(n, D_MODEL), norm_mix_pre, w_row, w_t, seq_len)
    y = _cmp_y_prompt(kvc, bsz, seq_len, wcmp)
    kc, vct = _cmp_combine(y, cmp_pe_k, cmp_w1_k, cmp_w2_k, cmp_pe_v, cmp_w1_v, cmp_w2_v)
    o = _nsa_prompt(qt, gnt, kc, vct, kas, vts, kaw, vtw, bsz, seq_len)
    return o, kvc, kvs, kvw, u, gm


def kernel(x_prompt, x_sample, cache_kv_cmp, cache_kv_slc, state_kv_win, state_s5, page_table, norm_mix_pre, norm_mix_post, norm_mlp_pre, norm_mlp_post, w_in, cmp_pe_k, cmp_w1_k, cmp_w2_k, cmp_pe_v, cmp_w1_v, cmp_w2_v, s5_a_re, s5_a_im, s5_log_dt, s5_b_re, s5_b_im, s5_c_re, s5_c_im, s5_d, s5_w_glu, s5_b_glu, w_branch_nsa, w_branch_s5, w_out, w_mlp_up, w_mlp_down):
    bsz, seq_len, _ = x_prompt.shape
    s = x_sample.shape[0]
    w_row, w_t = _inproj_weights(w_in)
    wcmp = _cmp_weights(cmp_w1_k, cmp_w1_v)
    cmp_w = (cmp_pe_k, cmp_w1_k, cmp_w2_k, cmp_pe_v, cmp_w1_v, cmp_w2_v)
    s5_ops = _s5_prep(s5_a_re, s5_a_im, s5_log_dt, s5_b_re, s5_b_im, s5_c_re, s5_c_im)
    mlp_w = _merge_weights(s5_w_glu, s5_b_glu, w_branch_nsa, w_branch_s5, w_out, w_mlp_up, w_mlp_down,
                           norm_mix_post, norm_mlp_pre, norm_mlp_post)

    o_p, kvc_p, kvs_p, kvw_p, u_p, gm_p = prompt_attention(x_prompt, norm_mix_pre, w_row, w_t, wcmp, *cmp_w)
    ys5_p, s5_p = _s5_prompt(u_p, s5_ops, s5_d, bsz, seq_len)
    y_p = _merge_mlp(x_prompt.reshape(bsz * seq_len, D_MODEL), o_p, ys5_p, gm_p, mlp_w)

    o_s, kvc_s, kvs_s, win_s, u_s, gm_s = sample_attention(
        x_sample, cache_kv_cmp, cache_kv_slc, state_kv_win, page_table, norm_mix_pre, w_row, w_t, wcmp, *cmp_w)
    ys5_s, s5_s = _s5_sample(u_s, state_s5, s5_ops, s5_d)
    y_s = _merge_mlp(x_sample.reshape(s, D_MODEL), o_s, ys5_s, gm_s, mlp_w)

    kv5 = lambda a, b, t: a.reshape(b, t, 2, N_KV, HEAD_DIM)
    win_rows = min(WINDOW, seq_len)
    win_p = kv5(kvw_p, bsz, seq_len)[:, seq_len - win_rows:]
    if win_rows < WINDOW:
        win_p = jnp.pad(win_p, ((0, 0), (WINDOW - win_rows, 0), (0, 0), (0, 0), (0, 0)))
    return (y_p.reshape(bsz, seq_len, D_MODEL), y_s.reshape(s, 1, D_MODEL),
            kv5(kvc_p, bsz, seq_len), kv5(kvs_p, bsz, seq_len), win_p, s5_p.astype(x_prompt.dtype),
            kv5(kvc_s, s, 1), kv5(kvs_s, s, 1), kv5(win_s, s, state_kv_win.shape[1]), s5_s.astype(state_s5.dtype))
```

```python
import functools
import math

import numpy as np
import jax
import jax.numpy as jnp
from jax import lax
from jax.experimental import pallas as pl
from jax.experimental.pallas import tpu as pltpu

F32 = jnp.float32
BF16 = jnp.bfloat16

D_MODEL = 1024
HEAD_DIM = 64
N_HEADS = 16
N_KV = 4
HPG = 4
CMP_STRIDE = 16
CMP_BLOCK = 32
SEL_BLOCK = 64
N_SELECT = 16
WINDOW = 512
Q_BLOCK = 128
S5_WIDTH = 512
S5_CH = 16
S5_GROUPS = 32
S5_STATE = 64
D_FF = 4096
D_Q = 1024
D_KV = 256
PAGE_SIZE = 128
EPS = 1e-6
NEG = -1e30
FORCE = 1e4
LANES = 128
VMEM_LIMIT = 56 * 1024 * 1024
KV_CHUNK = 256
WIN_KEYS = WINDOW + Q_BLOCK


def _cparams(sem):
    return pltpu.CompilerParams(dimension_semantics=sem, vmem_limit_bytes=VMEM_LIMIT)


def _gelu(x):
    return 0.5 * x * (1.0 + jnp.tanh(math.sqrt(2.0 / math.pi) * (x + 0.044715 * (x * x * x))))


def _sigmoid(x):
    return 1.0 / (1.0 + jnp.exp(-x))


def _rms(x, g):
    ms = jnp.mean(x * x, axis=-1, keepdims=True)
    return (x * lax.rsqrt(ms + EPS)) * g


def _split3(x):
    hi = x.astype(BF16)
    r1 = x - hi.astype(F32)
    mid = r1.astype(BF16)
    lo = (r1 - mid.astype(F32)).astype(BF16)
    return hi, mid, lo


WT_Q, WT_GN, WT_KV = 0, D_Q, D_Q + 64
WT_ROWS = WT_KV + 6 * D_KV


def _inproj_prompt_kernel(seq_len, x_ref, g_ref, wr_ref, wt_ref,
                          u_ref, gm_ref, kas_ref, kaw_ref, slab_ref,
                          qt_ref, vts_ref, vtw_ref, gnt_ref, kvct_ref, kvst_ref, kvwt_ref):
    tm = x_ref.shape[0]
    hb = _rms(x_ref[...], g_ref[...]).astype(BF16)

    def rowdot(lo, hi):
        return jnp.dot(hb, wr_ref[:, lo:hi], preferred_element_type=F32)

    zc = rowdot(0, 512)
    for combo in range(4):
        slab_ref[combo] = zc[:, combo * LANES:(combo + 1) * LANES]
    u_ref[...] = rowdot(512, 1024)
    gm_ref[...] = rowdot(1024, 3072)
    row = pl.program_id(0) * tm + lax.broadcasted_iota(jnp.int32, (tm, LANES), 0)
    blk = lax.rem(row, seq_len) // SEL_BLOCK
    lane = lax.broadcasted_iota(jnp.int32, (tm, LANES), 1)
    onehot = jnp.where(lane - HEAD_DIM == blk, 1.0, 0.0)
    zs = rowdot(3072, 3584)
    zw = rowdot(3584, 4096)
    for g in range(N_KV):
        kas_ref[g] = (zs[:, g * LANES:(g + 1) * LANES] + onehot).astype(BF16)
        kaw_ref[g] = zw[:, g * LANES:(g + 1) * LANES].astype(BF16)
    zt = lax.dot_general(wt_ref[...], hb, (((1,), (1,)), ((), ())), preferred_element_type=F32)
    kv0 = WT_KV
    kvct_ref[0] = zt[kv0:kv0 + 512]
    kvst_ref[0] = zt[kv0 + 512:kv0 + 1024]
    kvwt_ref[0] = zt[kv0 + 1024:kv0 + 1536]
    for c in range(tm // LANES):
        sl = slice(c * LANES, (c + 1) * LANES)
        qt_ref[c] = (zt[WT_Q:WT_Q + D_Q, sl] * (HEAD_DIM ** -0.5)).astype(BF16)
        gnt_ref[c] = zt[WT_GN:WT_GN + 64, sl]
        vts_ref[c] = zt[kv0 + 768:kv0 + 1024, sl].astype(BF16)
        vtw_ref[c] = zt[kv0 + 1280:kv0 + 1536, sl].astype(BF16)


def _inproj_weights(w_in):
    wq, wkv, wgn, wu, wgm = (w_in[:, :1024], w_in[:, 1024:2560], w_in[:, 2560:2608],
                             w_in[:, 2608:3120], w_in[:, 3120:])
    wkv6 = wkv.reshape(D_MODEL, 3, 2, N_KV, HEAD_DIM)
    zpad = jnp.zeros((D_MODEL, N_KV, HEAD_DIM), F32)
    kaug_s = jnp.concatenate([wkv6[:, 1, 0], zpad], axis=-1).reshape(D_MODEL, N_KV * LANES)
    kaug_w = jnp.concatenate([wkv6[:, 2, 0], zpad], axis=-1).reshape(D_MODEL, N_KV * LANES)
    w_row = jnp.concatenate([wkv[:, 0:512], wu, wgm, kaug_s, kaug_w], axis=1).astype(BF16)
    gn_rows = jnp.pad(wgn.T.reshape(N_KV, HPG * 3, D_MODEL), ((0, 0), (0, 4), (0, 0))).reshape(64, D_MODEL)
    w_t = jnp.concatenate([wq.T, gn_rows, wkv.T], axis=0).astype(BF16)
    return w_row, w_t


def _inproj_prompt(x2, g_pre, w_row, w_t, bsz, seq_len, tm=256):
    n = x2.shape[0]
    nc = n // LANES
    cpt = tm // LANES
    per = seq_len // tm
    row = lambda w: pl.BlockSpec((tm, w), lambda i: (i, 0))
    fmaj = pl.BlockSpec((1, 512, tm), lambda i: (i // per, 0, i % per))
    out_shape = (
        jax.ShapeDtypeStruct((n, 512), F32), jax.ShapeDtypeStruct((n, 2048), F32),
        jax.ShapeDtypeStruct((N_KV, n, LANES), BF16), jax.ShapeDtypeStruct((N_KV, n, LANES), BF16),
        jax.ShapeDtypeStruct((4, n, LANES), F32),
        jax.ShapeDtypeStruct((nc, 1024, LANES), BF16), jax.ShapeDtypeStruct((nc, 256, LANES), BF16),
        jax.ShapeDtypeStruct((nc, 256, LANES), BF16), jax.ShapeDtypeStruct((nc, 64, LANES), F32),
        jax.ShapeDtypeStruct((bsz, 512, seq_len), F32), jax.ShapeDtypeStruct((bsz, 512, seq_len), F32),
        jax.ShapeDtypeStruct((bsz, 512, seq_len), F32),
    )
    out_specs = (
        row(512), row(2048),
        pl.BlockSpec((N_KV, tm, LANES), lambda i: (0, i, 0)), pl.BlockSpec((N_KV, tm, LANES), lambda i: (0, i, 0)),
        pl.BlockSpec((4, tm, LANES), lambda i: (0, i, 0)),
        pl.BlockSpec((cpt, 1024, LANES), lambda i: (i, 0, 0)), pl.BlockSpec((cpt, 256, LANES), lambda i: (i, 0, 0)),
        pl.BlockSpec((cpt, 256, LANES), lambda i: (i, 0, 0)), pl.BlockSpec((cpt, 64, LANES), lambda i: (i, 0, 0)),
        fmaj, fmaj, fmaj,
    )
    return pl.pallas_call(
        functools.partial(_inproj_prompt_kernel, seq_len),
        grid=(n // tm,),
        in_specs=[row(D_MODEL), pl.BlockSpec((1, D_MODEL), lambda i: (0, 0)),
                  pl.BlockSpec(w_row.shape, lambda i: (0, 0)), pl.BlockSpec(w_t.shape, lambda i: (0, 0))],
        out_specs=out_specs, out_shape=out_shape,
        compiler_params=_cparams(("parallel",)), name="inproj_prompt",
    )(x2, g_pre.reshape(1, D_MODEL), w_row, w_t)


def _cmp_weights(w1_k, w1_v):
    eye2 = jnp.eye(2, dtype=F32)
    out = []
    for w1 in (w1_k, w1_v):
        w = w1.reshape(2, CMP_STRIDE, HEAD_DIM, HEAD_DIM)
        big = jnp.einsum('fsdh,ij->sidfjh', w, eye2).reshape(CMP_STRIDE * 2 * HEAD_DIM, 2 * 2 * HEAD_DIM)
        out += [big, big]
    return jnp.stack(out).astype(BF16)


def _cmp_y_from_slabs(slab_ref, w_ref, y_ref):
    nrows = y_ref.shape[1]
    for combo in range(4):
        xg = jnp.concatenate([slab_ref[combo, pl.ds(s, nrows, stride=CMP_STRIDE), :] for s in range(CMP_STRIDE)],
                             axis=1)
        y_ref[0, :, combo * 256:(combo + 1) * 256] = jnp.dot(xg.astype(BF16), w_ref[combo],
                                                             preferred_element_type=F32)


def _cmp_y_prompt(slabs, bsz, seq_len, wcmp):
    nch = seq_len // CMP_STRIDE
    rb = min(nch, 128)
    per = nch // rb
    return pl.pallas_call(
        _cmp_y_from_slabs,
        grid=(bsz, per),
        in_specs=[pl.BlockSpec((4, rb * CMP_STRIDE, LANES), lambda b, j: (0, b * per + j, 0)),
                  pl.BlockSpec(wcmp.shape, lambda b, j: (0, 0, 0))],
        out_specs=pl.BlockSpec((1, rb, 1024), lambda b, j: (b, j, 0)),
        out_shape=jax.ShapeDtypeStruct((bsz, nch, 1024), F32),
        compiler_params=_cparams(("parallel", "parallel")), name="cmp_y_prompt",
    )(slabs, wcmp)


def _cmp_combine_kernel(y_ref, pe_ref, w1f_ref, w2k_ref, w2vt_ref, kc_ref, vct_ref):
    r = y_ref.shape[1]
    pos = jnp.dot(pe_ref[...], w1f_ref[...], preferred_element_type=F32,
                  precision=lax.Precision.HIGHEST)
    for combo in range(4):
        kv, gp = combo // 2, combo % 2
        first = y_ref[0, :, combo * 256: combo * 256 + LANES]
        second = pltpu.roll(y_ref[0, :, combo * 256 + LANES: combo * 256 + 2 * LANES], r - 1, 0)
        p1 = pos[0:1, kv * HEAD_DIM:(kv + 1) * HEAD_DIM]
        pre = first + second + jnp.concatenate([p1, p1], axis=1)
        act = _gelu(pre).astype(BF16)
        if kv == 0:
            kc = jnp.dot(act, w2k_ref[...], preferred_element_type=F32)
            kc_ref[0, 2 * gp] = kc[:, 0:LANES].astype(BF16)
            kc_ref[0, 2 * gp + 1] = kc[:, LANES:2 * LANES].astype(BF16)
        else:
            vct = lax.dot_general(w2vt_ref[...], act, (((1,), (1,)), ((), ())), preferred_element_type=F32)
            vct_ref[0, 2 * gp] = vct[0:HEAD_DIM].astype(BF16)
            vct_ref[0, 2 * gp + 1] = vct[HEAD_DIM:2 * HEAD_DIM].astype(BF16)


def _cmp_combine(y, pe_k, w1_k, w2_k, pe_v, w1_v, w2_v):
    s, r, _ = y.shape
    pe = jnp.concatenate([pe_k.reshape(1, -1), pe_v.reshape(1, -1)], axis=1)
    pe8 = jnp.pad(pe, ((0, 7), (0, 0)))
    z = jnp.zeros((CMP_BLOCK * HEAD_DIM, HEAD_DIM), F32)
    w1f = jnp.concatenate([jnp.concatenate([w1_k.reshape(-1, HEAD_DIM), z], axis=1),
                           jnp.concatenate([z, w1_v.reshape(-1, HEAD_DIM)], axis=1)], axis=0)
    z64 = jnp.zeros((HEAD_DIM, HEAD_DIM), F32)
    w2k = jnp.concatenate([jnp.concatenate([w2_k, z64, z64, z64], axis=1),
                           jnp.concatenate([z64, z64, w2_k, z64], axis=1)], axis=0).astype(BF16)
    w2vt = jnp.concatenate([jnp.concatenate([w2_v.T, z64], axis=1),
                            jnp.concatenate([z64, w2_v.T], axis=1)], axis=0).astype(BF16)
    full = lambda a: pl.BlockSpec(a.shape, lambda i: (0,) * a.ndim)
    return pl.pallas_call(
        _cmp_combine_kernel,
        grid=(s,),
        in_specs=[pl.BlockSpec((1, r, 1024), lambda i: (i, 0, 0)), full(pe8), full(w1f), full(w2k), full(w2vt)],
        out_specs=(pl.BlockSpec((1, N_KV, r, LANES), lambda i: (i, 0, 0, 0)),
                   pl.BlockSpec((1, N_KV, HEAD_DIM, r), lambda i: (i, 0, 0, 0))),
        out_shape=(jax.ShapeDtypeStruct((s, N_KV, r, LANES), BF16),
                   jax.ShapeDtypeStruct((s, N_KV, HEAD_DIM, r), BF16)),
        compiler_params=_cparams(("parallel",)), name="cmp_combine",
    )(y, pe8, w1f, w2k, w2vt)


def _sel_matrix(nsel, nc, ncp):
    j = np.arange(nsel)
    lo = np.clip((j * SEL_BLOCK - CMP_BLOCK) // CMP_STRIDE + 1, 0, nc)
    hi = np.clip((j * SEL_BLOCK + SEL_BLOCK - 1) // CMP_STRIDE + 1, 0, nc)
    n = np.arange(ncp)
    return ((n[None, :] >= lo[:, None]) & (n[None, :] < hi[:, None])).astype(np.float32)


def _rank_select(score, k_sel):
    nj, nl = score.shape
    sub = 8
    tiles = [score[v * sub:(v + 1) * sub] for v in range(nj // sub)]
    cnts = [jnp.zeros((sub, nl), F32) for _ in tiles]
    jloc = lax.broadcasted_iota(jnp.int32, (sub, nl), 0)
    for i in range(nj):
        bi = jnp.broadcast_to(score[i:i + 1, :], (sub, nl))
        for v, t in enumerate(tiles):
            if v * sub > i:
                inc = jnp.where(bi >= t, 1.0, 0.0)
            elif v * sub + sub - 1 < i:
                inc = jnp.where(bi > t, 1.0, 0.0)
            else:
                inc = jnp.where(jloc > i - v * sub, jnp.where(bi >= t, 1.0, 0.0), jnp.where(bi > t, 1.0, 0.0))
            cnts[v] = cnts[v] + inc
    cnt = jnp.concatenate(cnts, axis=0)
    return jnp.where(cnt < k_sel, 1.0, 0.0)


def _nsa_prompt_kernel(nc_valid, k_sel, qt_ref, gnt_ref, kc_ref, vct_ref, kas_ref, vts_ref, kaw_ref, vtw_ref,
                       a_ref, o_ref):
    qb = pl.program_id(2)
    nlane = HPG * Q_BLOCK
    qt = qt_ref[0]
    qpos = qb * Q_BLOCK + lax.rem(lax.broadcasted_iota(jnp.int32, (1, nlane), 1), Q_BLOCK)

    rhs_q = jnp.concatenate([qt[h * HEAD_DIM:(h + 1) * HEAD_DIM] for h in range(HPG)], axis=1)
    kc = kc_ref[0, 0][:, 0:HEAD_DIM]
    ncp = kc.shape[0]
    sc = jnp.dot(kc, rhs_q, preferred_element_type=F32)
    nrow = lax.broadcasted_iota(jnp.int32, (ncp, nlane), 0)
    cmask = (nrow * CMP_STRIDE + (CMP_BLOCK - 1) <= qpos) & (nrow < nc_valid)
    sc = jnp.where(cmask, sc, NEG)
    e = jnp.where(cmask, jnp.exp(sc - jnp.max(sc, axis=0, keepdims=True)), 0.0)
    den = jnp.sum(e, axis=0, keepdims=True)
    p = e / jnp.maximum(den, 1e-30)
    o_c = jnp.dot(vct_ref[0, 0], p.astype(BF16), preferred_element_type=F32)
    imp = p[:, 0:Q_BLOCK]
    for h in range(1, HPG):
        imp = imp + p[:, h * Q_BLOCK:(h + 1) * Q_BLOCK]
    a = a_ref[...]
    imp_sel = sum(jnp.dot(a, part, preferred_element_type=F32) for part in _split3(imp))

    nsel = imp_sel.shape[0]
    jrow = lax.broadcasted_iota(jnp.int32, (nsel, Q_BLOCK), 0)
    qp1 = qb * Q_BLOCK + lax.broadcasted_iota(jnp.int32, (nsel, Q_BLOCK), 1)
    cur = qp1 // SEL_BLOCK
    forced = (jrow == 0) | (jrow == cur) | (jrow == cur - 1)
    score = jnp.where(jrow <= cur, jnp.where(forced, FORCE, imp_sel), -1.0)
    sel = _rank_select(score, k_sel)
    mq = ((sel - 1.0) * 1e30).astype(BF16)
    if nsel < HEAD_DIM:
        mq = jnp.concatenate([mq, jnp.zeros((HEAD_DIM - nsel, Q_BLOCK), BF16)], axis=0)
    rhs_aug = jnp.concatenate(
        [jnp.concatenate([qt[h * HEAD_DIM:(h + 1) * HEAD_DIM], mq], axis=0) for h in range(HPG)], axis=1)

    def sweep(c, carry, causal):
        m, l, acc = carry
        start = pl.multiple_of(c * KV_CHUNK, KV_CHUNK)
        s = jnp.dot(kas_ref[0, pl.ds(start, KV_CHUNK), :], rhs_aug, preferred_element_type=F32)
        if causal:
            tok = start + lax.broadcasted_iota(jnp.int32, (KV_CHUNK, nlane), 0)
            s = jnp.where(tok <= qpos, s, NEG)
        m_new = jnp.maximum(m, jnp.max(s, axis=0, keepdims=True))
        alpha = jnp.exp(m - m_new)
        pr = jnp.exp(s - m_new)
        l = l * alpha + jnp.sum(pr, axis=0, keepdims=True)
        vt = jnp.concatenate([vts_ref[2 * c], vts_ref[2 * c + 1]], axis=1)
        acc = acc * alpha + jnp.dot(vt, pr.astype(BF16), preferred_element_type=F32)
        return m_new, l, acc

    init = (jnp.full((1, nlane), NEG, F32), jnp.zeros((1, nlane), F32), jnp.zeros((HEAD_DIM, nlane), F32))
    n_full = qb // 2
    carry = lax.fori_loop(0, n_full, lambda c, cr: sweep(c, cr, False), init)
    m, l, acc = sweep(n_full, carry, True)
    o_s = acc / l

    c0 = jnp.maximum(qb - WINDOW // Q_BLOCK, 0)
    wstart = pl.multiple_of(c0 * Q_BLOCK, Q_BLOCK)
    sw = jnp.dot(kaw_ref[0, pl.ds(wstart, WIN_KEYS), :], rhs_aug, preferred_element_type=F32)
    rel = qpos - (wstart + lax.broadcasted_iota(jnp.int32, (WIN_KEYS, nlane), 0))
    wmask = (rel >= 0) & (rel < WINDOW)
    sw = jnp.where(wmask, sw, NEG)
    ew = jnp.where(wmask, jnp.exp(sw - jnp.max(sw, axis=0, keepdims=True)), 0.0)
    lw = jnp.sum(ew, axis=0, keepdims=True)
    vtw = jnp.concatenate([vtw_ref[c0 + i] for i in range(WIN_KEYS // Q_BLOCK)], axis=1)
    o_w = jnp.dot(vtw, ew.astype(BF16), preferred_element_type=F32) / lw

    gate = _sigmoid(gnt_ref[0])
    outs = []
    for h in range(HPG):
        sl = slice(h * Q_BLOCK, (h + 1) * Q_BLOCK)
        outs.append(gate[3 * h:3 * h + 1] * o_c[:, sl] + gate[3 * h + 1:3 * h + 2] * o_s[:, sl]
                    + gate[3 * h + 2:3 * h + 3] * o_w[:, sl])
    o_t = jnp.concatenate(outs, axis=0)
    o_ref[...] = o_t.T.astype(BF16)


def _nsa_prompt(qt, gnt, kc, vct, kas, vts, kaw, vtw, bsz, seq_len):
    nq = seq_len // Q_BLOCK
    nsel = seq_len // SEL_BLOCK
    nc_valid = seq_len // CMP_STRIDE - 1
    ncp = kc.shape[2]
    k_sel = min(N_SELECT, nsel)
    a = jnp.asarray(_sel_matrix(nsel, nc_valid, ncp), BF16)
    n = bsz * seq_len
    return pl.pallas_call(
        functools.partial(_nsa_prompt_kernel, nc_valid, k_sel),
        grid=(bsz, N_KV, nq),
        in_specs=[
            pl.BlockSpec((1, HPG * HEAD_DIM, LANES), lambda b, g, i: (b * nq + i, g, 0)),
            pl.BlockSpec((1, 16, LANES), lambda b, g, i: (b * nq + i, g, 0)),
            pl.BlockSpec((1, 1, ncp, LANES), lambda b, g, i: (b, g, 0, 0)),
            pl.BlockSpec((1, 1, HEAD_DIM, ncp), lambda b, g, i: (b, g, 0, 0)),
            pl.BlockSpec((1, seq_len, LANES), lambda b, g, i: (g, b, 0)),
            pl.BlockSpec((nq, HEAD_DIM, LANES), lambda b, g, i: (b, g, 0)),
            pl.BlockSpec((1, seq_len, LANES), lambda b, g, i: (g, b, 0)),
            pl.BlockSpec((nq, HEAD_DIM, LANES), lambda b, g, i: (b, g, 0)),
            pl.BlockSpec(a.shape, lambda b, g, i: (0, 0)),
        ],
        out_specs=pl.BlockSpec((Q_BLOCK, HPG * HEAD_DIM), lambda b, g, i: (b * nq + i, g)),
        out_shape=jax.ShapeDtypeStruct((n, D_Q), BF16),
        compiler_params=_cparams(("parallel", "parallel", "arbitrary")), name="nsa_prompt",
    )(qt, gnt, kc, vct, kas, vts, kaw, vtw, a)


S5_L = 16
S5_W = S5_L * S5_CH
S5_P = 2 * S5_STATE


def _s5_prep_kernel(are_ref, aim_ref, ldt_ref, bre_ref, bim_ref, cre_ref, cim_ref,
                    tg_ref, sg_ref, ogt_ref, misc_ref, bs_ref, oct_ref):
    are, aim = are_ref[0], aim_ref[0]
    dt = jnp.exp(ldt_ref[0])
    mag = jnp.exp(are * dt)
    ar, ai = mag * jnp.cos(aim * dt), mag * jnp.sin(aim * dt)
    den = are * are + aim * aim
    fr = ((ar - 1.0) * are + ai * aim) / den
    fi = (ai * are - (ar - 1.0) * aim) / den
    bre, bim = bre_ref[0], bim_ref[0]
    cre, cim = cre_ref[0], cim_ref[0]
    br, bi = fr * bre - fi * bim, fr * bim + fi * bre
    lo16 = lax.broadcasted_iota(jnp.int32, (S5_CH, S5_P), 1) < S5_STATE
    lo1 = lax.broadcasted_iota(jnp.int32, (1, S5_P), 1) < S5_STATE
    pr, pi = [jnp.ones_like(ar)], [jnp.zeros_like(ar)]
    for _ in range(S5_L):
        pr.append(pr[-1] * ar - pi[-1] * ai)
        pi.append(pr[-2] * ai + pi[-1] * ar)
    cpr = [cre * pr[k] - cim * pi[k] for k in range(S5_L + 1)]
    cpi = [cre * pi[k] + cim * pr[k] for k in range(S5_L + 1)]
    rpack = jnp.concatenate([jnp.where(lo16, cpr[k], cpi[k]) for k in range(S5_L)], axis=0)
    bpack = jnp.where(lo16, br, -bi)
    krow = lax.dot_general(bpack, rpack, (((1,), (1,)), ((), ())), preferred_element_type=F32,
                           precision=lax.Precision.HIGHEST)
    lane = lax.broadcasted_iota(jnp.int32, (S5_CH, S5_W), 1)
    for j in range(S5_L):
        shifted = krow if j == 0 else pltpu.roll(krow, j * S5_CH, 1)
        tg_ref[0, j * S5_CH:(j + 1) * S5_CH, :] = jnp.where(lane >= j * S5_CH, shifted, 0.0).astype(BF16)
        k = S5_L - 1 - j
        sblk = jnp.where(lo16, pr[k] * br - pi[k] * bi, pr[k] * bi + pi[k] * br)
        sg_ref[0, j * S5_CH:(j + 1) * S5_CH, :] = sblk.astype(BF16)
        if j == S5_L - 1:
            bs_ref[0] = sblk
        ogt_ref[0, j * S5_CH:(j + 1) * S5_CH, :] = jnp.where(lo16, cpr[j + 1], -cpi[j + 1]).astype(BF16)
    oct_ref[0] = jnp.where(lo16, cre, -cim)
    misc_ref[0] = jnp.concatenate([
        pr[S5_L], jnp.where(lo1, -pi[S5_L], pi[S5_L]), ar, jnp.where(lo1, -ai, ai),
        jnp.zeros((4, S5_P), F32)], axis=0)


def _s5_prep(a_re, a_im, log_dt, b_re, b_im, c_re, c_im):
    g = S5_GROUPS
    dup = lambda a: jnp.concatenate([a, a], axis=-1)
    are, aim = dup(a_re).reshape(g, 1, S5_P), dup(a_im).reshape(g, 1, S5_P)
    ldt = jnp.broadcast_to(log_dt.reshape(g, 1, 1), (g, 1, S5_P))
    bre, bim = dup(jnp.swapaxes(b_re, 1, 2)), dup(jnp.swapaxes(b_im, 1, 2))
    cre, cim = dup(c_re), dup(c_im)
    v1 = pl.BlockSpec((1, 1, S5_P), lambda i: (i, 0, 0))
    v16 = pl.BlockSpec((1, S5_CH, S5_P), lambda i: (i, 0, 0))
    return pl.pallas_call(
        _s5_prep_kernel, grid=(g,),
        in_specs=[v1, v1, v1, v16, v16, v16, v16],
        out_specs=(pl.BlockSpec((1, S5_W, S5_W), lambda i: (i, 0, 0)), pl.BlockSpec((1, S5_W, S5_P), lambda i: (i, 0, 0)),
                   pl.BlockSpec((1, S5_W, S5_P), lambda i: (i, 0, 0)), pl.BlockSpec((1, 8, S5_P), lambda i: (i, 0, 0)),
                   v16, v16),
        out_shape=(jax.ShapeDtypeStruct((g, S5_W, S5_W), BF16), jax.ShapeDtypeStruct((g, S5_W, S5_P), BF16),
                   jax.ShapeDtypeStruct((g, S5_W, S5_P), BF16), jax.ShapeDtypeStruct((g, 8, S5_P), F32),
                   jax.ShapeDtypeStruct((g, S5_CH, S5_P), F32), jax.ShapeDtypeStruct((g, S5_CH, S5_P), F32)),
        compiler_params=_cparams(("parallel",)), name="s5_prep",
    )(are, aim, ldt, bre, bim, cre, cim)


def _s5_sum_kernel(u_ref, sg_ref, s_ref):
    s_ref[0] = jnp.dot(u_ref[0].astype(BF16), sg_ref[0], preferred_element_type=F32)


def _s5_scan_kernel(s_ref, a1_ref, a2_ref, h_ref, last_ref, carry):
    @pl.when(pl.program_id(0) == 0)
    def _():
        carry[...] = jnp.zeros_like(carry)

    a1, a2 = a1_ref[...], a2_ref[...]

    def step(c, h):
        h_ref[c] = h
        return a1 * h + a2 * pltpu.roll(h, S5_STATE, 1) + s_ref[c]

    h = lax.fori_loop(0, s_ref.shape[0], step, carry[...])
    carry[...] = h
    last_ref[...] = h


def _s5_out_kernel(u_ref, h_ref, tg_ref, ogt_ref, d_ref, y_ref):
    u = u_ref[0]
    y = jnp.dot(u.astype(BF16), tg_ref[0], preferred_element_type=F32)
    y = y + lax.dot_general(h_ref[0].astype(BF16), ogt_ref[0], (((1,), (1,)), ((), ())), preferred_element_type=F32)
    y_ref[0] = y + d_ref[0] * u


def _s5_prompt(u, ops, s5_d, bsz, seq_len):
    tg, sg, ogt, misc = ops[0], ops[1], ops[2], ops[3]
    g, nch = S5_GROUPS, seq_len // S5_L
    rows = bsz * nch
    ug = u.reshape(bsz, nch, S5_L, g, S5_CH).transpose(3, 0, 1, 2, 4).reshape(g, rows, S5_W)
    gspec = lambda r, c: pl.BlockSpec((1, r, c), lambda i: (i, 0, 0))
    ssum = pl.pallas_call(
        _s5_sum_kernel, grid=(g,), in_specs=[gspec(rows, S5_W), gspec(S5_W, S5_P)],
        out_specs=gspec(rows, S5_P), out_shape=jax.ShapeDtypeStruct((g, rows, S5_P), F32),
        compiler_params=_cparams(("parallel",)), name="s5_sum",
    )(ug, sg)
    s_cm = ssum.reshape(g, bsz, nch, S5_P).transpose(2, 1, 0, 3).reshape(nch, bsz * g, S5_P)
    a1 = jnp.tile(misc[:, 0, :], (bsz, 1))
    a2 = jnp.tile(misc[:, 1, :], (bsz, 1))
    cb = min(nch, 32)
    hs, last = pl.pallas_call(
        _s5_scan_kernel, grid=(nch // cb,),
        in_specs=[pl.BlockSpec((cb, bsz * g, S5_P), lambda i: (i, 0, 0)),
                  pl.BlockSpec((bsz * g, S5_P), lambda i: (0, 0)), pl.BlockSpec((bsz * g, S5_P), lambda i: (0, 0))],
        out_specs=(pl.BlockSpec((cb, bsz * g, S5_P), lambda i: (i, 0, 0)), pl.BlockSpec((bsz * g, S5_P), lambda i: (0, 0))),
        out_shape=(jax.ShapeDtypeStruct((nch, bsz * g, S5_P), F32), jax.ShapeDtypeStruct((bsz * g, S5_P), F32)),
        scratch_shapes=[pltpu.VMEM((bsz * g, S5_P), F32)],
        compiler_params=_cparams(("arbitrary",)), name="s5_scan",
    )(s_cm, a1, a2)
    h_g = hs.reshape(nch, bsz, g, S5_P).transpose(2, 1, 0, 3).reshape(g, rows, S5_P)
    dvec = jnp.tile(s5_d.reshape(g, 1, S5_CH), (1, 1, S5_L))
    yg = pl.pallas_call(
        _s5_out_kernel, grid=(g,),
        in_specs=[gspec(rows, S5_W), gspec(rows, S5_P), gspec(S5_W, S5_W), gspec(S5_W, S5_P), gspec(1, S5_W)],
        out_specs=gspec(rows, S5_W), out_shape=jax.ShapeDtypeStruct((g, rows, S5_W), F32),
        compiler_params=_cparams(("parallel",)), name="s5_out",
    )(ug, h_g, tg, ogt, dvec)
    y = yg.reshape(g, bsz, nch, S5_L, S5_CH).transpose(1, 2, 3, 0, 4).reshape(bsz * seq_len, S5_WIDTH)
    state = last.reshape(bsz, g, 2, S5_STATE).transpose(0, 1, 3, 2)
    return y, state


def _s5_sample_kernel(u_ref, h0_ref, bs_ref, oct_ref, misc_ref, d_ref, y_ref, h1_ref):
    u, h0 = u_ref[0], h0_ref[0]
    hi = lax.Precision.HIGHEST
    bu = jnp.dot(u, bs_ref[0], preferred_element_type=F32, precision=hi)
    h1 = misc_ref[0, 2:3] * h0 + misc_ref[0, 3:4] * pltpu.roll(h0, S5_STATE, 1) + bu
    h1_ref[0] = h1
    y = lax.dot_general(h1, oct_ref[0], (((1,), (1,)), ((), ())), preferred_element_type=F32, precision=hi)
    y_ref[0] = y + d_ref[0] * u


def _s5_sample(u, state, ops, s5_d):
    misc, bs, oct_ = ops[3], ops[4], ops[5]
    s, g = u.shape[0], S5_GROUPS
    ug = u.reshape(s, g, S5_CH).transpose(1, 0, 2)
    h0 = state.astype(F32).transpose(1, 0, 3, 2).reshape(g, s, S5_P)
    gspec = lambda r, c: pl.BlockSpec((1, r, c), lambda i: (i, 0, 0))
    y, h1 = pl.pallas_call(
        _s5_sample_kernel, grid=(g,),
        in_specs=[gspec(s, S5_CH), gspec(s, S5_P), gspec(S5_CH, S5_P), gspec(S5_CH, S5_P), gspec(8, S5_P), gspec(1, S5_CH)],
        out_specs=(gspec(s, S5_CH), gspec(s, S5_P)),
        out_shape=(jax.ShapeDtypeStruct((g, s, S5_CH), F32), jax.ShapeDtypeStruct((g, s, S5_P), F32)),
        compiler_params=_cparams(("parallel",)), name="s5_sample",
    )(ug, h0, bs, oct_, misc, s5_d.reshape(g, 1, S5_CH))
    return (y.transpose(1, 0, 2).reshape(s, S5_WIDTH),
            h1.reshape(g, s, 2, S5_STATE).transpose(1, 0, 3, 2))


def _merge_mlp_kernel(x_ref, o_ref, ys_ref, gm_ref, wglu_ref, bglu_ref, wbn_ref, wbs_ref, wout_ref, wup_ref, wdn_ref,
                      npost_ref, nmpre_ref, nmpost_ref, out_ref):
    dot = lambda a, w_ref: jnp.dot(a.astype(BF16), w_ref[...], preferred_element_type=F32)
    z = _gelu(ys_ref[...])
    o_s5 = z * _sigmoid(dot(z, wglu_ref) + bglu_ref[...])
    merged = (_sigmoid(gm_ref[:, 0:D_MODEL]) * dot(o_ref[...], wbn_ref)
              + _sigmoid(gm_ref[:, D_MODEL:2 * D_MODEL]) * dot(o_s5, wbs_ref))
    x1 = x_ref[...] + _rms(dot(merged, wout_ref), npost_ref[...])
    hm = _rms(x1, nmpre_ref[...])
    up = jnp.maximum(dot(hm, wup_ref), 0.0)
    f = dot(up * up, wdn_ref)
    out_ref[...] = x1 + _rms(f, nmpost_ref[...])


def _merge_mlp(x2, o_nsa, y_s5, gm, wts, tm=256):
    n = x2.shape[0]
    tm = min(tm, n)
    row = lambda w: pl.BlockSpec((tm, w), lambda i: (i, 0))
    const = lambda a: pl.BlockSpec(a.shape, lambda i: (0, 0), pipeline_mode=pl.Buffered(1))
    return pl.pallas_call(
        _merge_mlp_kernel, grid=(n // tm,),
        in_specs=[row(D_MODEL), row(D_Q), row(S5_WIDTH), row(2 * D_MODEL)] + [const(a) for a in wts],
        out_specs=row(D_MODEL), out_shape=jax.ShapeDtypeStruct((n, D_MODEL), F32),
        compiler_params=_cparams(("parallel",)), name="merge_mlp",
    )(x2, o_nsa, y_s5, gm, *wts)


def _merge_weights(s5_w_glu, s5_b_glu, w_branch_nsa, w_branch_s5, w_out, w_mlp_up, w_mlp_down,
                   norm_mix_post, norm_mlp_pre, norm_mlp_post):
    r = lambda v: v.reshape(1, -1).astype(F32)
    b = lambda w: w.astype(BF16)
    return (b(s5_w_glu), r(s5_b_glu), b(w_branch_nsa), b(w_branch_s5), b(w_out), b(w_mlp_up), b(w_mlp_down),
            r(norm_mix_post), r(norm_mlp_pre), r(norm_mlp_post))


def _inproj_sample_kernel(x_ref, g_ref, wr_ref, wt_ref, kv_ref, u_ref, gm_ref, qg_ref):
    hb = _rms(x_ref[...], g_ref[...]).astype(BF16)
    u_ref[...] = jnp.dot(hb, wr_ref[:, 512:1024], preferred_element_type=F32)
    gm_ref[...] = jnp.dot(hb, wr_ref[:, 1024:3072], preferred_element_type=F32)
    z = lax.dot_general(hb, wt_ref[...], (((1,), (1,)), ((), ())), preferred_element_type=F32)
    qg_ref[...] = z[:, 0:WT_KV]
    kv_ref[...] = z[:, WT_KV:WT_ROWS]


def _inproj_sample(x2, g_pre, w_row, w_t):
    s = x2.shape[0]
    full = lambda a: pl.BlockSpec(a.shape, lambda i: (0, 0))
    o = lambda w: pl.BlockSpec((s, w), lambda i: (0, 0))
    return pl.pallas_call(
        _inproj_sample_kernel, grid=(1,),
        in_specs=[o(D_MODEL), pl.BlockSpec((1, D_MODEL), lambda i: (0, 0)), full(w_row), full(w_t)],
        out_specs=(o(1536), o(512), o(2048), o(WT_KV)),
        out_shape=(jax.ShapeDtypeStruct((s, 1536), F32), jax.ShapeDtypeStruct((s, 512), F32),
                   jax.ShapeDtypeStruct((s, 2048), F32), jax.ShapeDtypeStruct((s, WT_KV), F32)),
        compiler_params=_cparams(("arbitrary",)), name="inproj_sample",
    )(x2, g_pre.reshape(1, D_MODEL), w_row, w_t)


CMP_PAGES_PER_STEP = 16


def _cmp_y_sample_kernel(pps, pt_ref, *refs):
    x_refs, w_ref, y_ref, slab = refs[:pps], refs[pps], refs[pps + 1], refs[pps + 2]
    for r, x in enumerate(x_refs):
        for combo in range(4):
            kv, gp = combo // 2, combo % 2
            pair = x[0, kv, 2 * gp:2 * gp + 2].reshape(2 * HEAD_DIM, PAGE_SIZE)
            slab[combo, r * PAGE_SIZE:(r + 1) * PAGE_SIZE, :] = pair.T
    _cmp_y_from_slabs(slab, w_ref, y_ref)


def _cmp_y_sample(cache_t, page_table, wcmp):
    s, n_pages = page_table.shape
    cpp = PAGE_SIZE // CMP_STRIDE
    pps = CMP_PAGES_PER_STEP
    steps = n_pages // pps

    def page_spec(r):
        return pl.BlockSpec((1, 2, N_KV, HEAD_DIM, PAGE_SIZE), lambda b, j, pt: (pt[b, j * pps + r], 0, 0, 0, 0))

    grid_spec = pltpu.PrefetchScalarGridSpec(
        num_scalar_prefetch=1, grid=(s, steps),
        in_specs=[page_spec(r) for r in range(pps)] + [pl.BlockSpec(wcmp.shape, lambda b, j, pt: (0, 0, 0))],
        out_specs=pl.BlockSpec((1, pps * cpp, 1024), lambda b, j, pt: (b, j, 0)),
        scratch_shapes=[pltpu.VMEM((4, pps * PAGE_SIZE, LANES), F32)],
    )
    return pl.pallas_call(
        functools.partial(_cmp_y_sample_kernel, pps), grid_spec=grid_spec,
        out_shape=jax.ShapeDtypeStruct((s, n_pages * cpp, 1024), F32),
        compiler_params=_cparams(("parallel", "parallel")), name="cmp_y_sample",
    )(page_table, *([cache_t] * pps), wcmp)


def _nsa_sample_cmp_kernel(nc_valid, qbd_ref, kc_ref, vct_ref, a_ref, oc_ref, isel_ref):
    ncp = kc_ref.shape[2]
    sc = jnp.dot(kc_ref[0, 0], qbd_ref[0, 0], preferred_element_type=F32)
    for g in range(1, N_KV):
        sc = sc + jnp.dot(kc_ref[0, g], qbd_ref[0, g], preferred_element_type=F32)
    nrow = lax.broadcasted_iota(jnp.int32, (ncp, LANES), 0)
    cmask = nrow < nc_valid
    sc = jnp.where(cmask, sc, NEG)
    e = jnp.where(cmask, jnp.exp(sc - jnp.max(sc, axis=0, keepdims=True)), 0.0)
    p = e / jnp.sum(e, axis=0, keepdims=True)
    pb = p.astype(BF16)
    for g in range(N_KV):
        oc_ref[0, g] = jnp.dot(vct_ref[0, g], pb, preferred_element_type=F32)
    a = a_ref[...]
    r = sum(jnp.dot(a, part, preferred_element_type=F32) for part in _split3(p))
    tot = r
    for h in range(1, HPG):
        tot = tot + pltpu.roll(r, LANES - h * N_KV, 1)
    isel_ref[0] = tot


def _topk_sample_kernel(nsel, cur, k_sel, isel_ref, tri_ref, idx_ref):
    jp = isel_ref.shape[0]
    jrow = lax.broadcasted_iota(jnp.int32, (jp, LANES), 0)
    forced = (jrow == 0) | (jrow == cur) | (jrow == cur - 1)
    score = jnp.where(jrow <= cur, jnp.where(forced, FORCE, isel_ref[...]), -1.0)
    score = jnp.where(jrow < nsel, score, -2.0)
    sel = _rank_select(score, k_sel)
    rank = jnp.dot(tri_ref[...], sel.astype(BF16), preferred_element_type=F32)
    jf = jrow.astype(F32)
    rows = [jnp.sum(jnp.where((sel > 0.5) & (rank == float(r + 1)), jf, 0.0), axis=0, keepdims=True)
            for r in range(k_sel)]
    idx_ref[...] = jnp.concatenate(rows, axis=0).astype(jnp.int32)


def _nsa_sample_attn_kernel(n_cache, tbl_ref, q_ref, k0_ref, k1_ref, k2_ref, k3_ref, kvs_ref, win_ref, kvw_ref,
                            kvwc_ref, oc_ref, gate_ref, o_ref, wout_ref, m_sc, l_sc, acc_sc):
    b, r = pl.program_id(0), pl.program_id(1)
    blocks = (k0_ref, k1_ref, k2_ref, k3_ref)

    @pl.when(r == 0)
    def _():
        m_sc[...] = jnp.full(m_sc.shape, NEG, F32)
        l_sc[...] = jnp.zeros(l_sc.shape, F32)
        acc_sc[...] = jnp.zeros(acc_sc.shape, F32)

    nt = (((1,), (1,)), ((), ()))
    lane = lax.broadcasted_iota(jnp.int32, (8, PAGE_SIZE), 1)
    for g in range(N_KV):
        q = q_ref[0, g]
        kt = blocks[g][0, 0, 0].astype(BF16)
        vt = blocks[g][0, 1, 0].astype(BF16)
        half = lax.rem(tbl_ref[(b * N_KV + g) * n_cache + r], 2)
        valid = (lane >= half * SEL_BLOCK) & (lane < (half + 1) * SEL_BLOCK)
        s = jnp.where(valid, jnp.dot(q, kt, preferred_element_type=F32), NEG)
        m_old = m_sc[g]
        m_new = jnp.maximum(m_old, jnp.max(s, axis=1, keepdims=True))
        alpha = jnp.exp(m_old - m_new)
        p = jnp.where(valid, jnp.exp(s - m_new), 0.0)
        l_sc[g] = l_sc[g] * alpha + jnp.sum(p, axis=1, keepdims=True)
        acc_sc[g] = acc_sc[g] * alpha + lax.dot_general(p.astype(BF16), vt, nt, preferred_element_type=F32)
        m_sc[g] = m_new

    @pl.when(r == n_cache - 1)
    def _():
        kvw_new = kvw_ref[0]
        kvs_new = kvs_ref[0]
        nwin = win_ref.shape[4]
        wlane = lax.broadcasted_iota(jnp.int32, (HEAD_DIM, nwin), 1)
        for c in range(2 * N_KV):
            shifted = pltpu.roll(win_ref[0, c // N_KV, c % N_KV], nwin - 1, 1)
            wout_ref[0, c // N_KV, c % N_KV] = jnp.where(wlane == nwin - 1, kvwc_ref[0, c], shifted)
        for g in range(N_KV):
            col = g * HEAD_DIM
            q = q_ref[0, g]
            qf = q.astype(F32)
            kn = kvs_new[:, col:col + HEAD_DIM].astype(BF16).astype(F32)
            vn = kvs_new[:, 256 + col:256 + col + HEAD_DIM].astype(BF16).astype(F32)
            s_new = jnp.sum(qf * kn, axis=1, keepdims=True)
            m_old = m_sc[g]
            m_new = jnp.maximum(m_old, s_new)
            alpha = jnp.exp(m_old - m_new)
            p_new = jnp.exp(s_new - m_new)
            l_s = l_sc[g] * alpha + p_new
            o_s = (acc_sc[g] * alpha + p_new.astype(BF16).astype(F32) * vn) / l_s
            kw = win_ref[0, 0, g].astype(BF16)
            vw = win_ref[0, 1, g].astype(BF16)
            sw = jnp.dot(q, kw, preferred_element_type=F32)
            keep = lax.broadcasted_iota(jnp.int32, sw.shape, 1) >= 1
            sw = jnp.where(keep, sw, NEG)
            kwn = kvw_new[:, col:col + HEAD_DIM].astype(BF16).astype(F32)
            vwn = kvw_new[:, 256 + col:256 + col + HEAD_DIM].astype(BF16).astype(F32)
            sw_new = jnp.sum(qf * kwn, axis=1, keepdims=True)
            mw = jnp.maximum(jnp.max(sw, axis=1, keepdims=True), sw_new)
            pw = jnp.where(keep, jnp.exp(sw - mw), 0.0)
            pw_new = jnp.exp(sw_new - mw)
            lw = jnp.sum(pw, axis=1, keepdims=True) + pw_new
            o_w = (lax.dot_general(pw.astype(BF16), vw, nt, preferred_element_type=F32)
                   + pw_new.astype(BF16).astype(F32) * vwn) / lw
            gate = _sigmoid(gate_ref[0, g])
            o_ref[0, g] = gate[0] * oc_ref[0, g] + gate[1] * o_s + gate[2] * o_w


def sample_attention(x_sample, cmp_t, slc_t, win_t, page_table, norm_mix_pre, w_row, w_t,
                     wcmp, cmp_pe_k, cmp_w1_k, cmp_w2_k, cmp_pe_v, cmp_w1_v, cmp_w2_v):
    s = x_sample.shape[0]
    n_pages = page_table.shape[1]
    past = n_pages * PAGE_SIZE
    assert past % SEL_BLOCK == 0 and x_sample.shape[1] == 1
    kv, u, gm, qg = _inproj_sample(x_sample.reshape(s, D_MODEL), norm_mix_pre, w_row, w_t)
    q = qg[:, 0:D_Q].reshape(s, N_KV, HPG, HEAD_DIM) * (HEAD_DIM ** -0.5)
    gn = qg[:, D_Q:].reshape(s, N_KV, 16)[:, :, 0:12].reshape(s, N_KV, HPG, 3)
    kvc_new, kvs_new, kvw_new = kv[:, 0:512], kv[:, 512:1024], kv[:, 1024:1536]

    y = _cmp_y_sample(cmp_t, page_table, wcmp)
    kc, vct = _cmp_combine(y, cmp_pe_k, cmp_w1_k, cmp_w2_k, cmp_pe_v, cmp_w1_v, cmp_w2_v)
    ncp = kc.shape[2]
    nc_valid = (past + 1) // CMP_STRIDE - 1
    nsel = -(-(past + 1) // SEL_BLOCK)
    jp = -(-nsel // 8) * 8
    cur = past // SEL_BLOCK
    k_sel = min(N_SELECT, nsel)
    assert ncp >= nc_valid and (nc_valid - 1) * CMP_STRIDE + CMP_BLOCK - 1 <= past

    qb16 = q.astype(BF16)
    qbd = jnp.zeros((s, N_KV, LANES, LANES), BF16)
    gidx = jnp.arange(N_KV)
    lane_gh = gidx[:, None] + jnp.arange(HPG)[None, :] * N_KV
    qbd = qbd.at[:, gidx[:, None, None], jnp.arange(HEAD_DIM)[None, None, :], lane_gh[:, :, None]].set(qb16)
    pad_rows = lambda a: jnp.pad(a, [(0, 0)] * (a.ndim - 2) + [(0, 8 - HPG), (0, 0)])
    qrow = pad_rows(qb16)

    a = jnp.asarray(np.pad(_sel_matrix(nsel, nc_valid, ncp), ((0, jp - nsel), (0, 0))), BF16)
    oc_t, isel = pl.pallas_call(
        functools.partial(_nsa_sample_cmp_kernel, nc_valid), grid=(s,),
        in_specs=[pl.BlockSpec((1, N_KV, LANES, LANES), lambda b: (b, 0, 0, 0)),
                  pl.BlockSpec((1, N_KV, ncp, LANES), lambda b: (b, 0, 0, 0)),
                  pl.BlockSpec((1, N_KV, HEAD_DIM, ncp), lambda b: (b, 0, 0, 0)),
                  pl.BlockSpec(a.shape, lambda b: (0, 0))],
        out_specs=(pl.BlockSpec((1, N_KV, HEAD_DIM, LANES), lambda b: (b, 0, 0, 0)),
                   pl.BlockSpec((1, jp, LANES), lambda b: (b, 0, 0))),
        out_shape=(jax.ShapeDtypeStruct((s, N_KV, HEAD_DIM, LANES), F32), jax.ShapeDtypeStruct((s, jp, LANES), F32)),
        compiler_params=_cparams(("parallel",)), name="nsa_sample_cmp",
    )(qbd, kc, vct, a)

    assert s * N_KV == LANES
    isel_t = isel[:, :, 0:N_KV].transpose(1, 0, 2).reshape(jp, s * N_KV)
    tri = jnp.asarray(np.tril(np.ones((jp, jp), np.float32)), BF16)
    idx = pl.pallas_call(
        functools.partial(_topk_sample_kernel, nsel, cur, k_sel), grid=(1,),
        in_specs=[pl.BlockSpec((jp, LANES), lambda i: (0, 0)), pl.BlockSpec((jp, jp), lambda i: (0, 0))],
        out_specs=pl.BlockSpec((k_sel, LANES), lambda i: (0, 0)),
        out_shape=jax.ShapeDtypeStruct((k_sel, LANES), jnp.int32),
        compiler_params=_cparams(("arbitrary",)), name="topk_sample",
    )(isel_t, tri)
    n_cache = k_sel - 1
    blk = idx[0:n_cache].T.reshape(s, N_KV, n_cache)
    page = jnp.take_along_axis(page_table, (blk // 2).reshape(s, -1), axis=1).reshape(s, N_KV, n_cache)
    tbl = (page * 2 + blk % 2).astype(jnp.int32).reshape(s * N_KV * n_cache)

    oc_g = oc_t[:, gidx[:, None, None], jnp.arange(HEAD_DIM)[None, None, :], lane_gh[:, :, None]]
    oc_row = pad_rows(oc_g)
    gate_in = pad_rows(jnp.broadcast_to(gn.transpose(0, 1, 3, 2)[..., None], (s, N_KV, 3, HPG, HEAD_DIM)))
    wrows = win_t.shape[4]

    def blk_spec(g):
        return pl.BlockSpec((1, 2, 1, HEAD_DIM, PAGE_SIZE),
                            lambda b, r, t: (t[(b * N_KV + g) * n_cache + r] // 2, 0, g, 0, 0))

    per_b = lambda shape: pl.BlockSpec((1,) + shape, lambda b, r, t: (b,) + (0,) * len(shape))
    win_shape = (2, N_KV, HEAD_DIM, wrows)
    grid_spec = pltpu.PrefetchScalarGridSpec(
        num_scalar_prefetch=1, grid=(s, n_cache),
        in_specs=[per_b((N_KV, 8, HEAD_DIM))] + [blk_spec(g) for g in range(N_KV)]
                 + [per_b((1, 512)), per_b(win_shape), per_b((1, 512)), per_b((2 * N_KV, HEAD_DIM, 1)),
                    per_b((N_KV, 8, HEAD_DIM)), per_b((N_KV, 3, 8, HEAD_DIM))],
        out_specs=(per_b((N_KV, 8, HEAD_DIM)), per_b(win_shape)),
        scratch_shapes=[pltpu.VMEM((N_KV, 8, 1), F32), pltpu.VMEM((N_KV, 8, 1), F32),
                        pltpu.VMEM((N_KV, 8, HEAD_DIM), F32)],
    )
    o_row, win_out = pl.pallas_call(
        functools.partial(_nsa_sample_attn_kernel, n_cache), grid_spec=grid_spec,
        out_shape=(jax.ShapeDtypeStruct((s, N_KV, 8, HEAD_DIM), F32), jax.ShapeDtypeStruct((s,) + win_shape, F32)),
        compiler_params=_cparams(("parallel", "arbitrary")), name="nsa_sample_attn",
    )(tbl, qrow, slc_t, slc_t, slc_t, slc_t, kvs_new.reshape(s, 1, 512), win_t,
      kvw_new.reshape(s, 1, 512), kvw_new.reshape(s, 2 * N_KV, HEAD_DIM, 1), oc_row, gate_in)
    o = o_row[:, :, 0:HPG, :]
    return o.reshape(s, D_Q).astype(BF16), kvc_new, kvs_new, win_out, u, gm


def prompt_attention(x_prompt, norm_mix_pre, w_row, w_t, wcmp, cmp_pe_k, cmp_w1_k, cmp_w2_k, cmp_pe_v, cmp_w1_v, cmp_w2_v):
    bsz, seq_len, _ = x_prompt.shape
    assert seq_len % (2 * Q_BLOCK) == 0 and WIN_KEYS <= seq_len <= SEL_BLOCK * HEAD_DIM
    n = bsz * seq_len
    (u, gm, kas, kaw, slabs, qt, vts, vtw, gnt, kvct, kvst, kvwt) = _inproj_prompt(
        x_prompt.reshape(n, D_MODEL), norm_mix_pre, w_row, w_t, bsz, seq_len)
    y = _cmp_y_prompt(slabs, bsz, seq_len, wcmp)
    kc, vct = _cmp_combine(y, cmp_pe_k, cmp_w1_k, cmp_w2_k, cmp_pe_v, cmp_w1_v, cmp_w2_v)
    o = _nsa_prompt(qt, gnt, kc, vct, kas, vts, kaw, vtw, bsz, seq_len)
    return o, kvct, kvst, kvwt, u, gm


def kernel(x_prompt, x_sample, cache_kv_cmp, cache_kv_slc, state_kv_win, state_s5, page_table, norm_mix_pre, norm_mix_post, norm_mlp_pre, norm_mlp_post, w_in, cmp_pe_k, cmp_w1_k, cmp_w2_k, cmp_pe_v, cmp_w1_v, cmp_w2_v, s5_a_re, s5_a_im, s5_log_dt, s5_b_re, s5_b_im, s5_c_re, s5_c_im, s5_d, s5_w_glu, s5_b_glu, w_branch_nsa, w_branch_s5, w_out, w_mlp_up, w_mlp_down):
    bsz, seq_len, _ = x_prompt.shape
    s = x_sample.shape[0]
    w_row, w_t = _inproj_weights(w_in)
    wcmp = _cmp_weights(cmp_w1_k, cmp_w1_v)
    cmp_w = (cmp_pe_k, cmp_w1_k, cmp_w2_k, cmp_pe_v, cmp_w1_v, cmp_w2_v)
    s5_ops = _s5_prep(s5_a_re, s5_a_im, s5_log_dt, s5_b_re, s5_b_im, s5_c_re, s5_c_im)
    mlp_w = _merge_weights(s5_w_glu, s5_b_glu, w_branch_nsa, w_branch_s5, w_out, w_mlp_up, w_mlp_down,
                           norm_mix_post, norm_mlp_pre, norm_mlp_post)

    o_p, kvc_p, kvs_p, kvw_p, u_p, gm_p = prompt_attention(x_prompt, norm_mix_pre, w_row, w_t, wcmp, *cmp_w)
    ys5_p, s5_p = _s5_prompt(u_p, s5_ops, s5_d, bsz, seq_len)
    y_p = _merge_mlp(x_prompt.reshape(bsz * seq_len, D_MODEL), o_p, ys5_p, gm_p, mlp_w)

    feature_major = lambda c: jnp.transpose(c, (0, 2, 3, 4, 1))
    o_s, kvc_s, kvs_s, win_t, u_s, gm_s = sample_attention(
        x_sample, feature_major(cache_kv_cmp), feature_major(cache_kv_slc), feature_major(state_kv_win),
        page_table, norm_mix_pre, w_row, w_t, wcmp, *cmp_w)
    win_s = jnp.transpose(win_t, (0, 4, 1, 2, 3))
    ys5_s, s5_s = _s5_sample(u_s, state_s5, s5_ops, s5_d)
    y_s = _merge_mlp(x_sample.reshape(s, D_MODEL), o_s, ys5_s, gm_s, mlp_w)

    kv5 = lambda a, b, t: a.reshape(b, t, 2, N_KV, HEAD_DIM)
    token_major = lambda a: jnp.transpose(a.reshape(bsz, 2, N_KV, HEAD_DIM, -1), (0, 4, 1, 2, 3))
    win_rows = min(WINDOW, seq_len)
    win_p = token_major(kvw_p[:, :, seq_len - win_rows:])
    if win_rows < WINDOW:
        win_p = jnp.pad(win_p, ((0, 0), (WINDOW - win_rows, 0), (0, 0), (0, 0), (0, 0)))
    return (y_p.reshape(bsz, seq_len, D_MODEL), y_s.reshape(s, 1, D_MODEL),
            token_major(kvc_p), token_major(kvs_p), win_p, s5_p.astype(x_prompt.dtype),
            kv5(kvc_s, s, 1), kv5(kvs_s, s, 1), kv5(win_s, s, state_kv_win.shape[1]), s5_s.astype(state_s5.dtype))
```

```python
import functools
import math

import numpy as np
import jax
import jax.numpy as jnp
from jax import lax
from jax.experimental import pallas as pl
from jax.experimental.pallas import tpu as pltpu

F32 = jnp.float32
BF16 = jnp.bfloat16

D_MODEL = 1024
HEAD_DIM = 64
N_HEADS = 16
N_KV = 4
HPG = 4
CMP_STRIDE = 16
CMP_BLOCK = 32
SEL_BLOCK = 64
N_SELECT = 16
WINDOW = 512
Q_BLOCK = 128
S5_WIDTH = 512
S5_CH = 16
S5_GROUPS = 32
S5_STATE = 64
D_FF = 4096
D_Q = 1024
D_KV = 256
PAGE_SIZE = 128
EPS = 1e-6
NEG = -1e30
FORCE = 1e4
LANES = 128
VMEM_LIMIT = 56 * 1024 * 1024
SWEEP = 512
WIN_KEYS = WINDOW + Q_BLOCK


def _cparams(sem):
    return pltpu.CompilerParams(dimension_semantics=sem, vmem_limit_bytes=VMEM_LIMIT)


def _gelu(x):
    return 0.5 * x * (1.0 + jnp.tanh(math.sqrt(2.0 / math.pi) * (x + 0.044715 * (x * x * x))))


def _sigmoid(x):
    return 1.0 / (1.0 + jnp.exp(-x))


def _rms(x, g):
    ms = jnp.mean(x * x, axis=-1, keepdims=True)
    return (x * lax.rsqrt(ms + EPS)) * g


def _split3(x):
    hi = x.astype(BF16)
    r1 = x - hi.astype(F32)
    mid = r1.astype(BF16)
    lo = (r1 - mid.astype(F32)).astype(BF16)
    return hi, mid, lo


WT_Q, WT_GN, WT_KV = 0, D_Q, D_Q + 64
WT_ROWS = WT_KV + 6 * D_KV


def _inproj_prompt_kernel(seq_len, x_ref, g_ref, wr_ref, wt_ref,
                          u_ref, gm_ref, kas_ref, kaw_ref, slab_ref,
                          qt_ref, vts_ref, vtw_ref, gnt_ref, kvct_ref, kvst_ref, kvwt_ref):
    tm = x_ref.shape[0]
    hb = _rms(x_ref[...], g_ref[...]).astype(BF16)

    def rowdot(lo, hi):
        return jnp.dot(hb, wr_ref[:, lo:hi], preferred_element_type=F32)

    zc = rowdot(0, 512)
    for combo in range(4):
        slab_ref[combo] = zc[:, combo * LANES:(combo + 1) * LANES]
    u_ref[...] = rowdot(512, 1024)
    gm_ref[...] = rowdot(1024, 3072)
    row = pl.program_id(0) * tm + lax.broadcasted_iota(jnp.int32, (tm, LANES), 0)
    blk = lax.rem(row, seq_len) // SEL_BLOCK
    lane = lax.broadcasted_iota(jnp.int32, (tm, LANES), 1)
    onehot = jnp.where(lane - HEAD_DIM == blk, 1.0, 0.0)
    zs = rowdot(3072, 3584)
    zw = rowdot(3584, 4096)
    for g in range(N_KV):
        kas_ref[g] = (zs[:, g * LANES:(g + 1) * LANES] + onehot).astype(BF16)
        kaw_ref[g] = zw[:, g * LANES:(g + 1) * LANES].astype(BF16)
    zt = lax.dot_general(wt_ref[...], hb, (((1,), (1,)), ((), ())), preferred_element_type=F32)
    kv0 = WT_KV
    kvct_ref[0] = zt[kv0:kv0 + 512]
    kvst_ref[0] = zt[kv0 + 512:kv0 + 1024]
    kvwt_ref[0] = zt[kv0 + 1024:kv0 + 1536]
    for c in range(tm // LANES):
        sl = slice(c * LANES, (c + 1) * LANES)
        qt_ref[c] = (zt[WT_Q:WT_Q + D_Q, sl] * (HEAD_DIM ** -0.5)).astype(BF16)
        gnt_ref[c] = zt[WT_GN:WT_GN + 64, sl]
        vts_ref[c] = zt[kv0 + 768:kv0 + 1024, sl].astype(BF16)
        vtw_ref[c] = zt[kv0 + 1280:kv0 + 1536, sl].astype(BF16)


def _inproj_weights(w_in):
    wq, wkv, wgn, wu, wgm = (w_in[:, :1024], w_in[:, 1024:2560], w_in[:, 2560:2608],
                             w_in[:, 2608:3120], w_in[:, 3120:])
    wkv6 = wkv.reshape(D_MODEL, 3, 2, N_KV, HEAD_DIM)
    zpad = jnp.zeros((D_MODEL, N_KV, HEAD_DIM), F32)
    kaug_s = jnp.concatenate([wkv6[:, 1, 0], zpad], axis=-1).reshape(D_MODEL, N_KV * LANES)
    kaug_w = jnp.concatenate([wkv6[:, 2, 0], zpad], axis=-1).reshape(D_MODEL, N_KV * LANES)
    w_row = jnp.concatenate([wkv[:, 0:512], wu, wgm, kaug_s, kaug_w], axis=1).astype(BF16)
    gn_rows = jnp.pad(wgn.T.reshape(N_KV, HPG * 3, D_MODEL), ((0, 0), (0, 4), (0, 0))).reshape(64, D_MODEL)
    w_t = jnp.concatenate([wq.T, gn_rows, wkv.T], axis=0).astype(BF16)
    return w_row, w_t


def _inproj_prompt(x2, g_pre, w_row, w_t, bsz, seq_len, tm=256):
    n = x2.shape[0]
    nc = n // LANES
    cpt = tm // LANES
    per = seq_len // tm
    row = lambda w: pl.BlockSpec((tm, w), lambda i: (i, 0))
    fmaj = pl.BlockSpec((1, 512, tm), lambda i: (i // per, 0, i % per))
    out_shape = (
        jax.ShapeDtypeStruct((n, 512), F32), jax.ShapeDtypeStruct((n, 2048), F32),
        jax.ShapeDtypeStruct((N_KV, n, LANES), BF16), jax.ShapeDtypeStruct((N_KV, n, LANES), BF16),
        jax.ShapeDtypeStruct((4, n, LANES), F32),
        jax.ShapeDtypeStruct((nc, 1024, LANES), BF16), jax.ShapeDtypeStruct((nc, 256, LANES), BF16),
        jax.ShapeDtypeStruct((nc, 256, LANES), BF16), jax.ShapeDtypeStruct((nc, 64, LANES), F32),
        jax.ShapeDtypeStruct((bsz, 512, seq_len), F32), jax.ShapeDtypeStruct((bsz, 512, seq_len), F32),
        jax.ShapeDtypeStruct((bsz, 512, seq_len), F32),
    )
    out_specs = (
        row(512), row(2048),
        pl.BlockSpec((N_KV, tm, LANES), lambda i: (0, i, 0)), pl.BlockSpec((N_KV, tm, LANES), lambda i: (0, i, 0)),
        pl.BlockSpec((4, tm, LANES), lambda i: (0, i, 0)),
        pl.BlockSpec((cpt, 1024, LANES), lambda i: (i, 0, 0)), pl.BlockSpec((cpt, 256, LANES), lambda i: (i, 0, 0)),
        pl.BlockSpec((cpt, 256, LANES), lambda i: (i, 0, 0)), pl.BlockSpec((cpt, 64, LANES), lambda i: (i, 0, 0)),
        fmaj, fmaj, fmaj,
    )
    return pl.pallas_call(
        functools.partial(_inproj_prompt_kernel, seq_len),
        grid=(n // tm,),
        in_specs=[row(D_MODEL), pl.BlockSpec((1, D_MODEL), lambda i: (0, 0)),
                  pl.BlockSpec(w_row.shape, lambda i: (0, 0)), pl.BlockSpec(w_t.shape, lambda i: (0, 0))],
        out_specs=out_specs, out_shape=out_shape,
        compiler_params=_cparams(("parallel",)), name="inproj_prompt",
    )(x2, g_pre.reshape(1, D_MODEL), w_row, w_t)


def _cmp_weights(w1_k, w1_v):
    eye2 = jnp.eye(2, dtype=F32)
    out = []
    for w1 in (w1_k, w1_v):
        w = w1.reshape(2, CMP_STRIDE, HEAD_DIM, HEAD_DIM)
        big = jnp.einsum('fsdh,ij->sidfjh', w, eye2).reshape(CMP_STRIDE * 2 * HEAD_DIM, 2 * 2 * HEAD_DIM)
        out += [big, big]
    return jnp.stack(out).astype(BF16)


def _cmp_y_from_slabs(slab_ref, w_ref, y_ref):
    nrows = y_ref.shape[1]
    for combo in range(4):
        xg = jnp.concatenate([slab_ref[combo, pl.ds(s, nrows, stride=CMP_STRIDE), :] for s in range(CMP_STRIDE)],
                             axis=1)
        y_ref[0, :, combo * 256:(combo + 1) * 256] = jnp.dot(xg.astype(BF16), w_ref[combo],
                                                             preferred_element_type=F32)


def _cmp_y_prompt(slabs, bsz, seq_len, wcmp):
    nch = seq_len // CMP_STRIDE
    rb = min(nch, 128)
    per = nch // rb
    return pl.pallas_call(
        _cmp_y_from_slabs,
        grid=(bsz, per),
        in_specs=[pl.BlockSpec((4, rb * CMP_STRIDE, LANES), lambda b, j: (0, b * per + j, 0)),
                  pl.BlockSpec(wcmp.shape, lambda b, j: (0, 0, 0))],
        out_specs=pl.BlockSpec((1, rb, 1024), lambda b, j: (b, j, 0)),
        out_shape=jax.ShapeDtypeStruct((bsz, nch, 1024), F32),
        compiler_params=_cparams(("parallel", "parallel")), name="cmp_y_prompt",
    )(slabs, wcmp)


def _cmp_combine_kernel(y_ref, pe_ref, w1f_ref, w2k_ref, w2vt_ref, kc_ref, vct_ref):
    r = y_ref.shape[1]
    pos = jnp.dot(pe_ref[...], w1f_ref[...], preferred_element_type=F32,
                  precision=lax.Precision.HIGHEST)
    for combo in range(4):
        kv, gp = combo // 2, combo % 2
        first = y_ref[0, :, combo * 256: combo * 256 + LANES]
        second = pltpu.roll(y_ref[0, :, combo * 256 + LANES: combo * 256 + 2 * LANES], r - 1, 0)
        p1 = pos[0:1, kv * HEAD_DIM:(kv + 1) * HEAD_DIM]
        pre = first + second + jnp.concatenate([p1, p1], axis=1)
        act = _gelu(pre).astype(BF16)
        if kv == 0:
            kc = jnp.dot(act, w2k_ref[...], preferred_element_type=F32)
            kc_ref[0, 2 * gp] = kc[:, 0:LANES].astype(BF16)
            kc_ref[0, 2 * gp + 1] = kc[:, LANES:2 * LANES].astype(BF16)
        else:
            vct = lax.dot_general(w2vt_ref[...], act, (((1,), (1,)), ((), ())), preferred_element_type=F32)
            vct_ref[0, 2 * gp] = vct[0:HEAD_DIM].astype(BF16)
            vct_ref[0, 2 * gp + 1] = vct[HEAD_DIM:2 * HEAD_DIM].astype(BF16)


def _cmp_combine(y, pe_k, w1_k, w2_k, pe_v, w1_v, w2_v):
    s, r, _ = y.shape
    pe = jnp.concatenate([pe_k.reshape(1, -1), pe_v.reshape(1, -1)], axis=1)
    pe8 = jnp.pad(pe, ((0, 7), (0, 0)))
    z = jnp.zeros((CMP_BLOCK * HEAD_DIM, HEAD_DIM), F32)
    w1f = jnp.concatenate([jnp.concatenate([w1_k.reshape(-1, HEAD_DIM), z], axis=1),
                           jnp.concatenate([z, w1_v.reshape(-1, HEAD_DIM)], axis=1)], axis=0)
    z64 = jnp.zeros((HEAD_DIM, HEAD_DIM), F32)
    w2k = jnp.concatenate([jnp.concatenate([w2_k, z64, z64, z64], axis=1),
                           jnp.concatenate([z64, z64, w2_k, z64], axis=1)], axis=0).astype(BF16)
    w2vt = jnp.concatenate([jnp.concatenate([w2_v.T, z64], axis=1),
                            jnp.concatenate([z64, w2_v.T], axis=1)], axis=0).astype(BF16)
    full = lambda a: pl.BlockSpec(a.shape, lambda i: (0,) * a.ndim)
    return pl.pallas_call(
        _cmp_combine_kernel,
        grid=(s,),
        in_specs=[pl.BlockSpec((1, r, 1024), lambda i: (i, 0, 0)), full(pe8), full(w1f), full(w2k), full(w2vt)],
        out_specs=(pl.BlockSpec((1, N_KV, r, LANES), lambda i: (i, 0, 0, 0)),
                   pl.BlockSpec((1, N_KV, HEAD_DIM, r), lambda i: (i, 0, 0, 0))),
        out_shape=(jax.ShapeDtypeStruct((s, N_KV, r, LANES), BF16),
                   jax.ShapeDtypeStruct((s, N_KV, HEAD_DIM, r), BF16)),
        compiler_params=_cparams(("parallel",)), name="cmp_combine",
    )(y, pe8, w1f, w2k, w2vt)


def _sel_matrix(nsel, nc, ncp):
    j = np.arange(nsel)
    lo = np.clip((j * SEL_BLOCK - CMP_BLOCK) // CMP_STRIDE + 1, 0, nc)
    hi = np.clip((j * SEL_BLOCK + SEL_BLOCK - 1) // CMP_STRIDE + 1, 0, nc)
    n = np.arange(ncp)
    return ((n[None, :] >= lo[:, None]) & (n[None, :] < hi[:, None])).astype(np.float32)


def _rank_select(score, k_sel):
    nj, nl = score.shape
    sub = 8
    tiles = [score[v * sub:(v + 1) * sub] for v in range(nj // sub)]
    cnts = [jnp.zeros((sub, nl), F32) for _ in tiles]
    jloc = lax.broadcasted_iota(jnp.int32, (sub, nl), 0)
    for i in range(nj):
        bi = jnp.broadcast_to(score[i:i + 1, :], (sub, nl))
        for v, t in enumerate(tiles):
            if v * sub > i:
                inc = jnp.where(bi >= t, 1.0, 0.0)
            elif v * sub + sub - 1 < i:
                inc = jnp.where(bi > t, 1.0, 0.0)
            else:
                inc = jnp.where(jloc > i - v * sub, jnp.where(bi >= t, 1.0, 0.0), jnp.where(bi > t, 1.0, 0.0))
            cnts[v] = cnts[v] + inc
    cnt = jnp.concatenate(cnts, axis=0)
    return jnp.where(cnt < k_sel, 1.0, 0.0)


def _nsa_prompt_kernel(nc_valid, k_sel, qt_ref, gnt_ref, kc_ref, vct_ref, kas_ref, vts_ref, kaw_ref, vtw_ref,
                       a_ref, o_ref):
    qb = pl.program_id(2)
    heads = lambda t: jnp.concatenate([t] * HPG, axis=1)
    qt = qt_ref[0]
    ql = lax.broadcasted_iota(jnp.int32, (1, Q_BLOCK), 1)
    tl2 = lax.broadcasted_iota(jnp.int32, (Q_BLOCK, Q_BLOCK), 0)
    ql2 = lax.broadcasted_iota(jnp.int32, (Q_BLOCK, Q_BLOCK), 1)
    b_diag = jnp.where(tl2 <= ql2, 0.0, NEG)
    b_first = jnp.where(tl2 > ql2, 0.0, NEG)
    zero_q = jnp.zeros((HEAD_DIM, Q_BLOCK), BF16)
    rhs_q = jnp.concatenate(
        [jnp.concatenate([qt[h * HEAD_DIM:(h + 1) * HEAD_DIM], zero_q], axis=0) for h in range(HPG)], axis=1)

    ncp = kc_ref.shape[2]
    assert nc_valid >= ncp - 1
    sc = jnp.dot(kc_ref[0, 0], rhs_q, preferred_element_type=F32)
    edge = lax.shift_right_arithmetic(ql - (CMP_BLOCK - 1), 4)
    nrel = lax.broadcasted_iota(jnp.int32, (ncp, Q_BLOCK), 0) - qb * (Q_BLOCK // CMP_STRIDE)
    sc = sc + heads(jnp.where(nrel <= edge, 0.0, NEG))
    e = jnp.exp(sc - jnp.max(sc, axis=0, keepdims=True))
    den = jnp.sum(e, axis=0, keepdims=True)
    any_visible = heads(jnp.where(qb * Q_BLOCK + ql >= CMP_BLOCK - 1, 1.0, 0.0))
    p = e * (any_visible / den)
    o_c = jnp.dot(vct_ref[0, 0], p.astype(BF16), preferred_element_type=F32)
    imp = p[:, 0:Q_BLOCK]
    for h in range(1, HPG):
        imp = imp + p[:, h * Q_BLOCK:(h + 1) * Q_BLOCK]
    a = a_ref[...]
    imp_sel = sum(jnp.dot(a, part, preferred_element_type=F32) for part in _split3(imp))

    nsel = imp_sel.shape[0]
    jrow = lax.broadcasted_iota(jnp.int32, (nsel, Q_BLOCK), 0)
    qp1 = qb * Q_BLOCK + lax.broadcasted_iota(jnp.int32, (nsel, Q_BLOCK), 1)
    cur = qp1 // SEL_BLOCK
    forced = (jrow == 0) | (jrow == cur) | (jrow == cur - 1)
    score = jnp.where(jrow <= cur, jnp.where(forced, FORCE, imp_sel), -1.0)
    sel = jnp.where(jrow <= cur, _rank_select(score, k_sel), 0.0)
    if nsel < HEAD_DIM:
        sel = jnp.concatenate([sel, jnp.zeros((HEAD_DIM - nsel, Q_BLOCK), F32)], axis=0)
    mq = (sel - 1.0) * 1e30
    jrow64 = lax.broadcasted_iota(jnp.int32, (HEAD_DIM, Q_BLOCK), 0)
    mq_past = jnp.where(jrow64 >= qb * (Q_BLOCK // SEL_BLOCK), NEG, mq)

    def with_mask(mrows):
        mb = mrows.astype(BF16)
        return jnp.concatenate(
            [jnp.concatenate([qt[h * HEAD_DIM:(h + 1) * HEAD_DIM], mb], axis=0) for h in range(HPG)], axis=1)

    rhs_diag, rhs_past = with_mask(mq), with_mask(mq_past)

    dstart = pl.multiple_of(qb * Q_BLOCK, Q_BLOCK)
    sd = jnp.dot(kas_ref[0, pl.ds(dstart, Q_BLOCK), :], rhs_diag, preferred_element_type=F32) + heads(b_diag)
    m = jnp.max(sd, axis=0, keepdims=True)
    pd = jnp.exp(sd - m)
    l = jnp.sum(pd, axis=0, keepdims=True)
    acc = jnp.dot(vts_ref[qb], pd.astype(BF16), preferred_element_type=F32)

    cps = SWEEP // Q_BLOCK
    n_span = (qb + cps - 1) // cps

    def span_scores(i):
        start = pl.multiple_of(i * SWEEP, SWEEP)
        return jnp.dot(kas_ref[0, pl.ds(start, SWEEP), :], rhs_past, preferred_element_type=F32)

    def span_consume(i, s, m, l, acc):
        m_new = jnp.maximum(m, jnp.max(s, axis=0, keepdims=True))
        alpha = jnp.exp(m - m_new)
        pr = jnp.exp(s - m_new)
        l = l * alpha + jnp.sum(pr, axis=0, keepdims=True)
        vt = jnp.concatenate([vts_ref[cps * i + k] for k in range(cps)], axis=1)
        acc = acc * alpha + jnp.dot(vt, pr.astype(BF16), preferred_element_type=F32)
        return m_new, l, acc

    def body(i, carry):
        m, l, acc, s_prev = carry
        s_next = span_scores(i)
        m, l, acc = span_consume(i - 1, s_prev, m, l, acc)
        return m, l, acc, s_next

    m, l, acc, s_last = lax.fori_loop(1, n_span, body, (m, l, acc, span_scores(0)))
    m, l, acc = span_consume(jnp.maximum(n_span - 1, 0), s_last, m, l, acc)
    o_s = acc / l

    c0 = jnp.maximum(qb - WINDOW // Q_BLOCK, 0)
    wstart = pl.multiple_of(c0 * Q_BLOCK, Q_BLOCK)
    sw = jnp.dot(kaw_ref[0, pl.ds(wstart, WIN_KEYS), :], rhs_q, preferred_element_type=F32)
    wbias = []
    for i in range(WIN_KEYS // Q_BLOCK):
        d = qb - c0 - i
        wbias.append(jnp.where(d == WINDOW // Q_BLOCK, b_first,
                               jnp.where(d == 0, b_diag, jnp.where(d < 0, NEG, 0.0))))
    sw = sw + heads(jnp.concatenate(wbias, axis=0))
    ew = jnp.exp(sw - jnp.max(sw, axis=0, keepdims=True))
    lw = jnp.sum(ew, axis=0, keepdims=True)
    vtw = jnp.concatenate([vtw_ref[c0 + i] for i in range(WIN_KEYS // Q_BLOCK)], axis=1)
    o_w = jnp.dot(vtw, ew.astype(BF16), preferred_element_type=F32) / lw

    gate = _sigmoid(gnt_ref[0])
    outs = []
    for h in range(HPG):
        sl = slice(h * Q_BLOCK, (h + 1) * Q_BLOCK)
        outs.append(gate[3 * h:3 * h + 1] * o_c[:, sl] + gate[3 * h + 1:3 * h + 2] * o_s[:, sl]
                    + gate[3 * h + 2:3 * h + 3] * o_w[:, sl])
    o_t = jnp.concatenate(outs, axis=0)
    o_ref[...] = o_t.T.astype(BF16)


def _nsa_prompt(qt, gnt, kc, vct, kas, vts, kaw, vtw, bsz, seq_len):
    nq = seq_len // Q_BLOCK
    nsel = seq_len // SEL_BLOCK
    nc_valid = seq_len // CMP_STRIDE - 1
    ncp = kc.shape[2]
    k_sel = min(N_SELECT, nsel)
    a = jnp.asarray(_sel_matrix(nsel, nc_valid, ncp), BF16)
    n = bsz * seq_len
    return pl.pallas_call(
        functools.partial(_nsa_prompt_kernel, nc_valid, k_sel),
        grid=(bsz, N_KV, nq),
        in_specs=[
            pl.BlockSpec((1, HPG * HEAD_DIM, LANES), lambda b, g, i: (b * nq + i, g, 0)),
            pl.BlockSpec((1, 16, LANES), lambda b, g, i: (b * nq + i, g, 0)),
            pl.BlockSpec((1, 1, ncp, LANES), lambda b, g, i: (b, g, 0, 0)),
            pl.BlockSpec((1, 1, HEAD_DIM, ncp), lambda b, g, i: (b, g, 0, 0)),
            pl.BlockSpec((1, seq_len, LANES), lambda b, g, i: (g, b, 0)),
            pl.BlockSpec((nq, HEAD_DIM, LANES), lambda b, g, i: (b, g, 0)),
            pl.BlockSpec((1, seq_len, LANES), lambda b, g, i: (g, b, 0)),
            pl.BlockSpec((nq, HEAD_DIM, LANES), lambda b, g, i: (b, g, 0)),
            pl.BlockSpec(a.shape, lambda b, g, i: (0, 0)),
        ],
        out_specs=pl.BlockSpec((Q_BLOCK, HPG * HEAD_DIM), lambda b, g, i: (b * nq + i, g)),
        out_shape=jax.ShapeDtypeStruct((n, D_Q), BF16),
        compiler_params=_cparams(("parallel", "parallel", "arbitrary")), name="nsa_prompt",
    )(qt, gnt, kc, vct, kas, vts, kaw, vtw, a)


S5_L = 16
S5_W = S5_L * S5_CH
S5_P = 2 * S5_STATE


def _s5_prep_kernel(are_ref, aim_ref, ldt_ref, bre_ref, bim_ref, cre_ref, cim_ref,
                    tg_ref, sg_ref, ogt_ref, misc_ref, bs_ref, oct_ref):
    are, aim = are_ref[0], aim_ref[0]
    dt = jnp.exp(ldt_ref[0])
    mag = jnp.exp(are * dt)
    ar, ai = mag * jnp.cos(aim * dt), mag * jnp.sin(aim * dt)
    den = are * are + aim * aim
    fr = ((ar - 1.0) * are + ai * aim) / den
    fi = (ai * are - (ar - 1.0) * aim) / den
    bre, bim = bre_ref[0], bim_ref[0]
    cre, cim = cre_ref[0], cim_ref[0]
    br, bi = fr * bre - fi * bim, fr * bim + fi * bre
    lo16 = lax.broadcasted_iota(jnp.int32, (S5_CH, S5_P), 1) < S5_STATE
    lo1 = lax.broadcasted_iota(jnp.int32, (1, S5_P), 1) < S5_STATE
    pr, pi = [jnp.ones_like(ar)], [jnp.zeros_like(ar)]
    for _ in range(S5_L):
        pr.append(pr[-1] * ar - pi[-1] * ai)
        pi.append(pr[-2] * ai + pi[-1] * ar)
    cpr = [cre * pr[k] - cim * pi[k] for k in range(S5_L + 1)]
    cpi = [cre * pi[k] + cim * pr[k] for k in range(S5_L + 1)]
    rpack = jnp.concatenate([jnp.where(lo16, cpr[k], cpi[k]) for k in range(S5_L)], axis=0)
    bpack = jnp.where(lo16, br, -bi)
    krow = lax.dot_general(bpack, rpack, (((1,), (1,)), ((), ())), preferred_element_type=F32,
                           precision=lax.Precision.HIGHEST)
    lane = lax.broadcasted_iota(jnp.int32, (S5_CH, S5_W), 1)
    for j in range(S5_L):
        shifted = krow if j == 0 else pltpu.roll(krow, j * S5_CH, 1)
        tg_ref[0, j * S5_CH:(j + 1) * S5_CH, :] = jnp.where(lane >= j * S5_CH, shifted, 0.0).astype(BF16)
        k = S5_L - 1 - j
        sblk = jnp.where(lo16, pr[k] * br - pi[k] * bi, pr[k] * bi + pi[k] * br)
        sg_ref[0, j * S5_CH:(j + 1) * S5_CH, :] = sblk.astype(BF16)
        if j == S5_L - 1:
            bs_ref[0] = sblk
        ogt_ref[0, j * S5_CH:(j + 1) * S5_CH, :] = jnp.where(lo16, cpr[j + 1], -cpi[j + 1]).astype(BF16)
    oct_ref[0] = jnp.where(lo16, cre, -cim)
    misc_ref[0] = jnp.concatenate([
        pr[S5_L], jnp.where(lo1, -pi[S5_L], pi[S5_L]), ar, jnp.where(lo1, -ai, ai),
        jnp.zeros((4, S5_P), F32)], axis=0)


def _s5_prep(a_re, a_im, log_dt, b_re, b_im, c_re, c_im):
    g = S5_GROUPS
    dup = lambda a: jnp.concatenate([a, a], axis=-1)
    are, aim = dup(a_re).reshape(g, 1, S5_P), dup(a_im).reshape(g, 1, S5_P)
    ldt = jnp.broadcast_to(log_dt.reshape(g, 1, 1), (g, 1, S5_P))
    bre, bim = dup(jnp.swapaxes(b_re, 1, 2)), dup(jnp.swapaxes(b_im, 1, 2))
    cre, cim = dup(c_re), dup(c_im)
    v1 = pl.BlockSpec((1, 1, S5_P), lambda i: (i, 0, 0))
    v16 = pl.BlockSpec((1, S5_CH, S5_P), lambda i: (i, 0, 0))
    return pl.pallas_call(
        _s5_prep_kernel, grid=(g,),
        in_specs=[v1, v1, v1, v16, v16, v16, v16],
        out_specs=(pl.BlockSpec((1, S5_W, S5_W), lambda i: (i, 0, 0)), pl.BlockSpec((1, S5_W, S5_P), lambda i: (i, 0, 0)),
                   pl.BlockSpec((1, S5_W, S5_P), lambda i: (i, 0, 0)), pl.BlockSpec((1, 8, S5_P), lambda i: (i, 0, 0)),
                   v16, v16),
        out_shape=(jax.ShapeDtypeStruct((g, S5_W, S5_W), BF16), jax.ShapeDtypeStruct((g, S5_W, S5_P), BF16),
                   jax.ShapeDtypeStruct((g, S5_W, S5_P), BF16), jax.ShapeDtypeStruct((g, 8, S5_P), F32),
                   jax.ShapeDtypeStruct((g, S5_CH, S5_P), F32), jax.ShapeDtypeStruct((g, S5_CH, S5_P), F32)),
        compiler_params=_cparams(("parallel",)), name="s5_prep",
    )(are, aim, ldt, bre, bim, cre, cim)


def _s5_sum_kernel(u_ref, sg_ref, s_ref):
    s_ref[0] = jnp.dot(u_ref[0].astype(BF16), sg_ref[0], preferred_element_type=F32)


def _s5_scan_kernel(s_ref, a1_ref, a2_ref, h_ref, last_ref, carry):
    @pl.when(pl.program_id(0) == 0)
    def _():
        carry[...] = jnp.zeros_like(carry)

    a1, a2 = a1_ref[...], a2_ref[...]

    def step(c, h):
        h_ref[c] = h
        return a1 * h + a2 * pltpu.roll(h, S5_STATE, 1) + s_ref[c]

    h = lax.fori_loop(0, s_ref.shape[0], step, carry[...])
    carry[...] = h
    last_ref[...] = h


def _s5_out_kernel(u_ref, h_ref, tg_ref, ogt_ref, d_ref, y_ref):
    u = u_ref[0]
    y = jnp.dot(u.astype(BF16), tg_ref[0], preferred_element_type=F32)
    y = y + lax.dot_general(h_ref[0].astype(BF16), ogt_ref[0], (((1,), (1,)), ((), ())), preferred_element_type=F32)
    y_ref[0] = y + d_ref[0] * u


def _s5_prompt(u, ops, s5_d, bsz, seq_len):
    tg, sg, ogt, misc = ops[0], ops[1], ops[2], ops[3]
    g, nch = S5_GROUPS, seq_len // S5_L
    rows = bsz * nch
    ug = u.reshape(bsz, nch, S5_L, g, S5_CH).transpose(3, 0, 1, 2, 4).reshape(g, rows, S5_W)
    gspec = lambda r, c: pl.BlockSpec((1, r, c), lambda i: (i, 0, 0))
    ssum = pl.pallas_call(
        _s5_sum_kernel, grid=(g,), in_specs=[gspec(rows, S5_W), gspec(S5_W, S5_P)],
        out_specs=gspec(rows, S5_P), out_shape=jax.ShapeDtypeStruct((g, rows, S5_P), F32),
        compiler_params=_cparams(("parallel",)), name="s5_sum",
    )(ug, sg)
    s_cm = ssum.reshape(g, bsz, nch, S5_P).transpose(2, 1, 0, 3).reshape(nch, bsz * g, S5_P)
    a1 = jnp.tile(misc[:, 0, :], (bsz, 1))
    a2 = jnp.tile(misc[:, 1, :], (bsz, 1))
    cb = min(nch, 32)
    hs, last = pl.pallas_call(
        _s5_scan_kernel, grid=(nch // cb,),
        in_specs=[pl.BlockSpec((cb, bsz * g, S5_P), lambda i: (i, 0, 0)),
                  pl.BlockSpec((bsz * g, S5_P), lambda i: (0, 0)), pl.BlockSpec((bsz * g, S5_P), lambda i: (0, 0))],
        out_specs=(pl.BlockSpec((cb, bsz * g, S5_P), lambda i: (i, 0, 0)), pl.BlockSpec((bsz * g, S5_P), lambda i: (0, 0))),
        out_shape=(jax.ShapeDtypeStruct((nch, bsz * g, S5_P), F32), jax.ShapeDtypeStruct((bsz * g, S5_P), F32)),
        scratch_shapes=[pltpu.VMEM((bsz * g, S5_P), F32)],
        compiler_params=_cparams(("arbitrary",)), name="s5_scan",
    )(s_cm, a1, a2)
    h_g = hs.reshape(nch, bsz, g, S5_P).transpose(2, 1, 0, 3).reshape(g, rows, S5_P)
    dvec = jnp.tile(s5_d.reshape(g, 1, S5_CH), (1, 1, S5_L))
    yg = pl.pallas_call(
        _s5_out_kernel, grid=(g,),
        in_specs=[gspec(rows, S5_W), gspec(rows, S5_P), gspec(S5_W, S5_W), gspec(S5_W, S5_P), gspec(1, S5_W)],
        out_specs=gspec(rows, S5_W), out_shape=jax.ShapeDtypeStruct((g, rows, S5_W), F32),
        compiler_params=_cparams(("parallel",)), name="s5_out",
    )(ug, h_g, tg, ogt, dvec)
    y = yg.reshape(g, bsz, nch, S5_L, S5_CH).transpose(1, 2, 3, 0, 4).reshape(bsz * seq_len, S5_WIDTH)
    state = last.reshape(bsz, g, 2, S5_STATE).transpose(0, 1, 3, 2)
    return y, state


def _s5_sample_kernel(u_ref, h0_ref, bs_ref, oct_ref, misc_ref, d_ref, y_ref, h1_ref):
    u, h0 = u_ref[0], h0_ref[0]
    hi = lax.Precision.HIGHEST
    bu = jnp.dot(u, bs_ref[0], preferred_element_type=F32, precision=hi)
    h1 = misc_ref[0, 2:3] * h0 + misc_ref[0, 3:4] * pltpu.roll(h0, S5_STATE, 1) + bu
    h1_ref[0] = h1
    y = lax.dot_general(h1, oct_ref[0], (((1,), (1,)), ((), ())), preferred_element_type=F32, precision=hi)
    y_ref[0] = y + d_ref[0] * u


def _s5_sample(u, state, ops, s5_d):
    misc, bs, oct_ = ops[3], ops[4], ops[5]
    s, g = u.shape[0], S5_GROUPS
    ug = u.reshape(s, g, S5_CH).transpose(1, 0, 2)
    h0 = state.astype(F32).transpose(1, 0, 3, 2).reshape(g, s, S5_P)
    gspec = lambda r, c: pl.BlockSpec((1, r, c), lambda i: (i, 0, 0))
    y, h1 = pl.pallas_call(
        _s5_sample_kernel, grid=(g,),
        in_specs=[gspec(s, S5_CH), gspec(s, S5_P), gspec(S5_CH, S5_P), gspec(S5_CH, S5_P), gspec(8, S5_P), gspec(1, S5_CH)],
        out_specs=(gspec(s, S5_CH), gspec(s, S5_P)),
        out_shape=(jax.ShapeDtypeStruct((g, s, S5_CH), F32), jax.ShapeDtypeStruct((g, s, S5_P), F32)),
        compiler_params=_cparams(("parallel",)), name="s5_sample",
    )(ug, h0, bs, oct_, misc, s5_d.reshape(g, 1, S5_CH))
    return (y.transpose(1, 0, 2).reshape(s, S5_WIDTH),
            h1.reshape(g, s, 2, S5_STATE).transpose(1, 0, 3, 2))


def _merge_mlp_kernel(x_ref, o_ref, ys_ref, gm_ref, wglu_ref, bglu_ref, wbn_ref, wbs_ref, wout_ref, wup_ref, wdn_ref,
                      npost_ref, nmpre_ref, nmpost_ref, out_ref):
    dot = lambda a, w_ref: jnp.dot(a.astype(BF16), w_ref[...], preferred_element_type=F32)
    z = _gelu(ys_ref[...])
    o_s5 = z * _sigmoid(dot(z, wglu_ref) + bglu_ref[...])
    merged = (_sigmoid(gm_ref[:, 0:D_MODEL]) * dot(o_ref[...], wbn_ref)
              + _sigmoid(gm_ref[:, D_MODEL:2 * D_MODEL]) * dot(o_s5, wbs_ref))
    x1 = x_ref[...] + _rms(dot(merged, wout_ref), npost_ref[...])
    hm = _rms(x1, nmpre_ref[...])
    up = jnp.maximum(dot(hm, wup_ref), 0.0)
    f = dot(up * up, wdn_ref)
    out_ref[...] = x1 + _rms(f, nmpost_ref[...])


def _merge_mlp(x2, o_nsa, y_s5, gm, wts, tm=256):
    n = x2.shape[0]
    tm = min(tm, n)
    row = lambda w: pl.BlockSpec((tm, w), lambda i: (i, 0))
    const = lambda a: pl.BlockSpec(a.shape, lambda i: (0, 0), pipeline_mode=pl.Buffered(1))
    return pl.pallas_call(
        _merge_mlp_kernel, grid=(n // tm,),
        in_specs=[row(D_MODEL), row(D_Q), row(S5_WIDTH), row(2 * D_MODEL)] + [const(a) for a in wts],
        out_specs=row(D_MODEL), out_shape=jax.ShapeDtypeStruct((n, D_MODEL), F32),
        compiler_params=_cparams(("parallel",)), name="merge_mlp",
    )(x2, o_nsa, y_s5, gm, *wts)


def _merge_weights(s5_w_glu, s5_b_glu, w_branch_nsa, w_branch_s5, w_out, w_mlp_up, w_mlp_down,
                   norm_mix_post, norm_mlp_pre, norm_mlp_post):
    r = lambda v: v.reshape(1, -1).astype(F32)
    b = lambda w: w.astype(BF16)
    return (b(s5_w_glu), r(s5_b_glu), b(w_branch_nsa), b(w_branch_s5), b(w_out), b(w_mlp_up), b(w_mlp_down),
            r(norm_mix_post), r(norm_mlp_pre), r(norm_mlp_post))


def _inproj_sample_kernel(x_ref, g_ref, wr_ref, wt_ref, kv_ref, u_ref, gm_ref, qg_ref):
    hb = _rms(x_ref[...], g_ref[...]).astype(BF16)
    u_ref[...] = jnp.dot(hb, wr_ref[:, 512:1024], preferred_element_type=F32)
    gm_ref[...] = jnp.dot(hb, wr_ref[:, 1024:3072], preferred_element_type=F32)
    z = lax.dot_general(hb, wt_ref[...], (((1,), (1,)), ((), ())), preferred_element_type=F32)
    qg_ref[...] = z[:, 0:WT_KV]
    kv_ref[...] = z[:, WT_KV:WT_ROWS]


def _inproj_sample(x2, g_pre, w_row, w_t):
    s = x2.shape[0]
    full = lambda a: pl.BlockSpec(a.shape, lambda i: (0, 0))
    o = lambda w: pl.BlockSpec((s, w), lambda i: (0, 0))
    return pl.pallas_call(
        _inproj_sample_kernel, grid=(1,),
        in_specs=[o(D_MODEL), pl.BlockSpec((1, D_MODEL), lambda i: (0, 0)), full(w_row), full(w_t)],
        out_specs=(o(1536), o(512), o(2048), o(WT_KV)),
        out_shape=(jax.ShapeDtypeStruct((s, 1536), F32), jax.ShapeDtypeStruct((s, 512), F32),
                   jax.ShapeDtypeStruct((s, 2048), F32), jax.ShapeDtypeStruct((s, WT_KV), F32)),
        compiler_params=_cparams(("arbitrary",)), name="inproj_sample",
    )(x2, g_pre.reshape(1, D_MODEL), w_row, w_t)


CMP_PAGES_PER_STEP = 16


def _cmp_y_sample_kernel(pps, pt_ref, *refs):
    x_refs, w_ref, y_ref, slab = refs[:pps], refs[pps], refs[pps + 1], refs[pps + 2]
    for r, x in enumerate(x_refs):
        for combo in range(4):
            kv, gp = combo // 2, combo % 2
            pair = x[0, kv, 2 * gp:2 * gp + 2].reshape(2 * HEAD_DIM, PAGE_SIZE)
            slab[combo, r * PAGE_SIZE:(r + 1) * PAGE_SIZE, :] = pair.T
    _cmp_y_from_slabs(slab, w_ref, y_ref)


def _cmp_y_sample(cache_t, page_table, wcmp):
    s, n_pages = page_table.shape
    cpp = PAGE_SIZE // CMP_STRIDE
    pps = CMP_PAGES_PER_STEP
    steps = n_pages // pps

    def page_spec(r):
        return pl.BlockSpec((1, 2, N_KV, HEAD_DIM, PAGE_SIZE), lambda b, j, pt: (pt[b, j * pps + r], 0, 0, 0, 0))

    grid_spec = pltpu.PrefetchScalarGridSpec(
        num_scalar_prefetch=1, grid=(s, steps),
        in_specs=[page_spec(r) for r in range(pps)] + [pl.BlockSpec(wcmp.shape, lambda b, j, pt: (0, 0, 0))],
        out_specs=pl.BlockSpec((1, pps * cpp, 1024), lambda b, j, pt: (b, j, 0)),
        scratch_shapes=[pltpu.VMEM((4, pps * PAGE_SIZE, LANES), F32)],
    )
    return pl.pallas_call(
        functools.partial(_cmp_y_sample_kernel, pps), grid_spec=grid_spec,
        out_shape=jax.ShapeDtypeStruct((s, n_pages * cpp, 1024), F32),
        compiler_params=_cparams(("parallel", "parallel")), name="cmp_y_sample",
    )(page_table, *([cache_t] * pps), wcmp)


def _nsa_sample_cmp_kernel(nc_valid, qbd_ref, kc_ref, vct_ref, a_ref, oc_ref, isel_ref):
    ncp = kc_ref.shape[2]
    sc = jnp.dot(kc_ref[0, 0], qbd_ref[0, 0], preferred_element_type=F32)
    for g in range(1, N_KV):
        sc = sc + jnp.dot(kc_ref[0, g], qbd_ref[0, g], preferred_element_type=F32)
    nrow = lax.broadcasted_iota(jnp.int32, (ncp, LANES), 0)
    cmask = nrow < nc_valid
    sc = jnp.where(cmask, sc, NEG)
    e = jnp.where(cmask, jnp.exp(sc - jnp.max(sc, axis=0, keepdims=True)), 0.0)
    p = e / jnp.sum(e, axis=0, keepdims=True)
    pb = p.astype(BF16)
    for g in range(N_KV):
        oc_ref[0, g] = jnp.dot(vct_ref[0, g], pb, preferred_element_type=F32)
    a = a_ref[...]
    r = sum(jnp.dot(a, part, preferred_element_type=F32) for part in _split3(p))
    tot = r
    for h in range(1, HPG):
        tot = tot + pltpu.roll(r, LANES - h * N_KV, 1)
    isel_ref[0] = tot


def _topk_sample_kernel(nsel, cur, k_sel, isel_ref, tri_ref, idx_ref):
    jp = isel_ref.shape[0]
    jrow = lax.broadcasted_iota(jnp.int32, (jp, LANES), 0)
    forced = (jrow == 0) | (jrow == cur) | (jrow == cur - 1)
    score = jnp.where(jrow <= cur, jnp.where(forced, FORCE, isel_ref[...]), -1.0)
    score = jnp.where(jrow < nsel, score, -2.0)
    sel = _rank_select(score, k_sel)
    rank = jnp.dot(tri_ref[...], sel.astype(BF16), preferred_element_type=F32)
    jf = jrow.astype(F32)
    rows = [jnp.sum(jnp.where((sel > 0.5) & (rank == float(r + 1)), jf, 0.0), axis=0, keepdims=True)
            for r in range(k_sel)]
    idx_ref[...] = jnp.concatenate(rows, axis=0).astype(jnp.int32)


def _nsa_sample_attn_kernel(n_cache, tbl_ref, q_ref, *refs):
    blocks = refs[:N_KV * n_cache]
    kvs_ref, win_ref, kvw_ref, kvwc_ref, oc_ref, gate_ref, o_ref, wout_ref = refs[N_KV * n_cache:]
    b = pl.program_id(0)
    nt = (((1,), (1,)), ((), ()))
    lane = lax.broadcasted_iota(jnp.int32, (8, PAGE_SIZE), 1)
    kvw_new = kvw_ref[0]
    kvs_new = kvs_ref[0]
    nwin = win_ref.shape[4]
    wlane = lax.broadcasted_iota(jnp.int32, (HEAD_DIM, nwin), 1)
    for c in range(2 * N_KV):
        shifted = pltpu.roll(win_ref[0, c // N_KV, c % N_KV], nwin - 1, 1)
        wout_ref[0, c // N_KV, c % N_KV] = jnp.where(wlane == nwin - 1, kvwc_ref[0, c], shifted)
    for g in range(N_KV):
        col = g * HEAD_DIM
        q = q_ref[0, g]
        qf = q.astype(F32)
        pages = blocks[g * n_cache:(g + 1) * n_cache]
        kt = jnp.concatenate([pg[0, 0, 0].astype(BF16) for pg in pages], axis=1)
        vt = jnp.concatenate([pg[0, 1, 0].astype(BF16) for pg in pages], axis=1)
        bias = []
        for r in range(n_cache):
            half = lax.rem(tbl_ref[(b * N_KV + g) * n_cache + r], 2)
            bias.append(jnp.where((lane >= half * SEL_BLOCK) & (lane < (half + 1) * SEL_BLOCK), 0.0, NEG))
        s = jnp.dot(q, kt, preferred_element_type=F32) + jnp.concatenate(bias, axis=1)
        kn = kvs_new[:, col:col + HEAD_DIM].astype(BF16).astype(F32)
        vn = kvs_new[:, 256 + col:256 + col + HEAD_DIM].astype(BF16).astype(F32)
        s_new = jnp.sum(qf * kn, axis=1, keepdims=True)
        m_s = jnp.maximum(jnp.max(s, axis=1, keepdims=True), s_new)
        p = jnp.exp(s - m_s)
        p_new = jnp.exp(s_new - m_s)
        l_s = jnp.sum(p, axis=1, keepdims=True) + p_new
        o_s = (lax.dot_general(p.astype(BF16), vt, nt, preferred_element_type=F32)
               + p_new.astype(BF16).astype(F32) * vn) / l_s
        kw = win_ref[0, 0, g].astype(BF16)
        vw = win_ref[0, 1, g].astype(BF16)
        sw = jnp.dot(q, kw, preferred_element_type=F32)
        keep = lax.broadcasted_iota(jnp.int32, sw.shape, 1) >= 1
        sw = jnp.where(keep, sw, NEG)
        kwn = kvw_new[:, col:col + HEAD_DIM].astype(BF16).astype(F32)
        vwn = kvw_new[:, 256 + col:256 + col + HEAD_DIM].astype(BF16).astype(F32)
        sw_new = jnp.sum(qf * kwn, axis=1, keepdims=True)
        mw = jnp.maximum(jnp.max(sw, axis=1, keepdims=True), sw_new)
        pw = jnp.where(keep, jnp.exp(sw - mw), 0.0)
        pw_new = jnp.exp(sw_new - mw)
        lw = jnp.sum(pw, axis=1, keepdims=True) + pw_new
        o_w = (lax.dot_general(pw.astype(BF16), vw, nt, preferred_element_type=F32)
               + pw_new.astype(BF16).astype(F32) * vwn) / lw
        gate = _sigmoid(gate_ref[0, g])
        o_ref[0, g] = gate[0] * oc_ref[0, g] + gate[1] * o_s + gate[2] * o_w


def sample_attention(x_sample, cmp_t, slc_t, win_t, page_table, norm_mix_pre, w_row, w_t,
                     wcmp, cmp_pe_k, cmp_w1_k, cmp_w2_k, cmp_pe_v, cmp_w1_v, cmp_w2_v):
    s = x_sample.shape[0]
    n_pages = page_table.shape[1]
    past = n_pages * PAGE_SIZE
    assert past % SEL_BLOCK == 0 and x_sample.shape[1] == 1
    kv, u, gm, qg = _inproj_sample(x_sample.reshape(s, D_MODEL), norm_mix_pre, w_row, w_t)
    q = qg[:, 0:D_Q].reshape(s, N_KV, HPG, HEAD_DIM) * (HEAD_DIM ** -0.5)
    gn = qg[:, D_Q:].reshape(s, N_KV, 16)[:, :, 0:12].reshape(s, N_KV, HPG, 3)
    kvc_new, kvs_new, kvw_new = kv[:, 0:512], kv[:, 512:1024], kv[:, 1024:1536]

    y = _cmp_y_sample(cmp_t, page_table, wcmp)
    kc, vct = _cmp_combine(y, cmp_pe_k, cmp_w1_k, cmp_w2_k, cmp_pe_v, cmp_w1_v, cmp_w2_v)
    ncp = kc.shape[2]
    nc_valid = (past + 1) // CMP_STRIDE - 1
    nsel = -(-(past + 1) // SEL_BLOCK)
    jp = -(-nsel // 8) * 8
    cur = past // SEL_BLOCK
    k_sel = min(N_SELECT, nsel)
    assert ncp >= nc_valid and (nc_valid - 1) * CMP_STRIDE + CMP_BLOCK - 1 <= past

    qb16 = q.astype(BF16)
    qbd = jnp.zeros((s, N_KV, LANES, LANES), BF16)
    gidx = jnp.arange(N_KV)
    lane_gh = gidx[:, None] + jnp.arange(HPG)[None, :] * N_KV
    qbd = qbd.at[:, gidx[:, None, None], jnp.arange(HEAD_DIM)[None, None, :], lane_gh[:, :, None]].set(qb16)
    pad_rows = lambda a: jnp.pad(a, [(0, 0)] * (a.ndim - 2) + [(0, 8 - HPG), (0, 0)])
    qrow = pad_rows(qb16)

    a = jnp.asarray(np.pad(_sel_matrix(nsel, nc_valid, ncp), ((0, jp - nsel), (0, 0))), BF16)
    oc_t, isel = pl.pallas_call(
        functools.partial(_nsa_sample_cmp_kernel, nc_valid), grid=(s,),
        in_specs=[pl.BlockSpec((1, N_KV, LANES, LANES), lambda b: (b, 0, 0, 0)),
                  pl.BlockSpec((1, N_KV, ncp, LANES), lambda b: (b, 0, 0, 0)),
                  pl.BlockSpec((1, N_KV, HEAD_DIM, ncp), lambda b: (b, 0, 0, 0)),
                  pl.BlockSpec(a.shape, lambda b: (0, 0))],
        out_specs=(pl.BlockSpec((1, N_KV, HEAD_DIM, LANES), lambda b: (b, 0, 0, 0)),
                   pl.BlockSpec((1, jp, LANES), lambda b: (b, 0, 0))),
        out_shape=(jax.ShapeDtypeStruct((s, N_KV, HEAD_DIM, LANES), F32), jax.ShapeDtypeStruct((s, jp, LANES), F32)),
        compiler_params=_cparams(("parallel",)), name="nsa_sample_cmp",
    )(qbd, kc, vct, a)

    assert s * N_KV == LANES
    isel_t = isel[:, :, 0:N_KV].transpose(1, 0, 2).reshape(jp, s * N_KV)
    tri = jnp.asarray(np.tril(np.ones((jp, jp), np.float32)), BF16)
    idx = pl.pallas_call(
        functools.partial(_topk_sample_kernel, nsel, cur, k_sel), grid=(1,),
        in_specs=[pl.BlockSpec((jp, LANES), lambda i: (0, 0)), pl.BlockSpec((jp, jp), lambda i: (0, 0))],
        out_specs=pl.BlockSpec((k_sel, LANES), lambda i: (0, 0)),
        out_shape=jax.ShapeDtypeStruct((k_sel, LANES), jnp.int32),
        compiler_params=_cparams(("arbitrary",)), name="topk_sample",
    )(isel_t, tri)
    n_cache = k_sel - 1
    blk = idx[0:n_cache].T.reshape(s, N_KV, n_cache)
    page = jnp.take_along_axis(page_table, (blk // 2).reshape(s, -1), axis=1).reshape(s, N_KV, n_cache)
    tbl = (page * 2 + blk % 2).astype(jnp.int32).reshape(s * N_KV * n_cache)

    oc_g = oc_t[:, gidx[:, None, None], jnp.arange(HEAD_DIM)[None, None, :], lane_gh[:, :, None]]
    oc_row = pad_rows(oc_g)
    gate_in = pad_rows(jnp.broadcast_to(gn.transpose(0, 1, 3, 2)[..., None], (s, N_KV, 3, HPG, HEAD_DIM)))
    wrows = win_t.shape[4]

    def blk_spec(g, r):
        return pl.BlockSpec((1, 2, 1, HEAD_DIM, PAGE_SIZE),
                            lambda b, t: (t[(b * N_KV + g) * n_cache + r] // 2, 0, g, 0, 0))

    per_b = lambda shape: pl.BlockSpec((1,) + shape, lambda b, t: (b,) + (0,) * len(shape))
    win_shape = (2, N_KV, HEAD_DIM, wrows)
    n_blk = N_KV * n_cache
    grid_spec = pltpu.PrefetchScalarGridSpec(
        num_scalar_prefetch=1, grid=(s,),
        in_specs=[per_b((N_KV, 8, HEAD_DIM))] + [blk_spec(g, r) for g in range(N_KV) for r in range(n_cache)]
                 + [per_b((1, 512)), per_b(win_shape), per_b((1, 512)), per_b((2 * N_KV, HEAD_DIM, 1)),
                    per_b((N_KV, 8, HEAD_DIM)), per_b((N_KV, 3, 8, HEAD_DIM))],
        out_specs=(per_b((N_KV, 8, HEAD_DIM)), per_b(win_shape)),
    )
    o_row, win_out = pl.pallas_call(
        functools.partial(_nsa_sample_attn_kernel, n_cache), grid_spec=grid_spec,
        out_shape=(jax.ShapeDtypeStruct((s, N_KV, 8, HEAD_DIM), F32), jax.ShapeDtypeStruct((s,) + win_shape, F32)),
        compiler_params=_cparams(("parallel",)), name="nsa_sample_attn",
    )(tbl, qrow, *([slc_t] * n_blk), kvs_new.reshape(s, 1, 512), win_t,
      kvw_new.reshape(s, 1, 512), kvw_new.reshape(s, 2 * N_KV, HEAD_DIM, 1), oc_row, gate_in)
    o = o_row[:, :, 0:HPG, :]
    return o.reshape(s, D_Q).astype(BF16), kvc_new, kvs_new, win_out, u, gm


def prompt_attention(x_prompt, norm_mix_pre, w_row, w_t, wcmp, cmp_pe_k, cmp_w1_k, cmp_w2_k, cmp_pe_v, cmp_w1_v, cmp_w2_v):
    bsz, seq_len, _ = x_prompt.shape
    assert seq_len % SWEEP == 0 and WIN_KEYS <= seq_len <= SEL_BLOCK * HEAD_DIM
    n = bsz * seq_len
    (u, gm, kas, kaw, slabs, qt, vts, vtw, gnt, kvct, kvst, kvwt) = _inproj_prompt(
        x_prompt.reshape(n, D_MODEL), norm_mix_pre, w_row, w_t, bsz, seq_len)
    y = _cmp_y_prompt(slabs, bsz, seq_len, wcmp)
    kc, vct = _cmp_combine(y, cmp_pe_k, cmp_w1_k, cmp_w2_k, cmp_pe_v, cmp_w1_v, cmp_w2_v)
    o = _nsa_prompt(qt, gnt, kc, vct, kas, vts, kaw, vtw, bsz, seq_len)
    return o, kvct, kvst, kvwt, u, gm


def kernel(x_prompt, x_sample, cache_kv_cmp, cache_kv_slc, state_kv_win, state_s5, page_table, norm_mix_pre, norm_mix_post, norm_mlp_pre, norm_mlp_post, w_in, cmp_pe_k, cmp_w1_k, cmp_w2_k, cmp_pe_v, cmp_w1_v, cmp_w2_v, s5_a_re, s5_a_im, s5_log_dt, s5_b_re, s5_b_im, s5_c_re, s5_c_im, s5_d, s5_w_glu, s5_b_glu, w_branch_nsa, w_branch_s5, w_out, w_mlp_up, w_mlp_down):
    bsz, seq_len, _ = x_prompt.shape
    s = x_sample.shape[0]
    w_row, w_t = _inproj_weights(w_in)
    wcmp = _cmp_weights(cmp_w1_k, cmp_w1_v)
    cmp_w = (cmp_pe_k, cmp_w1_k, cmp_w2_k, cmp_pe_v, cmp_w1_v, cmp_w2_v)
    s5_ops = _s5_prep(s5_a_re, s5_a_im, s5_log_dt, s5_b_re, s5_b_im, s5_c_re, s5_c_im)
    mlp_w = _merge_weights(s5_w_glu, s5_b_glu, w_branch_nsa, w_branch_s5, w_out, w_mlp_up, w_mlp_down,
                           norm_mix_post, norm_mlp_pre, norm_mlp_post)

    o_p, kvc_p, kvs_p, kvw_p, u_p, gm_p = prompt_attention(x_prompt, norm_mix_pre, w_row, w_t, wcmp, *cmp_w)
    ys5_p, s5_p = _s5_prompt(u_p, s5_ops, s5_d, bsz, seq_len)
    y_p = _merge_mlp(x_prompt.reshape(bsz * seq_len, D_MODEL), o_p, ys5_p, gm_p, mlp_w)

    feature_major = lambda c: jnp.transpose(c, (0, 2, 3, 4, 1))
    o_s, kvc_s, kvs_s, win_t, u_s, gm_s = sample_attention(
        x_sample, feature_major(cache_kv_cmp), feature_major(cache_kv_slc), feature_major(state_kv_win),
        page_table, norm_mix_pre, w_row, w_t, wcmp, *cmp_w)
    win_s = jnp.transpose(win_t, (0, 4, 1, 2, 3))
    ys5_s, s5_s = _s5_sample(u_s, state_s5, s5_ops, s5_d)
    y_s = _merge_mlp(x_sample.reshape(s, D_MODEL), o_s, ys5_s, gm_s, mlp_w)

    kv5 = lambda a, b, t: a.reshape(b, t, 2, N_KV, HEAD_DIM)
    token_major = lambda a: jnp.transpose(a.reshape(bsz, 2, N_KV, HEAD_DIM, -1), (0, 4, 1, 2, 3))
    win_rows = min(WINDOW, seq_len)
    win_p = token_major(kvw_p[:, :, seq_len - win_rows:])
    if win_rows < WINDOW:
        win_p = jnp.pad(win_p, ((0, 0), (WINDOW - win_rows, 0), (0, 0), (0, 0), (0, 0)))
    return (y_p.reshape(bsz, seq_len, D_MODEL), y_s.reshape(s, 1, D_MODEL),
            token_major(kvc_p), token_major(kvs_p), win_p, s5_p.astype(x_prompt.dtype),
            kv5(kvc_s, s, 1), kv5(kvs_s, s, 1), kv5(win_s, s, state_kv_win.shape[1]), s5_s.astype(state_s5.dtype))
```

```python
import functools
import math

import numpy as np
import jax
import jax.numpy as jnp
from jax import lax
from jax.experimental import pallas as pl
from jax.experimental.pallas import tpu as pltpu

F32 = jnp.float32
BF16 = jnp.bfloat16

D_MODEL = 1024
HEAD_DIM = 64
N_HEADS = 16
N_KV = 4
HPG = 4
CMP_STRIDE = 16
CMP_BLOCK = 32
SEL_BLOCK = 64
N_SELECT = 16
WINDOW = 512
Q_BLOCK = 128
S5_WIDTH = 512
S5_CH = 16
S5_GROUPS = 32
S5_STATE = 64
D_FF = 4096
D_Q = 1024
D_KV = 256
PAGE_SIZE = 128
EPS = 1e-6
NEG = -1e30
FORCE = 1e4
LANES = 128
VMEM_LIMIT = 56 * 1024 * 1024
SWEEP = 512
WIN_KEYS = WINDOW + Q_BLOCK
Q_SCALE_LOG2 = HEAD_DIM ** -0.5 * math.log2(math.e)


def _cparams(sem):
    return pltpu.CompilerParams(dimension_semantics=sem, vmem_limit_bytes=VMEM_LIMIT)


def _gelu(x):
    return 0.5 * x * (1.0 + jnp.tanh(math.sqrt(2.0 / math.pi) * (x + 0.044715 * (x * x * x))))


def _sigmoid(x):
    return 1.0 / (1.0 + jnp.exp(-x))


def _rms(x, g):
    ms = jnp.mean(x * x, axis=-1, keepdims=True)
    return (x * lax.rsqrt(ms + EPS)) * g


def _split3(x):
    hi = x.astype(BF16)
    r1 = x - hi.astype(F32)
    mid = r1.astype(BF16)
    lo = (r1 - mid.astype(F32)).astype(BF16)
    return hi, mid, lo


WT_Q, WT_GN, WT_KV = 0, D_Q, D_Q + 64
WT_ROWS = WT_KV + 6 * D_KV


def _inproj_prompt_kernel(seq_len, x_ref, g_ref, wr_ref, wt_ref,
                          u_ref, gm_ref, kas_ref, kaw_ref, slab_ref,
                          qt_ref, vts_ref, vtw_ref, gnt_ref, kvct_ref, kvst_ref, kvwt_ref):
    tm = x_ref.shape[0]
    hb = _rms(x_ref[...], g_ref[...]).astype(BF16)

    def rowdot(lo, hi):
        return jnp.dot(hb, wr_ref[:, lo:hi], preferred_element_type=F32)

    zc = rowdot(0, 512)
    for combo in range(4):
        slab_ref[combo] = zc[:, combo * LANES:(combo + 1) * LANES]
    u_ref[...] = rowdot(512, 1024)
    gm_ref[...] = rowdot(1024, 3072)
    row = pl.program_id(0) * tm + lax.broadcasted_iota(jnp.int32, (tm, LANES), 0)
    blk = lax.rem(row, seq_len) // SEL_BLOCK
    lane = lax.broadcasted_iota(jnp.int32, (tm, LANES), 1)
    onehot = jnp.where(lane - HEAD_DIM == blk, 1.0, 0.0)
    zs = rowdot(3072, 3584)
    zw = rowdot(3584, 4096)
    for g in range(N_KV):
        kas_ref[g] = (zs[:, g * LANES:(g + 1) * LANES] + onehot).astype(BF16)
        kaw_ref[g] = zw[:, g * LANES:(g + 1) * LANES].astype(BF16)
    zt = lax.dot_general(wt_ref[...], hb, (((1,), (1,)), ((), ())), preferred_element_type=F32)
    kv0 = WT_KV
    kvct_ref[0] = zt[kv0:kv0 + 512]
    kvst_ref[0] = zt[kv0 + 512:kv0 + 1024]
    kvwt_ref[0] = zt[kv0 + 1024:kv0 + 1536]
    for c in range(tm // LANES):
        sl = slice(c * LANES, (c + 1) * LANES)
        qt_ref[c] = (zt[WT_Q:WT_Q + D_Q, sl] * Q_SCALE_LOG2).astype(BF16)
        gnt_ref[c] = zt[WT_GN:WT_GN + 64, sl]
        vts_ref[c] = zt[kv0 + 768:kv0 + 1024, sl].astype(BF16)
        vtw_ref[c] = zt[kv0 + 1280:kv0 + 1536, sl].astype(BF16)


def _inproj_weights(w_in):
    wq, wkv, wgn, wu, wgm = (w_in[:, :1024], w_in[:, 1024:2560], w_in[:, 2560:2608],
                             w_in[:, 2608:3120], w_in[:, 3120:])
    wkv6 = wkv.reshape(D_MODEL, 3, 2, N_KV, HEAD_DIM)
    zpad = jnp.zeros((D_MODEL, N_KV, HEAD_DIM), F32)
    kaug_s = jnp.concatenate([wkv6[:, 1, 0], zpad], axis=-1).reshape(D_MODEL, N_KV * LANES)
    kaug_w = jnp.concatenate([wkv6[:, 2, 0], zpad], axis=-1).reshape(D_MODEL, N_KV * LANES)
    w_row = jnp.concatenate([wkv[:, 0:512], wu, wgm, kaug_s, kaug_w], axis=1).astype(BF16)
    gn_rows = jnp.pad(wgn.T.reshape(N_KV, HPG * 3, D_MODEL), ((0, 0), (0, 4), (0, 0))).reshape(64, D_MODEL)
    w_t = jnp.concatenate([wq.T, gn_rows, wkv.T], axis=0).astype(BF16)
    return w_row, w_t


def _inproj_prompt(x2, g_pre, w_row, w_t, bsz, seq_len, tm=256):
    n = x2.shape[0]
    nc = n // LANES
    cpt = tm // LANES
    per = seq_len // tm
    row = lambda w: pl.BlockSpec((tm, w), lambda i: (i, 0))
    fmaj = pl.BlockSpec((1, 512, tm), lambda i: (i // per, 0, i % per))
    out_shape = (
        jax.ShapeDtypeStruct((n, 512), F32), jax.ShapeDtypeStruct((n, 2048), F32),
        jax.ShapeDtypeStruct((N_KV, n, LANES), BF16), jax.ShapeDtypeStruct((N_KV, n, LANES), BF16),
        jax.ShapeDtypeStruct((4, n, LANES), F32),
        jax.ShapeDtypeStruct((nc, 1024, LANES), BF16), jax.ShapeDtypeStruct((nc, 256, LANES), BF16),
        jax.ShapeDtypeStruct((nc, 256, LANES), BF16), jax.ShapeDtypeStruct((nc, 64, LANES), F32),
        jax.ShapeDtypeStruct((bsz, 512, seq_len), F32), jax.ShapeDtypeStruct((bsz, 512, seq_len), F32),
        jax.ShapeDtypeStruct((bsz, 512, seq_len), F32),
    )
    out_specs = (
        row(512), row(2048),
        pl.BlockSpec((N_KV, tm, LANES), lambda i: (0, i, 0)), pl.BlockSpec((N_KV, tm, LANES), lambda i: (0, i, 0)),
        pl.BlockSpec((4, tm, LANES), lambda i: (0, i, 0)),
        pl.BlockSpec((cpt, 1024, LANES), lambda i: (i, 0, 0)), pl.BlockSpec((cpt, 256, LANES), lambda i: (i, 0, 0)),
        pl.BlockSpec((cpt, 256, LANES), lambda i: (i, 0, 0)), pl.BlockSpec((cpt, 64, LANES), lambda i: (i, 0, 0)),
        fmaj, fmaj, fmaj,
    )
    return pl.pallas_call(
        functools.partial(_inproj_prompt_kernel, seq_len),
        grid=(n // tm,),
        in_specs=[row(D_MODEL), pl.BlockSpec((1, D_MODEL), lambda i: (0, 0)),
                  pl.BlockSpec(w_row.shape, lambda i: (0, 0)), pl.BlockSpec(w_t.shape, lambda i: (0, 0))],
        out_specs=out_specs, out_shape=out_shape,
        compiler_params=_cparams(("parallel",)), name="inproj_prompt",
    )(x2, g_pre.reshape(1, D_MODEL), w_row, w_t)


def _cmp_weights(w1_k, w1_v):
    eye2 = jnp.eye(2, dtype=F32)
    out = []
    for w1 in (w1_k, w1_v):
        w = w1.reshape(2, CMP_STRIDE, HEAD_DIM, HEAD_DIM)
        big = jnp.einsum('fsdh,ij->sidfjh', w, eye2).reshape(CMP_STRIDE * 2 * HEAD_DIM, 2 * 2 * HEAD_DIM)
        out += [big, big]
    return jnp.stack(out).astype(BF16)


def _cmp_y_from_slabs(slab_ref, w_ref, y_ref):
    nrows = y_ref.shape[1]
    for combo in range(4):
        xg = jnp.concatenate([slab_ref[combo, pl.ds(s, nrows, stride=CMP_STRIDE), :] for s in range(CMP_STRIDE)],
                             axis=1)
        y_ref[0, :, combo * 256:(combo + 1) * 256] = jnp.dot(xg.astype(BF16), w_ref[combo],
                                                             preferred_element_type=F32)


def _cmp_y_prompt(slabs, bsz, seq_len, wcmp):
    nch = seq_len // CMP_STRIDE
    rb = min(nch, 128)
    per = nch // rb
    return pl.pallas_call(
        _cmp_y_from_slabs,
        grid=(bsz, per),
        in_specs=[pl.BlockSpec((4, rb * CMP_STRIDE, LANES), lambda b, j: (0, b * per + j, 0)),
                  pl.BlockSpec(wcmp.shape, lambda b, j: (0, 0, 0))],
        out_specs=pl.BlockSpec((1, rb, 1024), lambda b, j: (b, j, 0)),
        out_shape=jax.ShapeDtypeStruct((bsz, nch, 1024), F32),
        compiler_params=_cparams(("parallel", "parallel")), name="cmp_y_prompt",
    )(slabs, wcmp)


def _cmp_combine_kernel(y_ref, pe_ref, w1f_ref, w2k_ref, w2vt_ref, kc_ref, vct_ref):
    r = y_ref.shape[1]
    pos = jnp.dot(pe_ref[...], w1f_ref[...], preferred_element_type=F32,
                  precision=lax.Precision.HIGHEST)
    for combo in range(4):
        kv, gp = combo // 2, combo % 2
        first = y_ref[0, :, combo * 256: combo * 256 + LANES]
        second = pltpu.roll(y_ref[0, :, combo * 256 + LANES: combo * 256 + 2 * LANES], r - 1, 0)
        p1 = pos[0:1, kv * HEAD_DIM:(kv + 1) * HEAD_DIM]
        pre = first + second + jnp.concatenate([p1, p1], axis=1)
        act = _gelu(pre).astype(BF16)
        if kv == 0:
            kc = jnp.dot(act, w2k_ref[...], preferred_element_type=F32)
            kc_ref[0, 2 * gp] = kc[:, 0:LANES].astype(BF16)
            kc_ref[0, 2 * gp + 1] = kc[:, LANES:2 * LANES].astype(BF16)
        else:
            vct = lax.dot_general(w2vt_ref[...], act, (((1,), (1,)), ((), ())), preferred_element_type=F32)
            vct_ref[0, 2 * gp] = vct[0:HEAD_DIM].astype(BF16)
            vct_ref[0, 2 * gp + 1] = vct[HEAD_DIM:2 * HEAD_DIM].astype(BF16)


def _cmp_combine(y, pe_k, w1_k, w2_k, pe_v, w1_v, w2_v):
    s, r, _ = y.shape
    pe = jnp.concatenate([pe_k.reshape(1, -1), pe_v.reshape(1, -1)], axis=1)
    pe8 = jnp.pad(pe, ((0, 7), (0, 0)))
    z = jnp.zeros((CMP_BLOCK * HEAD_DIM, HEAD_DIM), F32)
    w1f = jnp.concatenate([jnp.concatenate([w1_k.reshape(-1, HEAD_DIM), z], axis=1),
                           jnp.concatenate([z, w1_v.reshape(-1, HEAD_DIM)], axis=1)], axis=0)
    z64 = jnp.zeros((HEAD_DIM, HEAD_DIM), F32)
    w2k = jnp.concatenate([jnp.concatenate([w2_k, z64, z64, z64], axis=1),
                           jnp.concatenate([z64, z64, w2_k, z64], axis=1)], axis=0).astype(BF16)
    w2vt = jnp.concatenate([jnp.concatenate([w2_v.T, z64], axis=1),
                            jnp.concatenate([z64, w2_v.T], axis=1)], axis=0).astype(BF16)
    full = lambda a: pl.BlockSpec(a.shape, lambda i: (0,) * a.ndim)
    return pl.pallas_call(
        _cmp_combine_kernel,
        grid=(s,),
        in_specs=[pl.BlockSpec((1, r, 1024), lambda i: (i, 0, 0)), full(pe8), full(w1f), full(w2k), full(w2vt)],
        out_specs=(pl.BlockSpec((1, N_KV, r, LANES), lambda i: (i, 0, 0, 0)),
                   pl.BlockSpec((1, N_KV, HEAD_DIM, r), lambda i: (i, 0, 0, 0))),
        out_shape=(jax.ShapeDtypeStruct((s, N_KV, r, LANES), BF16),
                   jax.ShapeDtypeStruct((s, N_KV, HEAD_DIM, r), BF16)),
        compiler_params=_cparams(("parallel",)), name="cmp_combine",
    )(y, pe8, w1f, w2k, w2vt)


def _sel_matrix(nsel, nc, ncp):
    j = np.arange(nsel)
    lo = np.clip((j * SEL_BLOCK - CMP_BLOCK) // CMP_STRIDE + 1, 0, nc)
    hi = np.clip((j * SEL_BLOCK + SEL_BLOCK - 1) // CMP_STRIDE + 1, 0, nc)
    n = np.arange(ncp)
    return ((n[None, :] >= lo[:, None]) & (n[None, :] < hi[:, None])).astype(np.float32)


def _rank_select(score, k_sel):
    nj, nl = score.shape
    sub = 8
    tiles = [score[v * sub:(v + 1) * sub] for v in range(nj // sub)]
    cnts = [jnp.zeros((sub, nl), F32) for _ in tiles]
    jloc = lax.broadcasted_iota(jnp.int32, (sub, nl), 0)
    for i in range(nj):
        bi = jnp.broadcast_to(score[i:i + 1, :], (sub, nl))
        for v, t in enumerate(tiles):
            if v * sub > i:
                inc = jnp.where(bi >= t, 1.0, 0.0)
            elif v * sub + sub - 1 < i:
                inc = jnp.where(bi > t, 1.0, 0.0)
            else:
                inc = jnp.where(jloc > i - v * sub, jnp.where(bi >= t, 1.0, 0.0), jnp.where(bi > t, 1.0, 0.0))
            cnts[v] = cnts[v] + inc
    cnt = jnp.concatenate(cnts, axis=0)
    return jnp.where(cnt < k_sel, 1.0, 0.0)


def _nsa_prompt_kernel(nc_valid, k_sel, qt_ref, gnt_ref, kc_ref, vct_ref, kas_ref, vts_ref, kaw_ref, vtw_ref,
                       a_ref, o_ref, sa_ref, sb_ref):
    qb = pl.program_id(2)
    heads = lambda t: jnp.concatenate([t] * HPG, axis=1)

    def pv_and_sum(vt, p):
        va = jnp.concatenate([vt, jnp.ones((16, vt.shape[1]), BF16)], axis=0)
        out = jnp.dot(va, p.astype(BF16), preferred_element_type=F32)
        return out[0:HEAD_DIM], out[HEAD_DIM:HEAD_DIM + 1]
    qt = qt_ref[0]
    ql = lax.broadcasted_iota(jnp.int32, (1, Q_BLOCK), 1)
    tl2 = lax.broadcasted_iota(jnp.int32, (Q_BLOCK, Q_BLOCK), 0)
    ql2 = lax.broadcasted_iota(jnp.int32, (Q_BLOCK, Q_BLOCK), 1)
    b_diag = jnp.where(tl2 <= ql2, 0.0, NEG)
    b_first = jnp.where(tl2 > ql2, 0.0, NEG)
    zero_q = jnp.zeros((HEAD_DIM, Q_BLOCK), BF16)
    rhs_q = jnp.concatenate(
        [jnp.concatenate([qt[h * HEAD_DIM:(h + 1) * HEAD_DIM], zero_q], axis=0) for h in range(HPG)], axis=1)

    ncp = kc_ref.shape[2]
    assert nc_valid >= ncp - 1
    sc = jnp.dot(kc_ref[0, 0], rhs_q, preferred_element_type=F32)
    edge = lax.shift_right_arithmetic(ql - (CMP_BLOCK - 1), 4)
    nrel = lax.broadcasted_iota(jnp.int32, (ncp, Q_BLOCK), 0) - qb * (Q_BLOCK // CMP_STRIDE)
    sc = sc + heads(jnp.where(nrel <= edge, 0.0, NEG))
    e = jnp.exp2(sc - jnp.max(sc, axis=0, keepdims=True))
    den = jnp.sum(e, axis=0, keepdims=True)
    any_visible = heads(jnp.where(qb * Q_BLOCK + ql >= CMP_BLOCK - 1, 1.0, 0.0))
    p = e * (any_visible / den)
    o_c = jnp.dot(vct_ref[0, 0], p.astype(BF16), preferred_element_type=F32)
    imp = p[:, 0:Q_BLOCK]
    for h in range(1, HPG):
        imp = imp + p[:, h * Q_BLOCK:(h + 1) * Q_BLOCK]
    a = a_ref[...]
    imp_sel = sum(jnp.dot(a, part, preferred_element_type=F32) for part in _split3(imp))

    nsel = imp_sel.shape[0]
    jrow = lax.broadcasted_iota(jnp.int32, (nsel, Q_BLOCK), 0)
    qp1 = qb * Q_BLOCK + lax.broadcasted_iota(jnp.int32, (nsel, Q_BLOCK), 1)
    cur = qp1 // SEL_BLOCK
    forced = (jrow == 0) | (jrow == cur) | (jrow == cur - 1)
    score = jnp.where(jrow <= cur, jnp.where(forced, FORCE, imp_sel), -1.0)
    sel = jnp.where(jrow <= cur, _rank_select(score, k_sel), 0.0)
    if nsel < HEAD_DIM:
        sel = jnp.concatenate([sel, jnp.zeros((HEAD_DIM - nsel, Q_BLOCK), F32)], axis=0)
    mq = (sel - 1.0) * 1e30
    jrow64 = lax.broadcasted_iota(jnp.int32, (HEAD_DIM, Q_BLOCK), 0)
    mq_past = jnp.where(jrow64 >= qb * (Q_BLOCK // SEL_BLOCK), NEG, mq)

    def with_mask(mrows):
        mb = mrows.astype(BF16)
        return jnp.concatenate(
            [jnp.concatenate([qt[h * HEAD_DIM:(h + 1) * HEAD_DIM], mb], axis=0) for h in range(HPG)], axis=1)

    rhs_diag, rhs_past = with_mask(mq), with_mask(mq_past)

    dstart = pl.multiple_of(qb * Q_BLOCK, Q_BLOCK)
    sd = jnp.dot(kas_ref[0, pl.ds(dstart, Q_BLOCK), :], rhs_diag, preferred_element_type=F32) + heads(b_diag)
    m = jnp.max(sd, axis=0, keepdims=True)
    acc, l = pv_and_sum(vts_ref[qb], jnp.exp2(sd - m))

    cps = SWEEP // Q_BLOCK
    n_span = (qb + cps - 1) // cps

    def span_scores(i, buf):
        start = pl.multiple_of(i * SWEEP, SWEEP)
        s = jnp.dot(kas_ref[0, pl.ds(start, SWEEP), :], rhs_past, preferred_element_type=F32)
        buf[...] = s
        return jnp.max(s, axis=0, keepdims=True)

    def span_consume(i, buf, smax, m, l, acc):
        m_new = jnp.maximum(m, smax)
        alpha = jnp.exp2(m - m_new)
        vt = jnp.concatenate([vts_ref[cps * i + k] for k in range(cps)], axis=1)
        pv, psum = pv_and_sum(vt, jnp.exp2(buf[...] - m_new))
        return m_new, l * alpha + psum, acc * alpha + pv

    n_pair = (n_span + 1) // 2

    def pair(k, carry, prefetch):
        m, l, acc, smax0 = carry
        smax1 = span_scores(2 * k + 1, sb_ref)
        m, l, acc = span_consume(2 * k, sa_ref, smax0, m, l, acc)
        smax0 = span_scores(2 * k + 2, sa_ref) if prefetch else smax0
        m, l, acc = span_consume(2 * k + 1, sb_ref, smax1, m, l, acc)
        return m, l, acc, smax0

    carry = lax.fori_loop(0, n_pair - 1, lambda k, c: pair(k, c, True), (m, l, acc, span_scores(0, sa_ref)))
    m, l, acc, _ = pair(jnp.maximum(n_pair - 1, 0), carry, False)
    o_s = acc / l

    c0 = jnp.maximum(qb - WINDOW // Q_BLOCK, 0)
    wstart = pl.multiple_of(c0 * Q_BLOCK, Q_BLOCK)
    sw = jnp.dot(kaw_ref[0, pl.ds(wstart, WIN_KEYS), :], rhs_q, preferred_element_type=F32)
    wbias = []
    for i in range(WIN_KEYS // Q_BLOCK):
        d = qb - c0 - i
        wbias.append(jnp.where(d == WINDOW // Q_BLOCK, b_first,
                               jnp.where(d == 0, b_diag, jnp.where(d < 0, NEG, 0.0))))
    sw = sw + heads(jnp.concatenate(wbias, axis=0))
    vtw = jnp.concatenate([vtw_ref[c0 + i] for i in range(WIN_KEYS // Q_BLOCK)], axis=1)
    o_w, lw = pv_and_sum(vtw, jnp.exp2(sw - jnp.max(sw, axis=0, keepdims=True)))
    o_w = o_w / lw

    gate = _sigmoid(gnt_ref[0])
    outs = []
    for h in range(HPG):
        sl = slice(h * Q_BLOCK, (h + 1) * Q_BLOCK)
        outs.append(gate[3 * h:3 * h + 1] * o_c[:, sl] + gate[3 * h + 1:3 * h + 2] * o_s[:, sl]
                    + gate[3 * h + 2:3 * h + 3] * o_w[:, sl])
    o_t = jnp.concatenate(outs, axis=0)
    o_ref[...] = o_t.T.astype(BF16)


def _nsa_prompt(qt, gnt, kc, vct, kas, vts, kaw, vtw, bsz, seq_len):
    nq = seq_len // Q_BLOCK
    nsel = seq_len // SEL_BLOCK
    nc_valid = seq_len // CMP_STRIDE - 1
    ncp = kc.shape[2]
    k_sel = min(N_SELECT, nsel)
    a = jnp.asarray(_sel_matrix(nsel, nc_valid, ncp), BF16)
    n = bsz * seq_len
    return pl.pallas_call(
        functools.partial(_nsa_prompt_kernel, nc_valid, k_sel),
        grid=(bsz, N_KV, nq),
        in_specs=[
            pl.BlockSpec((1, HPG * HEAD_DIM, LANES), lambda b, g, i: (b * nq + i, g, 0)),
            pl.BlockSpec((1, 16, LANES), lambda b, g, i: (b * nq + i, g, 0)),
            pl.BlockSpec((1, 1, ncp, LANES), lambda b, g, i: (b, g, 0, 0)),
            pl.BlockSpec((1, 1, HEAD_DIM, ncp), lambda b, g, i: (b, g, 0, 0)),
            pl.BlockSpec((1, seq_len, LANES), lambda b, g, i: (g, b, 0)),
            pl.BlockSpec((nq, HEAD_DIM, LANES), lambda b, g, i: (b, g, 0)),
            pl.BlockSpec((1, seq_len, LANES), lambda b, g, i: (g, b, 0)),
            pl.BlockSpec((nq, HEAD_DIM, LANES), lambda b, g, i: (b, g, 0)),
            pl.BlockSpec(a.shape, lambda b, g, i: (0, 0)),
        ],
        out_specs=pl.BlockSpec((Q_BLOCK, HPG * HEAD_DIM), lambda b, g, i: (b * nq + i, g)),
        out_shape=jax.ShapeDtypeStruct((n, D_Q), BF16),
        scratch_shapes=[pltpu.VMEM((SWEEP, HPG * Q_BLOCK), F32), pltpu.VMEM((SWEEP, HPG * Q_BLOCK), F32)],
        compiler_params=_cparams(("parallel", "parallel", "arbitrary")), name="nsa_prompt",
    )(qt, gnt, kc, vct, kas, vts, kaw, vtw, a)


S5_L = 16
S5_W = S5_L * S5_CH
S5_P = 2 * S5_STATE


def _s5_prep_kernel(are_ref, aim_ref, ldt_ref, bre_ref, bim_ref, cre_ref, cim_ref,
                    tg_ref, sg_ref, ogt_ref, misc_ref, bs_ref, oct_ref):
    are, aim = are_ref[0], aim_ref[0]
    dt = jnp.exp(ldt_ref[0])
    mag = jnp.exp(are * dt)
    ar, ai = mag * jnp.cos(aim * dt), mag * jnp.sin(aim * dt)
    den = are * are + aim * aim
    fr = ((ar - 1.0) * are + ai * aim) / den
    fi = (ai * are - (ar - 1.0) * aim) / den
    bre, bim = bre_ref[0], bim_ref[0]
    cre, cim = cre_ref[0], cim_ref[0]
    br, bi = fr * bre - fi * bim, fr * bim + fi * bre
    lo16 = lax.broadcasted_iota(jnp.int32, (S5_CH, S5_P), 1) < S5_STATE
    lo1 = lax.broadcasted_iota(jnp.int32, (1, S5_P), 1) < S5_STATE
    pr, pi = [jnp.ones_like(ar)], [jnp.zeros_like(ar)]
    for _ in range(S5_L):
        pr.append(pr[-1] * ar - pi[-1] * ai)
        pi.append(pr[-2] * ai + pi[-1] * ar)
    cpr = [cre * pr[k] - cim * pi[k] for k in range(S5_L + 1)]
    cpi = [cre * pi[k] + cim * pr[k] for k in range(S5_L + 1)]
    rpack = jnp.concatenate([jnp.where(lo16, cpr[k], cpi[k]) for k in range(S5_L)], axis=0)
    bpack = jnp.where(lo16, br, -bi)
    krow = lax.dot_general(bpack, rpack, (((1,), (1,)), ((), ())), preferred_element_type=F32,
                           precision=lax.Precision.HIGHEST)
    lane = lax.broadcasted_iota(jnp.int32, (S5_CH, S5_W), 1)
    for j in range(S5_L):
        shifted = krow if j == 0 else pltpu.roll(krow, j * S5_CH, 1)
        tg_ref[0, j * S5_CH:(j + 1) * S5_CH, :] = jnp.where(lane >= j * S5_CH, shifted, 0.0).astype(BF16)
        k = S5_L - 1 - j
        sblk = jnp.where(lo16, pr[k] * br - pi[k] * bi, pr[k] * bi + pi[k] * br)
        sg_ref[0, j * S5_CH:(j + 1) * S5_CH, :] = sblk.astype(BF16)
        if j == S5_L - 1:
            bs_ref[0] = sblk
        ogt_ref[0, j * S5_CH:(j + 1) * S5_CH, :] = jnp.where(lo16, cpr[j + 1], -cpi[j + 1]).astype(BF16)
    oct_ref[0] = jnp.where(lo16, cre, -cim)
    misc_ref[0] = jnp.concatenate([
        pr[S5_L], jnp.where(lo1, -pi[S5_L], pi[S5_L]), ar, jnp.where(lo1, -ai, ai),
        jnp.zeros((4, S5_P), F32)], axis=0)


def _s5_prep(a_re, a_im, log_dt, b_re, b_im, c_re, c_im):
    g = S5_GROUPS
    dup = lambda a: jnp.concatenate([a, a], axis=-1)
    are, aim = dup(a_re).reshape(g, 1, S5_P), dup(a_im).reshape(g, 1, S5_P)
    ldt = jnp.broadcast_to(log_dt.reshape(g, 1, 1), (g, 1, S5_P))
    bre, bim = dup(jnp.swapaxes(b_re, 1, 2)), dup(jnp.swapaxes(b_im, 1, 2))
    cre, cim = dup(c_re), dup(c_im)
    v1 = pl.BlockSpec((1, 1, S5_P), lambda i: (i, 0, 0))
    v16 = pl.BlockSpec((1, S5_CH, S5_P), lambda i: (i, 0, 0))
    return pl.pallas_call(
        _s5_prep_kernel, grid=(g,),
        in_specs=[v1, v1, v1, v16, v16, v16, v16],
        out_specs=(pl.BlockSpec((1, S5_W, S5_W), lambda i: (i, 0, 0)), pl.BlockSpec((1, S5_W, S5_P), lambda i: (i, 0, 0)),
                   pl.BlockSpec((1, S5_W, S5_P), lambda i: (i, 0, 0)), pl.BlockSpec((1, 8, S5_P), lambda i: (i, 0, 0)),
                   v16, v16),
        out_shape=(jax.ShapeDtypeStruct((g, S5_W, S5_W), BF16), jax.ShapeDtypeStruct((g, S5_W, S5_P), BF16),
                   jax.ShapeDtypeStruct((g, S5_W, S5_P), BF16), jax.ShapeDtypeStruct((g, 8, S5_P), F32),
                   jax.ShapeDtypeStruct((g, S5_CH, S5_P), F32), jax.ShapeDtypeStruct((g, S5_CH, S5_P), F32)),
        compiler_params=_cparams(("parallel",)), name="s5_prep",
    )(are, aim, ldt, bre, bim, cre, cim)


def _s5_sum_kernel(u_ref, sg_ref, s_ref):
    s_ref[0] = jnp.dot(u_ref[0].astype(BF16), sg_ref[0], preferred_element_type=F32)


def _s5_scan_kernel(s_ref, a1_ref, a2_ref, h_ref, last_ref, carry):
    @pl.when(pl.program_id(0) == 0)
    def _():
        carry[...] = jnp.zeros_like(carry)

    a1, a2 = a1_ref[...], a2_ref[...]

    def step(c, h):
        h_ref[c] = h
        return a1 * h + a2 * pltpu.roll(h, S5_STATE, 1) + s_ref[c]

    h = lax.fori_loop(0, s_ref.shape[0], step, carry[...])
    carry[...] = h
    last_ref[...] = h


def _s5_out_kernel(u_ref, h_ref, tg_ref, ogt_ref, d_ref, y_ref):
    u = u_ref[0]
    y = jnp.dot(u.astype(BF16), tg_ref[0], preferred_element_type=F32)
    y = y + lax.dot_general(h_ref[0].astype(BF16), ogt_ref[0], (((1,), (1,)), ((), ())), preferred_element_type=F32)
    y_ref[0] = y + d_ref[0] * u


def _s5_prompt(u, ops, s5_d, bsz, seq_len):
    tg, sg, ogt, misc = ops[0], ops[1], ops[2], ops[3]
    g, nch = S5_GROUPS, seq_len // S5_L
    rows = bsz * nch
    ug = u.reshape(bsz, nch, S5_L, g, S5_CH).transpose(3, 0, 1, 2, 4).reshape(g, rows, S5_W)
    gspec = lambda r, c: pl.BlockSpec((1, r, c), lambda i: (i, 0, 0))
    ssum = pl.pallas_call(
        _s5_sum_kernel, grid=(g,), in_specs=[gspec(rows, S5_W), gspec(S5_W, S5_P)],
        out_specs=gspec(rows, S5_P), out_shape=jax.ShapeDtypeStruct((g, rows, S5_P), F32),
        compiler_params=_cparams(("parallel",)), name="s5_sum",
    )(ug, sg)
    s_cm = ssum.reshape(g, bsz, nch, S5_P).transpose(2, 1, 0, 3).reshape(nch, bsz * g, S5_P)
    a1 = jnp.tile(misc[:, 0, :], (bsz, 1))
    a2 = jnp.tile(misc[:, 1, :], (bsz, 1))
    cb = min(nch, 32)
    hs, last = pl.pallas_call(
        _s5_scan_kernel, grid=(nch // cb,),
        in_specs=[pl.BlockSpec((cb, bsz * g, S5_P), lambda i: (i, 0, 0)),
                  pl.BlockSpec((bsz * g, S5_P), lambda i: (0, 0)), pl.BlockSpec((bsz * g, S5_P), lambda i: (0, 0))],
        out_specs=(pl.BlockSpec((cb, bsz * g, S5_P), lambda i: (i, 0, 0)), pl.BlockSpec((bsz * g, S5_P), lambda i: (0, 0))),
        out_shape=(jax.ShapeDtypeStruct((nch, bsz * g, S5_P), F32), jax.ShapeDtypeStruct((bsz * g, S5_P), F32)),
        scratch_shapes=[pltpu.VMEM((bsz * g, S5_P), F32)],
        compiler_params=_cparams(("arbitrary",)), name="s5_scan",
    )(s_cm, a1, a2)
    h_g = hs.reshape(nch, bsz, g, S5_P).transpose(2, 1, 0, 3).reshape(g, rows, S5_P)
    dvec = jnp.tile(s5_d.reshape(g, 1, S5_CH), (1, 1, S5_L))
    yg = pl.pallas_call(
        _s5_out_kernel, grid=(g,),
        in_specs=[gspec(rows, S5_W), gspec(rows, S5_P), gspec(S5_W, S5_W), gspec(S5_W, S5_P), gspec(1, S5_W)],
        out_specs=gspec(rows, S5_W), out_shape=jax.ShapeDtypeStruct((g, rows, S5_W), F32),
        compiler_params=_cparams(("parallel",)), name="s5_out",
    )(ug, h_g, tg, ogt, dvec)
    y = yg.reshape(g, bsz, nch, S5_L, S5_CH).transpose(1, 2, 3, 0, 4).reshape(bsz * seq_len, S5_WIDTH)
    state = last.reshape(bsz, g, 2, S5_STATE).transpose(0, 1, 3, 2)
    return y, state


def _s5_sample_kernel(u_ref, h0_ref, bs_ref, oct_ref, misc_ref, d_ref, y_ref, h1_ref):
    u, h0 = u_ref[0], h0_ref[0]
    hi = lax.Precision.HIGHEST
    bu = jnp.dot(u, bs_ref[0], preferred_element_type=F32, precision=hi)
    h1 = misc_ref[0, 2:3] * h0 + misc_ref[0, 3:4] * pltpu.roll(h0, S5_STATE, 1) + bu
    h1_ref[0] = h1
    y = lax.dot_general(h1, oct_ref[0], (((1,), (1,)), ((), ())), preferred_element_type=F32, precision=hi)
    y_ref[0] = y + d_ref[0] * u


def _s5_sample(u, state, ops, s5_d):
    misc, bs, oct_ = ops[3], ops[4], ops[5]
    s, g = u.shape[0], S5_GROUPS
    ug = u.reshape(s, g, S5_CH).transpose(1, 0, 2)
    h0 = state.astype(F32).transpose(1, 0, 3, 2).reshape(g, s, S5_P)
    gspec = lambda r, c: pl.BlockSpec((1, r, c), lambda i: (i, 0, 0))
    y, h1 = pl.pallas_call(
        _s5_sample_kernel, grid=(g,),
        in_specs=[gspec(s, S5_CH), gspec(s, S5_P), gspec(S5_CH, S5_P), gspec(S5_CH, S5_P), gspec(8, S5_P), gspec(1, S5_CH)],
        out_specs=(gspec(s, S5_CH), gspec(s, S5_P)),
        out_shape=(jax.ShapeDtypeStruct((g, s, S5_CH), F32), jax.ShapeDtypeStruct((g, s, S5_P), F32)),
        compiler_params=_cparams(("parallel",)), name="s5_sample",
    )(ug, h0, bs, oct_, misc, s5_d.reshape(g, 1, S5_CH))
    return (y.transpose(1, 0, 2).reshape(s, S5_WIDTH),
            h1.reshape(g, s, 2, S5_STATE).transpose(1, 0, 3, 2))


def _merge_mlp_kernel(x_ref, o_ref, ys_ref, gm_ref, wglu_ref, bglu_ref, wbn_ref, wbs_ref, wout_ref, wup_ref, wdn_ref,
                      npost_ref, nmpre_ref, nmpost_ref, out_ref):
    dot = lambda a, w_ref: jnp.dot(a.astype(BF16), w_ref[...], preferred_element_type=F32)
    z = _gelu(ys_ref[...])
    o_s5 = z * _sigmoid(dot(z, wglu_ref) + bglu_ref[...])
    merged = (_sigmoid(gm_ref[:, 0:D_MODEL]) * dot(o_ref[...], wbn_ref)
              + _sigmoid(gm_ref[:, D_MODEL:2 * D_MODEL]) * dot(o_s5, wbs_ref))
    x1 = x_ref[...] + _rms(dot(merged, wout_ref), npost_ref[...])
    hm = _rms(x1, nmpre_ref[...])
    up = jnp.maximum(dot(hm, wup_ref), 0.0)
    f = dot(up * up, wdn_ref)
    out_ref[...] = x1 + _rms(f, nmpost_ref[...])


def _merge_mlp(x2, o_nsa, y_s5, gm, wts, tm=256):
    n = x2.shape[0]
    tm = min(tm, n)
    row = lambda w: pl.BlockSpec((tm, w), lambda i: (i, 0))
    const = lambda a: pl.BlockSpec(a.shape, lambda i: (0, 0), pipeline_mode=pl.Buffered(1))
    return pl.pallas_call(
        _merge_mlp_kernel, grid=(n // tm,),
        in_specs=[row(D_MODEL), row(D_Q), row(S5_WIDTH), row(2 * D_MODEL)] + [const(a) for a in wts],
        out_specs=row(D_MODEL), out_shape=jax.ShapeDtypeStruct((n, D_MODEL), F32),
        compiler_params=_cparams(("parallel",)), name="merge_mlp",
    )(x2, o_nsa, y_s5, gm, *wts)


def _merge_weights(s5_w_glu, s5_b_glu, w_branch_nsa, w_branch_s5, w_out, w_mlp_up, w_mlp_down,
                   norm_mix_post, norm_mlp_pre, norm_mlp_post):
    r = lambda v: v.reshape(1, -1).astype(F32)
    b = lambda w: w.astype(BF16)
    return (b(s5_w_glu), r(s5_b_glu), b(w_branch_nsa), b(w_branch_s5), b(w_out), b(w_mlp_up), b(w_mlp_down),
            r(norm_mix_post), r(norm_mlp_pre), r(norm_mlp_post))


def _inproj_sample_kernel(x_ref, g_ref, wr_ref, wt_ref, kv_ref, u_ref, gm_ref, qg_ref):
    hb = _rms(x_ref[...], g_ref[...]).astype(BF16)
    u_ref[...] = jnp.dot(hb, wr_ref[:, 512:1024], preferred_element_type=F32)
    gm_ref[...] = jnp.dot(hb, wr_ref[:, 1024:3072], preferred_element_type=F32)
    z = lax.dot_general(hb, wt_ref[...], (((1,), (1,)), ((), ())), preferred_element_type=F32)
    qg_ref[...] = z[:, 0:WT_KV]
    kv_ref[...] = z[:, WT_KV:WT_ROWS]


def _inproj_sample(x2, g_pre, w_row, w_t):
    s = x2.shape[0]
    full = lambda a: pl.BlockSpec(a.shape, lambda i: (0, 0))
    o = lambda w: pl.BlockSpec((s, w), lambda i: (0, 0))
    return pl.pallas_call(
        _inproj_sample_kernel, grid=(1,),
        in_specs=[o(D_MODEL), pl.BlockSpec((1, D_MODEL), lambda i: (0, 0)), full(w_row), full(w_t)],
        out_specs=(o(1536), o(512), o(2048), o(WT_KV)),
        out_shape=(jax.ShapeDtypeStruct((s, 1536), F32), jax.ShapeDtypeStruct((s, 512), F32),
                   jax.ShapeDtypeStruct((s, 2048), F32), jax.ShapeDtypeStruct((s, WT_KV), F32)),
        compiler_params=_cparams(("arbitrary",)), name="inproj_sample",
    )(x2, g_pre.reshape(1, D_MODEL), w_row, w_t)


CMP_PAGES_PER_STEP = 16


def _cmp_y_sample_kernel(pps, pt_ref, *refs):
    x_refs, w_ref, y_ref, slab = refs[:pps], refs[pps], refs[pps + 1], refs[pps + 2]
    for r, x in enumerate(x_refs):
        for combo in range(4):
            kv, gp = combo // 2, combo % 2
            pair = x[0, kv, 2 * gp:2 * gp + 2].reshape(2 * HEAD_DIM, PAGE_SIZE)
            slab[combo, r * PAGE_SIZE:(r + 1) * PAGE_SIZE, :] = pair.T
    _cmp_y_from_slabs(slab, w_ref, y_ref)


def _cmp_y_sample(cache_t, page_table, wcmp):
    s, n_pages = page_table.shape
    cpp = PAGE_SIZE // CMP_STRIDE
    pps = CMP_PAGES_PER_STEP
    steps = n_pages // pps

    def page_spec(r):
        return pl.BlockSpec((1, 2, N_KV, HEAD_DIM, PAGE_SIZE), lambda b, j, pt: (pt[b, j * pps + r], 0, 0, 0, 0))

    grid_spec = pltpu.PrefetchScalarGridSpec(
        num_scalar_prefetch=1, grid=(s, steps),
        in_specs=[page_spec(r) for r in range(pps)] + [pl.BlockSpec(wcmp.shape, lambda b, j, pt: (0, 0, 0))],
        out_specs=pl.BlockSpec((1, pps * cpp, 1024), lambda b, j, pt: (b, j, 0)),
        scratch_shapes=[pltpu.VMEM((4, pps * PAGE_SIZE, LANES), F32)],
    )
    return pl.pallas_call(
        functools.partial(_cmp_y_sample_kernel, pps), grid_spec=grid_spec,
        out_shape=jax.ShapeDtypeStruct((s, n_pages * cpp, 1024), F32),
        compiler_params=_cparams(("parallel", "parallel")), name="cmp_y_sample",
    )(page_table, *([cache_t] * pps), wcmp)


def _nsa_sample_cmp_kernel(nc_valid, qbd_ref, kc_ref, vct_ref, a_ref, oc_ref, isel_ref):
    ncp = kc_ref.shape[2]
    sc = jnp.dot(kc_ref[0, 0], qbd_ref[0, 0], preferred_element_type=F32)
    for g in range(1, N_KV):
        sc = sc + jnp.dot(kc_ref[0, g], qbd_ref[0, g], preferred_element_type=F32)
    nrow = lax.broadcasted_iota(jnp.int32, (ncp, LANES), 0)
    cmask = nrow < nc_valid
    sc = jnp.where(cmask, sc, NEG)
    e = jnp.where(cmask, jnp.exp(sc - jnp.max(sc, axis=0, keepdims=True)), 0.0)
    p = e / jnp.sum(e, axis=0, keepdims=True)
    pb = p.astype(BF16)
    for g in range(N_KV):
        oc_ref[0, g] = jnp.dot(vct_ref[0, g], pb, preferred_element_type=F32)
    a = a_ref[...]
    r = sum(jnp.dot(a, part, preferred_element_type=F32) for part in _split3(p))
    tot = r
    for h in range(1, HPG):
        tot = tot + pltpu.roll(r, LANES - h * N_KV, 1)
    isel_ref[0] = tot


def _topk_sample_kernel(nsel, cur, k_sel, isel_ref, tri_ref, idx_ref):
    jp = isel_ref.shape[0]
    jrow = lax.broadcasted_iota(jnp.int32, (jp, LANES), 0)
    forced = (jrow == 0) | (jrow == cur) | (jrow == cur - 1)
    score = jnp.where(jrow <= cur, jnp.where(forced, FORCE, isel_ref[...]), -1.0)
    score = jnp.where(jrow < nsel, score, -2.0)
    sel = _rank_select(score, k_sel)
    rank = jnp.dot(tri_ref[...], sel.astype(BF16), preferred_element_type=F32)
    jf = jrow.astype(F32)
    rows = [jnp.sum(jnp.where((sel > 0.5) & (rank == float(r + 1)), jf, 0.0), axis=0, keepdims=True)
            for r in range(k_sel)]
    idx_ref[...] = jnp.concatenate(rows, axis=0).astype(jnp.int32)


def _nsa_sample_attn_kernel(n_cache, tbl_ref, q_ref, *refs):
    blocks = refs[:N_KV * n_cache]
    kvs_ref, win_ref, kvw_ref, kvwc_ref, oc_ref, gate_ref, o_ref, wout_ref = refs[N_KV * n_cache:]
    b = pl.program_id(0)
    nt = (((1,), (1,)), ((), ()))
    lane = lax.broadcasted_iota(jnp.int32, (8, PAGE_SIZE), 1)
    kvw_new = kvw_ref[0]
    kvs_new = kvs_ref[0]
    nwin = win_ref.shape[4]
    wlane = lax.broadcasted_iota(jnp.int32, (HEAD_DIM, nwin), 1)
    for c in range(2 * N_KV):
        shifted = pltpu.roll(win_ref[0, c // N_KV, c % N_KV], nwin - 1, 1)
        wout_ref[0, c // N_KV, c % N_KV] = jnp.where(wlane == nwin - 1, kvwc_ref[0, c], shifted)
    for g in range(N_KV):
        col = g * HEAD_DIM
        q = q_ref[0, g]
        qf = q.astype(F32)
        pages = blocks[g * n_cache:(g + 1) * n_cache]
        kt = jnp.concatenate([pg[0, 0, 0].astype(BF16) for pg in pages], axis=1)
        vt = jnp.concatenate([pg[0, 1, 0].astype(BF16) for pg in pages], axis=1)
        bias = []
        for r in range(n_cache):
            half = lax.rem(tbl_ref[(b * N_KV + g) * n_cache + r], 2)
            bias.append(jnp.where((lane >= half * SEL_BLOCK) & (lane < (half + 1) * SEL_BLOCK), 0.0, NEG))
        s = jnp.dot(q, kt, preferred_element_type=F32) + jnp.concatenate(bias, axis=1)
        kn = kvs_new[:, col:col + HEAD_DIM].astype(BF16).astype(F32)
        vn = kvs_new[:, 256 + col:256 + col + HEAD_DIM].astype(BF16).astype(F32)
        s_new = jnp.sum(qf * kn, axis=1, keepdims=True)
        m_s = jnp.maximum(jnp.max(s, axis=1, keepdims=True), s_new)
        p = jnp.exp(s - m_s)
        p_new = jnp.exp(s_new - m_s)
        l_s = jnp.sum(p, axis=1, keepdims=True) + p_new
        o_s = (lax.dot_general(p.astype(BF16), vt, nt, preferred_element_type=F32)
               + p_new.astype(BF16).astype(F32) * vn) / l_s
        kw = win_ref[0, 0, g].astype(BF16)
        vw = win_ref[0, 1, g].astype(BF16)
        sw = jnp.dot(q, kw, preferred_element_type=F32)
        keep = lax.broadcasted_iota(jnp.int32, sw.shape, 1) >= 1
        sw = jnp.where(keep, sw, NEG)
        kwn = kvw_new[:, col:col + HEAD_DIM].astype(BF16).astype(F32)
        vwn = kvw_new[:, 256 + col:256 + col + HEAD_DIM].astype(BF16).astype(F32)
        sw_new = jnp.sum(qf * kwn, axis=1, keepdims=True)
        mw = jnp.maximum(jnp.max(sw, axis=1, keepdims=True), sw_new)
        pw = jnp.where(keep, jnp.exp(sw - mw), 0.0)
        pw_new = jnp.exp(sw_new - mw)
        lw = jnp.sum(pw, axis=1, keepdims=True) + pw_new
        o_w = (lax.dot_general(pw.astype(BF16), vw, nt, preferred_element_type=F32)
               + pw_new.astype(BF16).astype(F32) * vwn) / lw
        gate = _sigmoid(gate_ref[0, g])
        o_ref[0, g] = gate[0] * oc_ref[0, g] + gate[1] * o_s + gate[2] * o_w


def sample_attention(x_sample, cmp_t, slc_t, win_t, page_table, norm_mix_pre, w_row, w_t,
                     wcmp, cmp_pe_k, cmp_w1_k, cmp_w2_k, cmp_pe_v, cmp_w1_v, cmp_w2_v):
    s = x_sample.shape[0]
    n_pages = page_table.shape[1]
    past = n_pages * PAGE_SIZE
    assert past % SEL_BLOCK == 0 and x_sample.shape[1] == 1
    kv, u, gm, qg = _inproj_sample(x_sample.reshape(s, D_MODEL), norm_mix_pre, w_row, w_t)
    q = qg[:, 0:D_Q].reshape(s, N_KV, HPG, HEAD_DIM) * (HEAD_DIM ** -0.5)
    gn = qg[:, D_Q:].reshape(s, N_KV, 16)[:, :, 0:12].reshape(s, N_KV, HPG, 3)
    kvc_new, kvs_new, kvw_new = kv[:, 0:512], kv[:, 512:1024], kv[:, 1024:1536]

    y = _cmp_y_sample(cmp_t, page_table, wcmp)
    kc, vct = _cmp_combine(y, cmp_pe_k, cmp_w1_k, cmp_w2_k, cmp_pe_v, cmp_w1_v, cmp_w2_v)
    ncp = kc.shape[2]
    nc_valid = (past + 1) // CMP_STRIDE - 1
    nsel = -(-(past + 1) // SEL_BLOCK)
    jp = -(-nsel // 8) * 8
    cur = past // SEL_BLOCK
    k_sel = min(N_SELECT, nsel)
    assert ncp >= nc_valid and (nc_valid - 1) * CMP_STRIDE + CMP_BLOCK - 1 <= past

    qb16 = q.astype(BF16)
    qbd = jnp.zeros((s, N_KV, LANES, LANES), BF16)
    gidx = jnp.arange(N_KV)
    lane_gh = gidx[:, None] + jnp.arange(HPG)[None, :] * N_KV
    qbd = qbd.at[:, gidx[:, None, None], jnp.arange(HEAD_DIM)[None, None, :], lane_gh[:, :, None]].set(qb16)
    pad_rows = lambda a: jnp.pad(a, [(0, 0)] * (a.ndim - 2) + [(0, 8 - HPG), (0, 0)])
    qrow = pad_rows(qb16)

    a = jnp.asarray(np.pad(_sel_matrix(nsel, nc_valid, ncp), ((0, jp - nsel), (0, 0))), BF16)
    oc_t, isel = pl.pallas_call(
        functools.partial(_nsa_sample_cmp_kernel, nc_valid), grid=(s,),
        in_specs=[pl.BlockSpec((1, N_KV, LANES, LANES), lambda b: (b, 0, 0, 0)),
                  pl.BlockSpec((1, N_KV, ncp, LANES), lambda b: (b, 0, 0, 0)),
                  pl.BlockSpec((1, N_KV, HEAD_DIM, ncp), lambda b: (b, 0, 0, 0)),
                  pl.BlockSpec(a.shape, lambda b: (0, 0))],
        out_specs=(pl.BlockSpec((1, N_KV, HEAD_DIM, LANES), lambda b: (b, 0, 0, 0)),
                   pl.BlockSpec((1, jp, LANES), lambda b: (b, 0, 0))),
        out_shape=(jax.ShapeDtypeStruct((s, N_KV, HEAD_DIM, LANES), F32), jax.ShapeDtypeStruct((s, jp, LANES), F32)),
        compiler_params=_cparams(("parallel",)), name="nsa_sample_cmp",
    )(qbd, kc, vct, a)

    assert s * N_KV == LANES
    isel_t = isel[:, :, 0:N_KV].transpose(1, 0, 2).reshape(jp, s * N_KV)
    tri = jnp.asarray(np.tril(np.ones((jp, jp), np.float32)), BF16)
    idx = pl.pallas_call(
        functools.partial(_topk_sample_kernel, nsel, cur, k_sel), grid=(1,),
        in_specs=[pl.BlockSpec((jp, LANES), lambda i: (0, 0)), pl.BlockSpec((jp, jp), lambda i: (0, 0))],
        out_specs=pl.BlockSpec((k_sel, LANES), lambda i: (0, 0)),
        out_shape=jax.ShapeDtypeStruct((k_sel, LANES), jnp.int32),
        compiler_params=_cparams(("arbitrary",)), name="topk_sample",
    )(isel_t, tri)
    n_cache = k_sel - 1
    blk = idx[0:n_cache].T.reshape(s, N_KV, n_cache)
    page = jnp.take_along_axis(page_table, (blk // 2).reshape(s, -1), axis=1).reshape(s, N_KV, n_cache)
    tbl = (page * 2 + blk % 2).astype(jnp.int32).reshape(s * N_KV * n_cache)

    oc_g = oc_t[:, gidx[:, None, None], jnp.arange(HEAD_DIM)[None, None, :], lane_gh[:, :, None]]
    oc_row = pad_rows(oc_g)
    gate_in = pad_rows(jnp.broadcast_to(gn.transpose(0, 1, 3, 2)[..., None], (s, N_KV, 3, HPG, HEAD_DIM)))
    wrows = win_t.shape[4]

    def blk_spec(g, r):
        return pl.BlockSpec((1, 2, 1, HEAD_DIM, PAGE_SIZE),
                            lambda b, t: (t[(b * N_KV + g) * n_cache + r] // 2, 0, g, 0, 0))

    per_b = lambda shape: pl.BlockSpec((1,) + shape, lambda b, t: (b,) + (0,) * len(shape))
    win_shape = (2, N_KV, HEAD_DIM, wrows)
    n_blk = N_KV * n_cache
    grid_spec = pltpu.PrefetchScalarGridSpec(
        num_scalar_prefetch=1, grid=(s,),
        in_specs=[per_b((N_KV, 8, HEAD_DIM))] + [blk_spec(g, r) for g in range(N_KV) for r in range(n_cache)]
                 + [per_b((1, 512)), per_b(win_shape), per_b((1, 512)), per_b((2 * N_KV, HEAD_DIM, 1)),
                    per_b((N_KV, 8, HEAD_DIM)), per_b((N_KV, 3, 8, HEAD_DIM))],
        out_specs=(per_b((N_KV, 8, HEAD_DIM)), per_b(win_shape)),
    )
    o_row, win_out = pl.pallas_call(
        functools.partial(_nsa_sample_attn_kernel, n_cache), grid_spec=grid_spec,
        out_shape=(jax.ShapeDtypeStruct((s, N_KV, 8, HEAD_DIM), F32), jax.ShapeDtypeStruct((s,) + win_shape, F32)),
        compiler_params=_cparams(("parallel",)), name="nsa_sample_attn",
    )(tbl, qrow, *([slc_t] * n_blk), kvs_new.reshape(s, 1, 512), win_t,
      kvw_new.reshape(s, 1, 512), kvw_new.reshape(s, 2 * N_KV, HEAD_DIM, 1), oc_row, gate_in)
    o = o_row[:, :, 0:HPG, :]
    return o.reshape(s, D_Q).astype(BF16), kvc_new, kvs_new, win_out, u, gm


def prompt_attention(x_prompt, norm_mix_pre, w_row, w_t, wcmp, cmp_pe_k, cmp_w1_k, cmp_w2_k, cmp_pe_v, cmp_w1_v, cmp_w2_v):
    bsz, seq_len, _ = x_prompt.shape
    assert seq_len % (2 * SWEEP) == 0 and WIN_KEYS <= seq_len <= SEL_BLOCK * HEAD_DIM
    n = bsz * seq_len
    (u, gm, kas, kaw, slabs, qt, vts, vtw, gnt, kvct, kvst, kvwt) = _inproj_prompt(
        x_prompt.reshape(n, D_MODEL), norm_mix_pre, w_row, w_t, bsz, seq_len)
    y = _cmp_y_prompt(slabs, bsz, seq_len, wcmp)
    kc, vct = _cmp_combine(y, cmp_pe_k, cmp_w1_k, cmp_w2_k, cmp_pe_v, cmp_w1_v, cmp_w2_v)
    o = _nsa_prompt(qt, gnt, kc, vct, kas, vts, kaw, vtw, bsz, seq_len)
    return o, kvct, kvst, kvwt, u, gm


def kernel(x_prompt, x_sample, cache_kv_cmp, cache_kv_slc, state_kv_win, state_s5, page_table, norm_mix_pre, norm_mix_post, norm_mlp_pre, norm_mlp_post, w_in, cmp_pe_k, cmp_w1_k, cmp_w2_k, cmp_pe_v, cmp_w1_v, cmp_w2_v, s5_a_re, s5_a_im, s5_log_dt, s5_b_re, s5_b_im, s5_c_re, s5_c_im, s5_d, s5_w_glu, s5_b_glu, w_branch_nsa, w_branch_s5, w_out, w_mlp_up, w_mlp_down):
    bsz, seq_len, _ = x_prompt.shape
    s = x_sample.shape[0]
    w_row, w_t = _inproj_weights(w_in)
    wcmp = _cmp_weights(cmp_w1_k, cmp_w1_v)
    cmp_w = (cmp_pe_k, cmp_w1_k, cmp_w2_k, cmp_pe_v, cmp_w1_v, cmp_w2_v)
    s5_ops = _s5_prep(s5_a_re, s5_a_im, s5_log_dt, s5_b_re, s5_b_im, s5_c_re, s5_c_im)
    mlp_w = _merge_weights(s5_w_glu, s5_b_glu, w_branch_nsa, w_branch_s5, w_out, w_mlp_up, w_mlp_down,
                           norm_mix_post, norm_mlp_pre, norm_mlp_post)

    o_p, kvc_p, kvs_p, kvw_p, u_p, gm_p = prompt_attention(x_prompt, norm_mix_pre, w_row, w_t, wcmp, *cmp_w)
    ys5_p, s5_p = _s5_prompt(u_p, s5_ops, s5_d, bsz, seq_len)
    y_p = _merge_mlp(x_prompt.reshape(bsz * seq_len, D_MODEL), o_p, ys5_p, gm_p, mlp_w)

    feature_major = lambda c: jnp.transpose(c, (0, 2, 3, 4, 1))
    o_s, kvc_s, kvs_s, win_t, u_s, gm_s = sample_attention(
        x_sample, feature_major(cache_kv_cmp), feature_major(cache_kv_slc), feature_major(state_kv_win),
        page_table, norm_mix_pre, w_row, w_t, wcmp, *cmp_w)
    win_s = jnp.transpose(win_t, (0, 4, 1, 2, 3))
    ys5_s, s5_s = _s5_sample(u_s, state_s5, s5_ops, s5_d)
    y_s = _merge_mlp(x_sample.reshape(s, D_MODEL), o_s, ys5_s, gm_s, mlp_w)

    kv5 = lambda a, b, t: a.reshape(b, t, 2, N_KV, HEAD_DIM)
    token_major = lambda a: jnp.transpose(a.reshape(bsz, 2, N_KV, HEAD_DIM, -1), (0, 4, 1, 2, 3))
    win_rows = min(WINDOW, seq_len)
    win_p = token_major(kvw_p[:, :, seq_len - win_rows:])
    if win_rows < WINDOW:
        win_p = jnp.pad(win_p, ((0, 0), (WINDOW - win_rows, 0), (0, 0), (0, 0), (0, 0)))
    return (y_p.reshape(bsz, seq_len, D_MODEL), y_s.reshape(s, 1, D_MODEL),
            token_major(kvc_p), token_major(kvs_p), win_p, s5_p.astype(x_prompt.dtype),
            kv5(kvc_s, s, 1), kv5(kvs_s, s, 1), kv5(win_s, s, state_kv_win.shape[1]), s5_s.astype(state_s5.dtype))
```

```python
import functools
import math

import numpy as np
import jax
import jax.numpy as jnp
from jax import lax
from jax.experimental import pallas as pl
from jax.experimental.pallas import tpu as pltpu

F32 = jnp.float32
BF16 = jnp.bfloat16

D_MODEL = 1024
HEAD_DIM = 64
N_HEADS = 16
N_KV = 4
HPG = 4
CMP_STRIDE = 16
CMP_BLOCK = 32
SEL_BLOCK = 64
N_SELECT = 16
WINDOW = 512
Q_BLOCK = 128
S5_WIDTH = 512
S5_CH = 16
S5_GROUPS = 32
S5_STATE = 64
D_FF = 4096
D_Q = 1024
D_KV = 256
PAGE_SIZE = 128
EPS = 1e-6
NEG = -1e30
FORCE = 1e4
LANES = 128
VMEM_LIMIT = 56 * 1024 * 1024
SWEEP = 512
WIN_KEYS = WINDOW + Q_BLOCK
Q_SCALE_LOG2 = HEAD_DIM ** -0.5 * math.log2(math.e)


def _cparams(sem):
    return pltpu.CompilerParams(dimension_semantics=sem, vmem_limit_bytes=VMEM_LIMIT)


def _gelu(x):
    return 0.5 * x * (1.0 + jnp.tanh(math.sqrt(2.0 / math.pi) * (x + 0.044715 * (x * x * x))))


def _sigmoid(x):
    return 1.0 / (1.0 + jnp.exp(-x))


def _rms(x, g):
    ms = jnp.mean(x * x, axis=-1, keepdims=True)
    return (x * lax.rsqrt(ms + EPS)) * g


def _split3(x):
    hi = x.astype(BF16)
    r1 = x - hi.astype(F32)
    mid = r1.astype(BF16)
    lo = (r1 - mid.astype(F32)).astype(BF16)
    return hi, mid, lo


WT_Q, WT_GN, WT_KV = 0, D_Q, D_Q + 64
WT_ROWS = WT_KV + 6 * D_KV


def _inproj_prompt_kernel(seq_len, x_ref, g_ref, wr_ref, wt_ref,
                          u_ref, gm_ref, kas_ref, kaw_ref, slab_ref,
                          qt_ref, vts_ref, vtw_ref, gnt_ref, kvct_ref, kvst_ref, kvwt_ref):
    tm = x_ref.shape[0]
    hb = _rms(x_ref[...], g_ref[...]).astype(BF16)

    def rowdot(lo, hi):
        return jnp.dot(hb, wr_ref[:, lo:hi], preferred_element_type=F32)

    zc = rowdot(0, 512)
    for combo in range(4):
        slab_ref[combo] = zc[:, combo * LANES:(combo + 1) * LANES]
    zu = rowdot(512, 1024)
    for m in range(4):
        u_ref[m] = zu[:, m * LANES:(m + 1) * LANES]
    gm_ref[...] = rowdot(1024, 3072)
    row = pl.program_id(0) * tm + lax.broadcasted_iota(jnp.int32, (tm, LANES), 0)
    blk = lax.rem(row, seq_len) // SEL_BLOCK
    lane = lax.broadcasted_iota(jnp.int32, (tm, LANES), 1)
    onehot = jnp.where(lane - HEAD_DIM == blk, 1.0, 0.0)
    zs = rowdot(3072, 3584)
    zw = rowdot(3584, 4096)
    for g in range(N_KV):
        kas_ref[g] = (zs[:, g * LANES:(g + 1) * LANES] + onehot).astype(BF16)
        kaw_ref[g] = zw[:, g * LANES:(g + 1) * LANES].astype(BF16)
    zt = lax.dot_general(wt_ref[...], hb, (((1,), (1,)), ((), ())), preferred_element_type=F32)
    kv0 = WT_KV
    kvct_ref[0] = zt[kv0:kv0 + 512]
    kvst_ref[0] = zt[kv0 + 512:kv0 + 1024]
    kvwt_ref[0] = zt[kv0 + 1024:kv0 + 1536]
    for c in range(tm // LANES):
        sl = slice(c * LANES, (c + 1) * LANES)
        qt_ref[c] = (zt[WT_Q:WT_Q + D_Q, sl] * Q_SCALE_LOG2).astype(BF16)
        gnt_ref[c] = zt[WT_GN:WT_GN + 64, sl]
        vts_ref[c] = zt[kv0 + 768:kv0 + 1024, sl].astype(BF16)
        vtw_ref[c] = zt[kv0 + 1280:kv0 + 1536, sl].astype(BF16)


def _inproj_weights(w_in):
    wq, wkv, wgn, wu, wgm = (w_in[:, :1024], w_in[:, 1024:2560], w_in[:, 2560:2608],
                             w_in[:, 2608:3120], w_in[:, 3120:])
    wkv6 = wkv.reshape(D_MODEL, 3, 2, N_KV, HEAD_DIM)
    zpad = jnp.zeros((D_MODEL, N_KV, HEAD_DIM), F32)
    kaug_s = jnp.concatenate([wkv6[:, 1, 0], zpad], axis=-1).reshape(D_MODEL, N_KV * LANES)
    kaug_w = jnp.concatenate([wkv6[:, 2, 0], zpad], axis=-1).reshape(D_MODEL, N_KV * LANES)
    w_row = jnp.concatenate([wkv[:, 0:512], wu, wgm, kaug_s, kaug_w], axis=1).astype(BF16)
    gn_rows = jnp.pad(wgn.T.reshape(N_KV, HPG * 3, D_MODEL), ((0, 0), (0, 4), (0, 0))).reshape(64, D_MODEL)
    w_t = jnp.concatenate([wq.T, gn_rows, wkv.T], axis=0).astype(BF16)
    return w_row, w_t


def _inproj_prompt(x2, g_pre, w_row, w_t, bsz, seq_len, tm=256):
    n = x2.shape[0]
    nc = n // LANES
    cpt = tm // LANES
    per = seq_len // tm
    row = lambda w: pl.BlockSpec((tm, w), lambda i: (i, 0))
    fmaj = pl.BlockSpec((1, 512, tm), lambda i: (i // per, 0, i % per))
    out_shape = (
        jax.ShapeDtypeStruct((4, n, LANES), F32), jax.ShapeDtypeStruct((n, 2048), F32),
        jax.ShapeDtypeStruct((N_KV, n, LANES), BF16), jax.ShapeDtypeStruct((N_KV, n, LANES), BF16),
        jax.ShapeDtypeStruct((4, n, LANES), F32),
        jax.ShapeDtypeStruct((nc, 1024, LANES), BF16), jax.ShapeDtypeStruct((nc, 256, LANES), BF16),
        jax.ShapeDtypeStruct((nc, 256, LANES), BF16), jax.ShapeDtypeStruct((nc, 64, LANES), F32),
        jax.ShapeDtypeStruct((bsz, 512, seq_len), F32), jax.ShapeDtypeStruct((bsz, 512, seq_len), F32),
        jax.ShapeDtypeStruct((bsz, 512, seq_len), F32),
    )
    out_specs = (
        pl.BlockSpec((4, tm, LANES), lambda i: (0, i, 0)), row(2048),
        pl.BlockSpec((N_KV, tm, LANES), lambda i: (0, i, 0)), pl.BlockSpec((N_KV, tm, LANES), lambda i: (0, i, 0)),
        pl.BlockSpec((4, tm, LANES), lambda i: (0, i, 0)),
        pl.BlockSpec((cpt, 1024, LANES), lambda i: (i, 0, 0)), pl.BlockSpec((cpt, 256, LANES), lambda i: (i, 0, 0)),
        pl.BlockSpec((cpt, 256, LANES), lambda i: (i, 0, 0)), pl.BlockSpec((cpt, 64, LANES), lambda i: (i, 0, 0)),
        fmaj, fmaj, fmaj,
    )
    return pl.pallas_call(
        functools.partial(_inproj_prompt_kernel, seq_len),
        grid=(n // tm,),
        in_specs=[row(D_MODEL), pl.BlockSpec((1, D_MODEL), lambda i: (0, 0)),
                  pl.BlockSpec(w_row.shape, lambda i: (0, 0)), pl.BlockSpec(w_t.shape, lambda i: (0, 0))],
        out_specs=out_specs, out_shape=out_shape,
        compiler_params=_cparams(("parallel",)), name="inproj_prompt",
    )(x2, g_pre.reshape(1, D_MODEL), w_row, w_t)


def _cmp_weights(w1_k, w1_v):
    eye2 = jnp.eye(2, dtype=F32)
    out = []
    for w1 in (w1_k, w1_v):
        w = w1.reshape(2, CMP_STRIDE, HEAD_DIM, HEAD_DIM)
        big = jnp.einsum('fsdh,ij->sidfjh', w, eye2).reshape(CMP_STRIDE * 2 * HEAD_DIM, 2 * 2 * HEAD_DIM)
        out += [big, big]
    return jnp.stack(out).astype(BF16)


def _cmp_y_from_slabs(slab_ref, w_ref, y_ref):
    nrows = y_ref.shape[1]
    for combo in range(4):
        xg = jnp.concatenate([slab_ref[combo, pl.ds(s, nrows, stride=CMP_STRIDE), :] for s in range(CMP_STRIDE)],
                             axis=1)
        y_ref[0, :, combo * 256:(combo + 1) * 256] = jnp.dot(xg.astype(BF16), w_ref[combo],
                                                             preferred_element_type=F32)


def _cmp_y_prompt(slabs, bsz, seq_len, wcmp):
    nch = seq_len // CMP_STRIDE
    rb = min(nch, 128)
    per = nch // rb
    return pl.pallas_call(
        _cmp_y_from_slabs,
        grid=(bsz, per),
        in_specs=[pl.BlockSpec((4, rb * CMP_STRIDE, LANES), lambda b, j: (0, b * per + j, 0)),
                  pl.BlockSpec(wcmp.shape, lambda b, j: (0, 0, 0))],
        out_specs=pl.BlockSpec((1, rb, 1024), lambda b, j: (b, j, 0)),
        out_shape=jax.ShapeDtypeStruct((bsz, nch, 1024), F32),
        compiler_params=_cparams(("parallel", "parallel")), name="cmp_y_prompt",
    )(slabs, wcmp)


def _cmp_combine_kernel(y_ref, pe_ref, w1f_ref, w2k_ref, w2vt_ref, kc_ref, vct_ref):
    r = y_ref.shape[1]
    pos = jnp.dot(pe_ref[...], w1f_ref[...], preferred_element_type=F32,
                  precision=lax.Precision.HIGHEST)
    for combo in range(4):
        kv, gp = combo // 2, combo % 2
        first = y_ref[0, :, combo * 256: combo * 256 + LANES]
        second = pltpu.roll(y_ref[0, :, combo * 256 + LANES: combo * 256 + 2 * LANES], r - 1, 0)
        p1 = pos[0:1, kv * HEAD_DIM:(kv + 1) * HEAD_DIM]
        pre = first + second + jnp.concatenate([p1, p1], axis=1)
        act = _gelu(pre).astype(BF16)
        if kv == 0:
            kc = jnp.dot(act, w2k_ref[...], preferred_element_type=F32)
            kc_ref[0, 2 * gp] = kc[:, 0:LANES].astype(BF16)
            kc_ref[0, 2 * gp + 1] = kc[:, LANES:2 * LANES].astype(BF16)
        else:
            vct = lax.dot_general(w2vt_ref[...], act, (((1,), (1,)), ((), ())), preferred_element_type=F32)
            vct_ref[0, 2 * gp] = vct[0:HEAD_DIM].astype(BF16)
            vct_ref[0, 2 * gp + 1] = vct[HEAD_DIM:2 * HEAD_DIM].astype(BF16)


def _cmp_combine(y, pe_k, w1_k, w2_k, pe_v, w1_v, w2_v):
    s, r, _ = y.shape
    pe = jnp.concatenate([pe_k.reshape(1, -1), pe_v.reshape(1, -1)], axis=1)
    pe8 = jnp.pad(pe, ((0, 7), (0, 0)))
    z = jnp.zeros((CMP_BLOCK * HEAD_DIM, HEAD_DIM), F32)
    w1f = jnp.concatenate([jnp.concatenate([w1_k.reshape(-1, HEAD_DIM), z], axis=1),
                           jnp.concatenate([z, w1_v.reshape(-1, HEAD_DIM)], axis=1)], axis=0)
    z64 = jnp.zeros((HEAD_DIM, HEAD_DIM), F32)
    w2k = jnp.concatenate([jnp.concatenate([w2_k, z64, z64, z64], axis=1),
                           jnp.concatenate([z64, z64, w2_k, z64], axis=1)], axis=0).astype(BF16)
    w2vt = jnp.concatenate([jnp.concatenate([w2_v.T, z64], axis=1),
                            jnp.concatenate([z64, w2_v.T], axis=1)], axis=0).astype(BF16)
    full = lambda a: pl.BlockSpec(a.shape, lambda i: (0,) * a.ndim)
    return pl.pallas_call(
        _cmp_combine_kernel,
        grid=(s,),
        in_specs=[pl.BlockSpec((1, r, 1024), lambda i: (i, 0, 0)), full(pe8), full(w1f), full(w2k), full(w2vt)],
        out_specs=(pl.BlockSpec((1, N_KV, r, LANES), lambda i: (i, 0, 0, 0)),
                   pl.BlockSpec((1, N_KV, HEAD_DIM, r), lambda i: (i, 0, 0, 0))),
        out_shape=(jax.ShapeDtypeStruct((s, N_KV, r, LANES), BF16),
                   jax.ShapeDtypeStruct((s, N_KV, HEAD_DIM, r), BF16)),
        compiler_params=_cparams(("parallel",)), name="cmp_combine",
    )(y, pe8, w1f, w2k, w2vt)


def _sel_matrix(nsel, nc, ncp):
    j = np.arange(nsel)
    lo = np.clip((j * SEL_BLOCK - CMP_BLOCK) // CMP_STRIDE + 1, 0, nc)
    hi = np.clip((j * SEL_BLOCK + SEL_BLOCK - 1) // CMP_STRIDE + 1, 0, nc)
    n = np.arange(ncp)
    return ((n[None, :] >= lo[:, None]) & (n[None, :] < hi[:, None])).astype(np.float32)


def _rank_select(score, k_sel):
    nj, nl = score.shape
    sub = 8
    tiles = [score[v * sub:(v + 1) * sub] for v in range(nj // sub)]
    cnts = [jnp.zeros((sub, nl), F32) for _ in tiles]
    jloc = lax.broadcasted_iota(jnp.int32, (sub, nl), 0)
    for i in range(nj):
        bi = jnp.broadcast_to(score[i:i + 1, :], (sub, nl))
        for v, t in enumerate(tiles):
            if v * sub > i:
                inc = jnp.where(bi >= t, 1.0, 0.0)
            elif v * sub + sub - 1 < i:
                inc = jnp.where(bi > t, 1.0, 0.0)
            else:
                inc = jnp.where(jloc > i - v * sub, jnp.where(bi >= t, 1.0, 0.0), jnp.where(bi > t, 1.0, 0.0))
            cnts[v] = cnts[v] + inc
    cnt = jnp.concatenate(cnts, axis=0)
    return jnp.where(cnt < k_sel, 1.0, 0.0)


def _nsa_prompt_kernel(nc_valid, k_sel, qt_ref, gnt_ref, kc_ref, vct_ref, kas_ref, vts_ref, kaw_ref, vtw_ref,
                       a_ref, o_ref, sa_ref, sb_ref):
    qb = pl.program_id(2)
    heads = lambda t: jnp.concatenate([t] * HPG, axis=1)

    def pv_and_sum(vt, p):
        va = jnp.concatenate([vt, jnp.ones((16, vt.shape[1]), BF16)], axis=0)
        out = jnp.dot(va, p.astype(BF16), preferred_element_type=F32)
        return out[0:HEAD_DIM], out[HEAD_DIM:HEAD_DIM + 1]
    qt = qt_ref[0]
    ql = lax.broadcasted_iota(jnp.int32, (1, Q_BLOCK), 1)
    tl2 = lax.broadcasted_iota(jnp.int32, (Q_BLOCK, Q_BLOCK), 0)
    ql2 = lax.broadcasted_iota(jnp.int32, (Q_BLOCK, Q_BLOCK), 1)
    b_diag = jnp.where(tl2 <= ql2, 0.0, NEG)
    b_first = jnp.where(tl2 > ql2, 0.0, NEG)
    zero_q = jnp.zeros((HEAD_DIM, Q_BLOCK), BF16)
    rhs_q = jnp.concatenate(
        [jnp.concatenate([qt[h * HEAD_DIM:(h + 1) * HEAD_DIM], zero_q], axis=0) for h in range(HPG)], axis=1)

    ncp = kc_ref.shape[2]
    assert nc_valid >= ncp - 1
    sc = jnp.dot(kc_ref[0, 0], rhs_q, preferred_element_type=F32)
    edge = lax.shift_right_arithmetic(ql - (CMP_BLOCK - 1), 4)
    nrel = lax.broadcasted_iota(jnp.int32, (ncp, Q_BLOCK), 0) - qb * (Q_BLOCK // CMP_STRIDE)
    sc = sc + heads(jnp.where(nrel <= edge, 0.0, NEG))
    e = jnp.exp2(sc - jnp.max(sc, axis=0, keepdims=True))
    den = jnp.sum(e, axis=0, keepdims=True)
    any_visible = heads(jnp.where(qb * Q_BLOCK + ql >= CMP_BLOCK - 1, 1.0, 0.0))
    p = e * (any_visible / den)
    o_c = jnp.dot(vct_ref[0, 0], p.astype(BF16), preferred_element_type=F32)
    imp = p[:, 0:Q_BLOCK]
    for h in range(1, HPG):
        imp = imp + p[:, h * Q_BLOCK:(h + 1) * Q_BLOCK]
    a = a_ref[...]
    imp_sel = sum(jnp.dot(a, part, preferred_element_type=F32) for part in _split3(imp))

    c0 = jnp.maximum(qb - WINDOW // Q_BLOCK, 0)
    wstart = pl.multiple_of(c0 * Q_BLOCK, Q_BLOCK)
    sw = jnp.dot(kaw_ref[0, pl.ds(wstart, WIN_KEYS), :], rhs_q, preferred_element_type=F32)
    wbias = []
    for i in range(WIN_KEYS // Q_BLOCK):
        d = qb - c0 - i
        wbias.append(jnp.where(d == WINDOW // Q_BLOCK, b_first,
                               jnp.where(d == 0, b_diag, jnp.where(d < 0, NEG, 0.0))))
    sw = sw + heads(jnp.concatenate(wbias, axis=0))
    vtw = jnp.concatenate([vtw_ref[c0 + i] for i in range(WIN_KEYS // Q_BLOCK)], axis=1)
    o_w, lw = pv_and_sum(vtw, jnp.exp2(sw - jnp.max(sw, axis=0, keepdims=True)))
    o_w = o_w / lw

    nsel = imp_sel.shape[0]
    jrow = lax.broadcasted_iota(jnp.int32, (nsel, Q_BLOCK), 0)
    qp1 = qb * Q_BLOCK + lax.broadcasted_iota(jnp.int32, (nsel, Q_BLOCK), 1)
    cur = qp1 // SEL_BLOCK
    forced = (jrow == 0) | (jrow == cur) | (jrow == cur - 1)
    score = jnp.where(jrow <= cur, jnp.where(forced, FORCE, imp_sel), -1.0)
    sel = jnp.where(jrow <= cur, _rank_select(score, k_sel), 0.0)
    if nsel < HEAD_DIM:
        sel = jnp.concatenate([sel, jnp.zeros((HEAD_DIM - nsel, Q_BLOCK), F32)], axis=0)
    mq = (sel - 1.0) * 1e30
    jrow64 = lax.broadcasted_iota(jnp.int32, (HEAD_DIM, Q_BLOCK), 0)
    mq_past = jnp.where(jrow64 >= qb * (Q_BLOCK // SEL_BLOCK), NEG, mq)

    def with_mask(mrows):
        mb = mrows.astype(BF16)
        return jnp.concatenate(
            [jnp.concatenate([qt[h * HEAD_DIM:(h + 1) * HEAD_DIM], mb], axis=0) for h in range(HPG)], axis=1)

    rhs_diag, rhs_past = with_mask(mq), with_mask(mq_past)

    dstart = pl.multiple_of(qb * Q_BLOCK, Q_BLOCK)
    sd = jnp.dot(kas_ref[0, pl.ds(dstart, Q_BLOCK), :], rhs_diag, preferred_element_type=F32) + heads(b_diag)
    m = jnp.max(sd, axis=0, keepdims=True)
    acc, l = pv_and_sum(vts_ref[qb], jnp.exp2(sd - m))

    cps = SWEEP // Q_BLOCK
    n_span = (qb + cps - 1) // cps

    def span_scores(i, buf):
        start = pl.multiple_of(i * SWEEP, SWEEP)
        s = jnp.dot(kas_ref[0, pl.ds(start, SWEEP), :], rhs_past, preferred_element_type=F32)
        buf[...] = s
        return jnp.max(s, axis=0, keepdims=True)

    def span_consume(i, buf, smax, m, l, acc):
        m_new = jnp.maximum(m, smax)
        alpha = jnp.exp2(m - m_new)
        vt = jnp.concatenate([vts_ref[cps * i + k] for k in range(cps)], axis=1)
        pv, psum = pv_and_sum(vt, jnp.exp2(buf[...] - m_new))
        return m_new, l * alpha + psum, acc * alpha + pv

    n_pair = (n_span + 1) // 2

    def pair(k, carry, prefetch):
        m, l, acc, smax0 = carry
        smax1 = span_scores(2 * k + 1, sb_ref)
        m, l, acc = span_consume(2 * k, sa_ref, smax0, m, l, acc)
        smax0 = span_scores(2 * k + 2, sa_ref) if prefetch else smax0
        m, l, acc = span_consume(2 * k + 1, sb_ref, smax1, m, l, acc)
        return m, l, acc, smax0

    carry = lax.fori_loop(0, n_pair - 1, lambda k, c: pair(k, c, True), (m, l, acc, span_scores(0, sa_ref)))
    m, l, acc, _ = pair(jnp.maximum(n_pair - 1, 0), carry, False)
    o_s = acc / l

    gate = _sigmoid(gnt_ref[0])
    outs = []
    for h in range(HPG):
        sl = slice(h * Q_BLOCK, (h + 1) * Q_BLOCK)
        outs.append(gate[3 * h:3 * h + 1] * o_c[:, sl] + gate[3 * h + 1:3 * h + 2] * o_s[:, sl]
                    + gate[3 * h + 2:3 * h + 3] * o_w[:, sl])
    o_t = jnp.concatenate(outs, axis=0)
    o_ref[...] = o_t.T.astype(BF16)


def _nsa_prompt(qt, gnt, kc, vct, kas, vts, kaw, vtw, bsz, seq_len):
    nq = seq_len // Q_BLOCK
    nsel = seq_len // SEL_BLOCK
    nc_valid = seq_len // CMP_STRIDE - 1
    ncp = kc.shape[2]
    k_sel = min(N_SELECT, nsel)
    a = jnp.asarray(_sel_matrix(nsel, nc_valid, ncp), BF16)
    n = bsz * seq_len
    return pl.pallas_call(
        functools.partial(_nsa_prompt_kernel, nc_valid, k_sel),
        grid=(bsz, N_KV, nq),
        in_specs=[
            pl.BlockSpec((1, HPG * HEAD_DIM, LANES), lambda b, g, i: (b * nq + i, g, 0)),
            pl.BlockSpec((1, 16, LANES), lambda b, g, i: (b * nq + i, g, 0)),
            pl.BlockSpec((1, 1, ncp, LANES), lambda b, g, i: (b, g, 0, 0)),
            pl.BlockSpec((1, 1, HEAD_DIM, ncp), lambda b, g, i: (b, g, 0, 0)),
            pl.BlockSpec((1, seq_len, LANES), lambda b, g, i: (g, b, 0)),
            pl.BlockSpec((nq, HEAD_DIM, LANES), lambda b, g, i: (b, g, 0)),
            pl.BlockSpec((1, seq_len, LANES), lambda b, g, i: (g, b, 0)),
            pl.BlockSpec((nq, HEAD_DIM, LANES), lambda b, g, i: (b, g, 0)),
            pl.BlockSpec(a.shape, lambda b, g, i: (0, 0)),
        ],
        out_specs=pl.BlockSpec((Q_BLOCK, HPG * HEAD_DIM), lambda b, g, i: (b * nq + i, g)),
        out_shape=jax.ShapeDtypeStruct((n, D_Q), BF16),
        scratch_shapes=[pltpu.VMEM((SWEEP, HPG * Q_BLOCK), F32), pltpu.VMEM((SWEEP, HPG * Q_BLOCK), F32)],
        compiler_params=_cparams(("parallel", "parallel", "arbitrary")), name="nsa_prompt",
    )(qt, gnt, kc, vct, kas, vts, kaw, vtw, a)


S5_L = 16
S5_W = S5_L * S5_CH
S5_P = 2 * S5_STATE
S5_OCT = LANES // S5_CH
S5_NOCT = S5_GROUPS // S5_OCT
S5_OW = S5_L * LANES
S5_OP = S5_OCT * S5_P


def _s5_prep_kernel(are_ref, aim_ref, ldt_ref, bre_ref, bim_ref, cre_ref, cim_ref, spread_ref,
                    tg_ref, sg_ref, ogt_ref, misc_ref, bs_ref, oct_ref):
    g8 = lax.rem(pl.program_id(0), S5_OCT)
    are, aim = are_ref[0], aim_ref[0]
    dt = jnp.exp(ldt_ref[0])
    mag = jnp.exp(are * dt)
    ar, ai = mag * jnp.cos(aim * dt), mag * jnp.sin(aim * dt)
    den = are * are + aim * aim
    fr = ((ar - 1.0) * are + ai * aim) / den
    fi = (ai * are - (ar - 1.0) * aim) / den
    bre, bim = bre_ref[0], bim_ref[0]
    cre, cim = cre_ref[0], cim_ref[0]
    br, bi = fr * bre - fi * bim, fr * bim + fi * bre
    lo16 = lax.broadcasted_iota(jnp.int32, (S5_CH, S5_P), 1) < S5_STATE
    lo1 = lax.broadcasted_iota(jnp.int32, (1, S5_P), 1) < S5_STATE
    pr, pi = [jnp.ones_like(ar)], [jnp.zeros_like(ar)]
    for _ in range(S5_L):
        pr.append(pr[-1] * ar - pi[-1] * ai)
        pi.append(pr[-2] * ai + pi[-1] * ar)
    cpr = [cre * pr[k] - cim * pi[k] for k in range(S5_L + 1)]
    cpi = [cre * pi[k] + cim * pr[k] for k in range(S5_L + 1)]
    rpack = jnp.concatenate([jnp.where(lo16, cpr[k], cpi[k]) for k in range(S5_L)], axis=0)
    bpack = jnp.where(lo16, br, -bi)
    krow = lax.dot_general(bpack, rpack, (((1,), (1,)), ((), ())), preferred_element_type=F32,
                           precision=lax.Precision.HIGHEST)
    wide = jnp.dot(krow.astype(BF16), spread_ref[...], preferred_element_type=F32)
    wide = pltpu.roll(wide, g8 * S5_CH, 1)
    lane = lax.broadcasted_iota(jnp.int32, (S5_CH, S5_OW), 1)
    zpad = jnp.zeros((S5_CH, S5_OP - S5_P), F32)
    place = lambda blk: pltpu.roll(jnp.concatenate([blk, zpad], axis=1), g8 * S5_P, 1).astype(BF16)
    for j in range(S5_L):
        rows = pl.ds(pl.multiple_of(j * LANES + g8 * S5_CH, S5_CH), S5_CH)
        shifted = wide if j == 0 else pltpu.roll(wide, j * LANES, 1)
        tg_ref[0, rows, :] = jnp.where(lane >= j * LANES, shifted, 0.0).astype(BF16)
        k = S5_L - 1 - j
        sblk = jnp.where(lo16, pr[k] * br - pi[k] * bi, pr[k] * bi + pi[k] * br)
        sg_ref[0, rows, :] = place(sblk)
        if j == S5_L - 1:
            bs_ref[0] = sblk
        ogt_ref[0, rows, :] = place(jnp.where(lo16, cpr[j + 1], -cpi[j + 1]))
    oct_ref[0] = jnp.where(lo16, cre, -cim)
    misc_ref[0] = jnp.concatenate([
        pr[S5_L], jnp.where(lo1, -pi[S5_L], pi[S5_L]), ar, jnp.where(lo1, -ai, ai),
        jnp.zeros((4, S5_P), F32)], axis=0)


def _s5_prep(a_re, a_im, log_dt, b_re, b_im, c_re, c_im):
    g = S5_GROUPS
    dup = lambda a: jnp.concatenate([a, a], axis=-1)
    are, aim = dup(a_re).reshape(g, 1, S5_P), dup(a_im).reshape(g, 1, S5_P)
    ldt = jnp.broadcast_to(log_dt.reshape(g, 1, 1), (g, 1, S5_P))
    bre, bim = dup(jnp.swapaxes(b_re, 1, 2)), dup(jnp.swapaxes(b_im, 1, 2))
    cre, cim = dup(c_re), dup(c_im)
    v1 = pl.BlockSpec((1, 1, S5_P), lambda i: (i, 0, 0))
    v16 = pl.BlockSpec((1, S5_CH, S5_P), lambda i: (i, 0, 0))
    lag_co = np.arange(S5_W)
    spread = np.zeros((S5_W, S5_OW), np.float32)
    spread[lag_co, (lag_co // S5_CH) * LANES + lag_co % S5_CH] = 1.0
    spread = jnp.asarray(spread, BF16)
    octet = lambda w: pl.BlockSpec((1, S5_OW, w), lambda i: (i // S5_OCT, 0, 0))
    return pl.pallas_call(
        _s5_prep_kernel, grid=(g,),
        in_specs=[v1, v1, v1, v16, v16, v16, v16, pl.BlockSpec(spread.shape, lambda i: (0, 0))],
        out_specs=(octet(S5_OW), octet(S5_OP), octet(S5_OP), pl.BlockSpec((1, 8, S5_P), lambda i: (i, 0, 0)),
                   v16, v16),
        out_shape=(jax.ShapeDtypeStruct((S5_NOCT, S5_OW, S5_OW), BF16), jax.ShapeDtypeStruct((S5_NOCT, S5_OW, S5_OP), BF16),
                   jax.ShapeDtypeStruct((S5_NOCT, S5_OW, S5_OP), BF16), jax.ShapeDtypeStruct((g, 8, S5_P), F32),
                   jax.ShapeDtypeStruct((g, S5_CH, S5_P), F32), jax.ShapeDtypeStruct((g, S5_CH, S5_P), F32)),
        compiler_params=_cparams(("arbitrary",)), name="s5_prep",
    )(are, aim, ldt, bre, bim, cre, cim, spread)


def _s5_chunks(u_ref):
    nch = u_ref.shape[1] // S5_L
    return jnp.concatenate([u_ref[0, pl.ds(j, nch, stride=S5_L), :] for j in range(S5_L)], axis=1)


def _s5_sum_kernel(u_ref, sg_ref, s_ref):
    s_ref[0] = jnp.dot(_s5_chunks(u_ref).astype(BF16), sg_ref[0], preferred_element_type=F32)


def _s5_scan_kernel(s_ref, a1_ref, a2_ref, h_ref, last_ref, carry):
    @pl.when(pl.program_id(0) == 0)
    def _():
        carry[...] = jnp.zeros_like(carry)

    a1, a2 = a1_ref[...], a2_ref[...]

    def step(c, h):
        h_ref[c] = h
        return a1 * h + a2 * pltpu.roll(h, S5_STATE, 1) + s_ref[c]

    h = lax.fori_loop(0, s_ref.shape[0], step, carry[...])
    carry[...] = h
    last_ref[...] = h


def _s5_out_kernel(u_ref, h_ref, tg_ref, ogt_ref, d_ref, y_ref):
    u = _s5_chunks(u_ref)
    y = jnp.dot(u.astype(BF16), tg_ref[0], preferred_element_type=F32)
    y = y + lax.dot_general(h_ref[0].astype(BF16), ogt_ref[0], (((1,), (1,)), ((), ())), preferred_element_type=F32)
    y = y + d_ref[0] * u
    nch = y.shape[0]
    for k in range(S5_L):
        y_ref[0, pl.ds(k, nch, stride=S5_L), :] = y[:, k * LANES:(k + 1) * LANES]


def _s5_prompt(uslab, ops, s5_d, bsz, seq_len):
    tg, sg, ogt, misc = ops[0], ops[1], ops[2], ops[3]
    g, nch = S5_GROUPS, seq_len // S5_L
    rows = bsz * nch
    slab = pl.BlockSpec((1, seq_len, LANES), lambda m, b: (m, b, 0))
    per_oct = lambda r, c: pl.BlockSpec((1, r, c), lambda m, b: (m, 0, 0))
    per_row = lambda c: pl.BlockSpec((1, nch, c), lambda m, b: (m, b, 0))
    ssum = pl.pallas_call(
        _s5_sum_kernel, grid=(S5_NOCT, bsz), in_specs=[slab, per_oct(S5_OW, S5_OP)],
        out_specs=per_row(S5_OP), out_shape=jax.ShapeDtypeStruct((S5_NOCT, rows, S5_OP), F32),
        compiler_params=_cparams(("parallel", "parallel")), name="s5_sum",
    )(uslab, sg)
    s_cm = (ssum.reshape(S5_NOCT, bsz, nch, S5_OCT, S5_P).transpose(2, 1, 0, 3, 4).reshape(nch, bsz * g, S5_P))
    a1 = jnp.tile(misc[:, 0, :], (bsz, 1))
    a2 = jnp.tile(misc[:, 1, :], (bsz, 1))
    cb = min(nch, 32)
    hs, last = pl.pallas_call(
        _s5_scan_kernel, grid=(nch // cb,),
        in_specs=[pl.BlockSpec((cb, bsz * g, S5_P), lambda i: (i, 0, 0)),
                  pl.BlockSpec((bsz * g, S5_P), lambda i: (0, 0)), pl.BlockSpec((bsz * g, S5_P), lambda i: (0, 0))],
        out_specs=(pl.BlockSpec((cb, bsz * g, S5_P), lambda i: (i, 0, 0)), pl.BlockSpec((bsz * g, S5_P), lambda i: (0, 0))),
        out_shape=(jax.ShapeDtypeStruct((nch, bsz * g, S5_P), F32), jax.ShapeDtypeStruct((bsz * g, S5_P), F32)),
        scratch_shapes=[pltpu.VMEM((bsz * g, S5_P), F32)],
        compiler_params=_cparams(("arbitrary",)), name="s5_scan",
    )(s_cm, a1, a2)
    h_oct = (hs.reshape(nch, bsz, S5_NOCT, S5_OCT, S5_P).transpose(2, 1, 0, 3, 4).reshape(S5_NOCT, rows, S5_OP))
    dvec = jnp.tile(s5_d.reshape(S5_NOCT, 1, LANES), (1, 1, S5_L))
    yslab = pl.pallas_call(
        _s5_out_kernel, grid=(S5_NOCT, bsz),
        in_specs=[slab, per_row(S5_OP), per_oct(S5_OW, S5_OW), per_oct(S5_OW, S5_OP), per_oct(1, S5_OW)],
        out_specs=slab, out_shape=jax.ShapeDtypeStruct((S5_NOCT, bsz * seq_len, LANES), F32),
        compiler_params=_cparams(("parallel", "parallel")), name="s5_out",
    )(uslab, h_oct, tg, ogt, dvec)
    state = last.reshape(bsz, g, 2, S5_STATE).transpose(0, 1, 3, 2)
    return yslab, state


def _s5_sample_kernel(u_ref, h0_ref, bs_ref, oct_ref, misc_ref, d_ref, y_ref, h1_ref):
    u, h0 = u_ref[0], h0_ref[0]
    hi = lax.Precision.HIGHEST
    bu = jnp.dot(u, bs_ref[0], preferred_element_type=F32, precision=hi)
    h1 = misc_ref[0, 2:3] * h0 + misc_ref[0, 3:4] * pltpu.roll(h0, S5_STATE, 1) + bu
    h1_ref[0] = h1
    y = lax.dot_general(h1, oct_ref[0], (((1,), (1,)), ((), ())), preferred_element_type=F32, precision=hi)
    y_ref[0] = y + d_ref[0] * u


def _s5_sample(u, state, ops, s5_d):
    misc, bs, oct_ = ops[3], ops[4], ops[5]
    s, g = u.shape[0], S5_GROUPS
    ug = u.reshape(s, g, S5_CH).transpose(1, 0, 2)
    h0 = state.astype(F32).transpose(1, 0, 3, 2).reshape(g, s, S5_P)
    gspec = lambda r, c: pl.BlockSpec((1, r, c), lambda i: (i, 0, 0))
    y, h1 = pl.pallas_call(
        _s5_sample_kernel, grid=(g,),
        in_specs=[gspec(s, S5_CH), gspec(s, S5_P), gspec(S5_CH, S5_P), gspec(S5_CH, S5_P), gspec(8, S5_P), gspec(1, S5_CH)],
        out_specs=(gspec(s, S5_CH), gspec(s, S5_P)),
        out_shape=(jax.ShapeDtypeStruct((g, s, S5_CH), F32), jax.ShapeDtypeStruct((g, s, S5_P), F32)),
        compiler_params=_cparams(("parallel",)), name="s5_sample",
    )(ug, h0, bs, oct_, misc, s5_d.reshape(g, 1, S5_CH))
    return (y.transpose(1, 0, 2).reshape(s, S5_WIDTH),
            h1.reshape(g, s, 2, S5_STATE).transpose(1, 0, 3, 2))


def _merge_mlp_kernel(x_ref, o_ref, ys_ref, gm_ref, wglu_ref, bglu_ref, wbn_ref, wbs_ref, wout_ref, wup_ref, wdn_ref,
                      npost_ref, nmpre_ref, nmpost_ref, out_ref):
    dot = lambda a, w_ref: jnp.dot(a.astype(BF16), w_ref[...], preferred_element_type=F32)
    z = _gelu(jnp.concatenate([ys_ref[m] for m in range(S5_NOCT)], axis=1))
    o_s5 = z * _sigmoid(dot(z, wglu_ref) + bglu_ref[...])
    merged = (_sigmoid(gm_ref[:, 0:D_MODEL]) * dot(o_ref[...], wbn_ref)
              + _sigmoid(gm_ref[:, D_MODEL:2 * D_MODEL]) * dot(o_s5, wbs_ref))
    x1 = x_ref[...] + _rms(dot(merged, wout_ref), npost_ref[...])
    hm = _rms(x1, nmpre_ref[...])
    up = jnp.maximum(dot(hm, wup_ref), 0.0)
    f = dot(up * up, wdn_ref)
    out_ref[...] = x1 + _rms(f, nmpost_ref[...])


def _merge_mlp(x2, o_nsa, y_s5, gm, wts, tm=256):
    n = x2.shape[0]
    tm = min(tm, n)
    row = lambda w: pl.BlockSpec((tm, w), lambda i: (i, 0))
    const = lambda a: pl.BlockSpec(a.shape, lambda i: (0, 0), pipeline_mode=pl.Buffered(1))
    return pl.pallas_call(
        _merge_mlp_kernel, grid=(n // tm,),
        in_specs=[row(D_MODEL), row(D_Q), pl.BlockSpec((S5_NOCT, tm, LANES), lambda i: (0, i, 0)), row(2 * D_MODEL)]
                 + [const(a) for a in wts],
        out_specs=row(D_MODEL), out_shape=jax.ShapeDtypeStruct((n, D_MODEL), F32),
        compiler_params=_cparams(("parallel",)), name="merge_mlp",
    )(x2, o_nsa, y_s5, gm, *wts)


def _merge_weights(s5_w_glu, s5_b_glu, w_branch_nsa, w_branch_s5, w_out, w_mlp_up, w_mlp_down,
                   norm_mix_post, norm_mlp_pre, norm_mlp_post):
    r = lambda v: v.reshape(1, -1).astype(F32)
    b = lambda w: w.astype(BF16)
    return (b(s5_w_glu), r(s5_b_glu), b(w_branch_nsa), b(w_branch_s5), b(w_out), b(w_mlp_up), b(w_mlp_down),
            r(norm_mix_post), r(norm_mlp_pre), r(norm_mlp_post))


def _inproj_sample_kernel(x_ref, g_ref, wr_ref, wt_ref, kv_ref, u_ref, gm_ref, qg_ref):
    hb = _rms(x_ref[...], g_ref[...]).astype(BF16)
    u_ref[...] = jnp.dot(hb, wr_ref[:, 512:1024], preferred_element_type=F32)
    gm_ref[...] = jnp.dot(hb, wr_ref[:, 1024:3072], preferred_element_type=F32)
    z = lax.dot_general(hb, wt_ref[...], (((1,), (1,)), ((), ())), preferred_element_type=F32)
    qg_ref[...] = z[:, 0:WT_KV]
    kv_ref[...] = z[:, WT_KV:WT_ROWS]


def _inproj_sample(x2, g_pre, w_row, w_t):
    s = x2.shape[0]
    full = lambda a: pl.BlockSpec(a.shape, lambda i: (0, 0))
    o = lambda w: pl.BlockSpec((s, w), lambda i: (0, 0))
    return pl.pallas_call(
        _inproj_sample_kernel, grid=(1,),
        in_specs=[o(D_MODEL), pl.BlockSpec((1, D_MODEL), lambda i: (0, 0)), full(w_row), full(w_t)],
        out_specs=(o(1536), o(512), o(2048), o(WT_KV)),
        out_shape=(jax.ShapeDtypeStruct((s, 1536), F32), jax.ShapeDtypeStruct((s, 512), F32),
                   jax.ShapeDtypeStruct((s, 2048), F32), jax.ShapeDtypeStruct((s, WT_KV), F32)),
        compiler_params=_cparams(("arbitrary",)), name="inproj_sample",
    )(x2, g_pre.reshape(1, D_MODEL), w_row, w_t)


CMP_PAGES_PER_STEP = 32


def _cmp_y_sample_kernel(pps, pt_ref, *refs):
    x_refs, w_ref, y_ref, slab = refs[:pps], refs[pps], refs[pps + 1], refs[pps + 2]
    for r, x in enumerate(x_refs):
        for combo in range(4):
            kv, gp = combo // 2, combo % 2
            pair = x[0, kv, 2 * gp:2 * gp + 2].reshape(2 * HEAD_DIM, PAGE_SIZE)
            slab[combo, r * PAGE_SIZE:(r + 1) * PAGE_SIZE, :] = pair.T
    _cmp_y_from_slabs(slab, w_ref, y_ref)


def _cmp_y_sample(cache_t, page_table, wcmp):
    s, n_pages = page_table.shape
    cpp = PAGE_SIZE // CMP_STRIDE
    pps = CMP_PAGES_PER_STEP
    steps = n_pages // pps

    def page_spec(r):
        return pl.BlockSpec((1, 2, N_KV, HEAD_DIM, PAGE_SIZE), lambda b, j, pt: (pt[b, j * pps + r], 0, 0, 0, 0))

    grid_spec = pltpu.PrefetchScalarGridSpec(
        num_scalar_prefetch=1, grid=(s, steps),
        in_specs=[page_spec(r) for r in range(pps)] + [pl.BlockSpec(wcmp.shape, lambda b, j, pt: (0, 0, 0))],
        out_specs=pl.BlockSpec((1, pps * cpp, 1024), lambda b, j, pt: (b, j, 0)),
        scratch_shapes=[pltpu.VMEM((4, pps * PAGE_SIZE, LANES), F32)],
    )
    return pl.pallas_call(
        functools.partial(_cmp_y_sample_kernel, pps), grid_spec=grid_spec,
        out_shape=jax.ShapeDtypeStruct((s, n_pages * cpp, 1024), F32),
        compiler_params=_cparams(("parallel", "parallel")), name="cmp_y_sample",
    )(page_table, *([cache_t] * pps), wcmp)


def _nsa_sample_cmp_kernel(nc_valid, qbd_ref, kc_ref, vct_ref, a_ref, oc_ref, isel_ref):
    ncp = kc_ref.shape[2]
    sc = jnp.dot(kc_ref[0, 0], qbd_ref[0, 0], preferred_element_type=F32)
    for g in range(1, N_KV):
        sc = sc + jnp.dot(kc_ref[0, g], qbd_ref[0, g], preferred_element_type=F32)
    nrow = lax.broadcasted_iota(jnp.int32, (ncp, LANES), 0)
    cmask = nrow < nc_valid
    sc = jnp.where(cmask, sc, NEG)
    e = jnp.where(cmask, jnp.exp(sc - jnp.max(sc, axis=0, keepdims=True)), 0.0)
    p = e / jnp.sum(e, axis=0, keepdims=True)
    pb = p.astype(BF16)
    for g in range(N_KV):
        oc_ref[0, g] = jnp.dot(vct_ref[0, g], pb, preferred_element_type=F32)
    a = a_ref[...]
    r = sum(jnp.dot(a, part, preferred_element_type=F32) for part in _split3(p))
    tot = r
    for h in range(1, HPG):
        tot = tot + pltpu.roll(r, LANES - h * N_KV, 1)
    isel_ref[0] = tot


def _topk_sample_kernel(nsel, cur, k_sel, isel_ref, tri_ref, idx_ref):
    jp = isel_ref.shape[0]
    jrow = lax.broadcasted_iota(jnp.int32, (jp, LANES), 0)
    forced = (jrow == 0) | (jrow == cur) | (jrow == cur - 1)
    score = jnp.where(jrow <= cur, jnp.where(forced, FORCE, isel_ref[...]), -1.0)
    score = jnp.where(jrow < nsel, score, -2.0)
    sel = _rank_select(score, k_sel)
    rank = jnp.dot(tri_ref[...], sel.astype(BF16), preferred_element_type=F32)
    jf = jrow.astype(F32)
    rows = [jnp.sum(jnp.where((sel > 0.5) & (rank == float(r + 1)), jf, 0.0), axis=0, keepdims=True)
            for r in range(k_sel)]
    idx_ref[...] = jnp.concatenate(rows, axis=0).astype(jnp.int32)


def _nsa_sample_attn_kernel(n_cache, tbl_ref, q_ref, *refs):
    blocks = refs[:N_KV * n_cache]
    kvs_ref, win_ref, kvw_ref, kvwc_ref, oc_ref, gate_ref, o_ref, wout_ref = refs[N_KV * n_cache:]
    b = pl.program_id(0)
    nt = (((1,), (1,)), ((), ()))
    lane = lax.broadcasted_iota(jnp.int32, (8, PAGE_SIZE), 1)
    kvw_new = kvw_ref[0]
    kvs_new = kvs_ref[0]
    nwin = win_ref.shape[4]
    wlane = lax.broadcasted_iota(jnp.int32, (HEAD_DIM, nwin), 1)
    for c in range(2 * N_KV):
        shifted = pltpu.roll(win_ref[0, c // N_KV, c % N_KV], nwin - 1, 1)
        wout_ref[0, c // N_KV, c % N_KV] = jnp.where(wlane == nwin - 1, kvwc_ref[0, c], shifted)
    for g in range(N_KV):
        col = g * HEAD_DIM
        q = q_ref[0, g]
        qf = q.astype(F32)
        pages = blocks[g * n_cache:(g + 1) * n_cache]
        kt = jnp.concatenate([pg[0, 0, 0].astype(BF16) for pg in pages], axis=1)
        vt = jnp.concatenate([pg[0, 1, 0].astype(BF16) for pg in pages], axis=1)
        bias = []
        for r in range(n_cache):
            half = lax.rem(tbl_ref[(b * N_KV + g) * n_cache + r], 2)
            bias.append(jnp.where((lane >= half * SEL_BLOCK) & (lane < (half + 1) * SEL_BLOCK), 0.0, NEG))
        s = jnp.dot(q, kt, preferred_element_type=F32) + jnp.concatenate(bias, axis=1)
        kn = kvs_new[:, col:col + HEAD_DIM].astype(BF16).astype(F32)
        vn = kvs_new[:, 256 + col:256 + col + HEAD_DIM].astype(BF16).astype(F32)
        s_new = jnp.sum(qf * kn, axis=1, keepdims=True)
        m_s = jnp.maximum(jnp.max(s, axis=1, keepdims=True), s_new)
        p = jnp.exp(s - m_s)
        p_new = jnp.exp(s_new - m_s)
        l_s = jnp.sum(p, axis=1, keepdims=True) + p_new
        o_s = (lax.dot_general(p.astype(BF16), vt, nt, preferred_element_type=F32)
               + p_new.astype(BF16).astype(F32) * vn) / l_s
        kw = win_ref[0, 0, g].astype(BF16)
        vw = win_ref[0, 1, g].astype(BF16)
        sw = jnp.dot(q, kw, preferred_element_type=F32)
        keep = lax.broadcasted_iota(jnp.int32, sw.shape, 1) >= 1
        sw = jnp.where(keep, sw, NEG)
        kwn = kvw_new[:, col:col + HEAD_DIM].astype(BF16).astype(F32)
        vwn = kvw_new[:, 256 + col:256 + col + HEAD_DIM].astype(BF16).astype(F32)
        sw_new = jnp.sum(qf * kwn, axis=1, keepdims=True)
        mw = jnp.maximum(jnp.max(sw, axis=1, keepdims=True), sw_new)
        pw = jnp.where(keep, jnp.exp(sw - mw), 0.0)
        pw_new = jnp.exp(sw_new - mw)
        lw = jnp.sum(pw, axis=1, keepdims=True) + pw_new
        o_w = (lax.dot_general(pw.astype(BF16), vw, nt, preferred_element_type=F32)
               + pw_new.astype(BF16).astype(F32) * vwn) / lw
        gate = _sigmoid(gate_ref[0, g])
        o_ref[0, g] = gate[0] * oc_ref[0, g] + gate[1] * o_s + gate[2] * o_w


def sample_attention(x_sample, cmp_t, slc_t, win_t, page_table, norm_mix_pre, w_row, w_t,
                     wcmp, cmp_pe_k, cmp_w1_k, cmp_w2_k, cmp_pe_v, cmp_w1_v, cmp_w2_v):
    s = x_sample.shape[0]
    n_pages = page_table.shape[1]
    past = n_pages * PAGE_SIZE
    assert past % SEL_BLOCK == 0 and x_sample.shape[1] == 1
    kv, u, gm, qg = _inproj_sample(x_sample.reshape(s, D_MODEL), norm_mix_pre, w_row, w_t)
    q = qg[:, 0:D_Q].reshape(s, N_KV, HPG, HEAD_DIM) * (HEAD_DIM ** -0.5)
    gn = qg[:, D_Q:].reshape(s, N_KV, 16)[:, :, 0:12].reshape(s, N_KV, HPG, 3)
    kvc_new, kvs_new, kvw_new = kv[:, 0:512], kv[:, 512:1024], kv[:, 1024:1536]

    y = _cmp_y_sample(cmp_t, page_table, wcmp)
    kc, vct = _cmp_combine(y, cmp_pe_k, cmp_w1_k, cmp_w2_k, cmp_pe_v, cmp_w1_v, cmp_w2_v)
    ncp = kc.shape[2]
    nc_valid = (past + 1) // CMP_STRIDE - 1
    nsel = -(-(past + 1) // SEL_BLOCK)
    jp = -(-nsel // 8) * 8
    cur = past // SEL_BLOCK
    k_sel = min(N_SELECT, nsel)
    assert ncp >= nc_valid and (nc_valid - 1) * CMP_STRIDE + CMP_BLOCK - 1 <= past

    qb16 = q.astype(BF16)
    qbd = jnp.zeros((s, N_KV, LANES, LANES), BF16)
    gidx = jnp.arange(N_KV)
    lane_gh = gidx[:, None] + jnp.arange(HPG)[None, :] * N_KV
    qbd = qbd.at[:, gidx[:, None, None], jnp.arange(HEAD_DIM)[None, None, :], lane_gh[:, :, None]].set(qb16)
    pad_rows = lambda a: jnp.pad(a, [(0, 0)] * (a.ndim - 2) + [(0, 8 - HPG), (0, 0)])
    qrow = pad_rows(qb16)

    a = jnp.asarray(np.pad(_sel_matrix(nsel, nc_valid, ncp), ((0, jp - nsel), (0, 0))), BF16)
    oc_t, isel = pl.pallas_call(
        functools.partial(_nsa_sample_cmp_kernel, nc_valid), grid=(s,),
        in_specs=[pl.BlockSpec((1, N_KV, LANES, LANES), lambda b: (b, 0, 0, 0)),
                  pl.BlockSpec((1, N_KV, ncp, LANES), lambda b: (b, 0, 0, 0)),
                  pl.BlockSpec((1, N_KV, HEAD_DIM, ncp), lambda b: (b, 0, 0, 0)),
                  pl.BlockSpec(a.shape, lambda b: (0, 0))],
        out_specs=(pl.BlockSpec((1, N_KV, HEAD_DIM, LANES), lambda b: (b, 0, 0, 0)),
                   pl.BlockSpec((1, jp, LANES), lambda b: (b, 0, 0))),
        out_shape=(jax.ShapeDtypeStruct((s, N_KV, HEAD_DIM, LANES), F32), jax.ShapeDtypeStruct((s, jp, LANES), F32)),
        compiler_params=_cparams(("parallel",)), name="nsa_sample_cmp",
    )(qbd, kc, vct, a)

    assert s * N_KV == LANES
    isel_t = isel[:, :, 0:N_KV].transpose(1, 0, 2).reshape(jp, s * N_KV)
    tri = jnp.asarray(np.tril(np.ones((jp, jp), np.float32)), BF16)
    idx = pl.pallas_call(
        functools.partial(_topk_sample_kernel, nsel, cur, k_sel), grid=(1,),
        in_specs=[pl.BlockSpec((jp, LANES), lambda i: (0, 0)), pl.BlockSpec((jp, jp), lambda i: (0, 0))],
        out_specs=pl.BlockSpec((k_sel, LANES), lambda i: (0, 0)),
        out_shape=jax.ShapeDtypeStruct((k_sel, LANES), jnp.int32),
        compiler_params=_cparams(("arbitrary",)), name="topk_sample",
    )(isel_t, tri)
    n_cache = k_sel - 1
    blk = idx[0:n_cache].T.reshape(s, N_KV, n_cache)
    page = jnp.take_along_axis(page_table, (blk // 2).reshape(s, -1), axis=1).reshape(s, N_KV, n_cache)
    tbl = (page * 2 + blk % 2).astype(jnp.int32).reshape(s * N_KV * n_cache)

    oc_g = oc_t[:, gidx[:, None, None], jnp.arange(HEAD_DIM)[None, None, :], lane_gh[:, :, None]]
    oc_row = pad_rows(oc_g)
    gate_in = pad_rows(jnp.broadcast_to(gn.transpose(0, 1, 3, 2)[..., None], (s, N_KV, 3, HPG, HEAD_DIM)))
    wrows = win_t.shape[4]

    def blk_spec(g, r):
        return pl.BlockSpec((1, 2, 1, HEAD_DIM, PAGE_SIZE),
                            lambda b, t: (t[(b * N_KV + g) * n_cache + r] // 2, 0, g, 0, 0))

    per_b = lambda shape: pl.BlockSpec((1,) + shape, lambda b, t: (b,) + (0,) * len(shape))
    win_shape = (2, N_KV, HEAD_DIM, wrows)
    n_blk = N_KV * n_cache
    grid_spec = pltpu.PrefetchScalarGridSpec(
        num_scalar_prefetch=1, grid=(s,),
        in_specs=[per_b((N_KV, 8, HEAD_DIM))] + [blk_spec(g, r) for g in range(N_KV) for r in range(n_cache)]
                 + [per_b((1, 512)), per_b(win_shape), per_b((1, 512)), per_b((2 * N_KV, HEAD_DIM, 1)),
                    per_b((N_KV, 8, HEAD_DIM)), per_b((N_KV, 3, 8, HEAD_DIM))],
        out_specs=(per_b((N_KV, 8, HEAD_DIM)), per_b(win_shape)),
    )
    o_row, win_out = pl.pallas_call(
        functools.partial(_nsa_sample_attn_kernel, n_cache), grid_spec=grid_spec,
        out_shape=(jax.ShapeDtypeStruct((s, N_KV, 8, HEAD_DIM), F32), jax.ShapeDtypeStruct((s,) + win_shape, F32)),
        compiler_params=_cparams(("parallel",)), name="nsa_sample_attn",
    )(tbl, qrow, *([slc_t] * n_blk), kvs_new.reshape(s, 1, 512), win_t,
      kvw_new.reshape(s, 1, 512), kvw_new.reshape(s, 2 * N_KV, HEAD_DIM, 1), oc_row, gate_in)
    o = o_row[:, :, 0:HPG, :]
    return o.reshape(s, D_Q).astype(BF16), kvc_new, kvs_new, win_out, u, gm


def prompt_attention(x_prompt, norm_mix_pre, w_row, w_t, wcmp, cmp_pe_k, cmp_w1_k, cmp_w2_k, cmp_pe_v, cmp_w1_v, cmp_w2_v):
    bsz, seq_len, _ = x_prompt.shape
    assert seq_len % (2 * SWEEP) == 0 and WIN_KEYS <= seq_len <= SEL_BLOCK * HEAD_DIM
    n = bsz * seq_len
    (u, gm, kas, kaw, slabs, qt, vts, vtw, gnt, kvct, kvst, kvwt) = _inproj_prompt(
        x_prompt.reshape(n, D_MODEL), norm_mix_pre, w_row, w_t, bsz, seq_len)
    y = _cmp_y_prompt(slabs, bsz, seq_len, wcmp)
    kc, vct = _cmp_combine(y, cmp_pe_k, cmp_w1_k, cmp_w2_k, cmp_pe_v, cmp_w1_v, cmp_w2_v)
    o = _nsa_prompt(qt, gnt, kc, vct, kas, vts, kaw, vtw, bsz, seq_len)
    return o, kvct, kvst, kvwt, u, gm


def kernel(x_prompt, x_sample, cache_kv_cmp, cache_kv_slc, state_kv_win, state_s5, page_table, norm_mix_pre, norm_mix_post, norm_mlp_pre, norm_mlp_post, w_in, cmp_pe_k, cmp_w1_k, cmp_w2_k, cmp_pe_v, cmp_w1_v, cmp_w2_v, s5_a_re, s5_a_im, s5_log_dt, s5_b_re, s5_b_im, s5_c_re, s5_c_im, s5_d, s5_w_glu, s5_b_glu, w_branch_nsa, w_branch_s5, w_out, w_mlp_up, w_mlp_down):
    bsz, seq_len, _ = x_prompt.shape
    s = x_sample.shape[0]
    w_row, w_t = _inproj_weights(w_in)
    wcmp = _cmp_weights(cmp_w1_k, cmp_w1_v)
    cmp_w = (cmp_pe_k, cmp_w1_k, cmp_w2_k, cmp_pe_v, cmp_w1_v, cmp_w2_v)
    s5_ops = _s5_prep(s5_a_re, s5_a_im, s5_log_dt, s5_b_re, s5_b_im, s5_c_re, s5_c_im)
    mlp_w = _merge_weights(s5_w_glu, s5_b_glu, w_branch_nsa, w_branch_s5, w_out, w_mlp_up, w_mlp_down,
                           norm_mix_post, norm_mlp_pre, norm_mlp_post)

    o_p, kvc_p, kvs_p, kvw_p, u_p, gm_p = prompt_attention(x_prompt, norm_mix_pre, w_row, w_t, wcmp, *cmp_w)
    ys5_p, s5_p = _s5_prompt(u_p, s5_ops, s5_d, bsz, seq_len)
    y_p = _merge_mlp(x_prompt.reshape(bsz * seq_len, D_MODEL), o_p, ys5_p, gm_p, mlp_w)

    feature_major = lambda c: jnp.transpose(c, (0, 2, 3, 4, 1))
    o_s, kvc_s, kvs_s, win_t, u_s, gm_s = sample_attention(
        x_sample, feature_major(cache_kv_cmp), feature_major(cache_kv_slc), feature_major(state_kv_win),
        page_table, norm_mix_pre, w_row, w_t, wcmp, *cmp_w)
    win_s = jnp.transpose(win_t, (0, 4, 1, 2, 3))
    ys5_s, s5_s = _s5_sample(u_s, state_s5, s5_ops, s5_d)
    ys5_s = ys5_s.reshape(s, S5_NOCT, LANES).transpose(1, 0, 2)
    y_s = _merge_mlp(x_sample.reshape(s, D_MODEL), o_s, ys5_s, gm_s, mlp_w)

    kv5 = lambda a, b, t: a.reshape(b, t, 2, N_KV, HEAD_DIM)
    token_major = lambda a: jnp.transpose(a.reshape(bsz, 2, N_KV, HEAD_DIM, -1), (0, 4, 1, 2, 3))
    win_rows = min(WINDOW, seq_len)
    win_p = token_major(kvw_p[:, :, seq_len - win_rows:])
    if win_rows < WINDOW:
        win_p = jnp.pad(win_p, ((0, 0), (WINDOW - win_rows, 0), (0, 0), (0, 0), (0, 0)))
    return (y_p.reshape(bsz, seq_len, D_MODEL), y_s.reshape(s, 1, D_MODEL),
            token_major(kvc_p), token_major(kvs_p), win_p, s5_p.astype(x_prompt.dtype),
            kv5(kvc_s, s, 1), kv5(kvs_s, s, 1), kv5(win_s, s, state_kv_win.shape[1]), s5_s.astype(state_s5.dtype))
```

```python
import functools
import math

import numpy as np
import jax
import jax.numpy as jnp
from jax import lax
from jax.experimental import pallas as pl
from jax.experimental.pallas import tpu as pltpu

F32 = jnp.float32
BF16 = jnp.bfloat16

D_MODEL = 1024
HEAD_DIM = 64
N_HEADS = 16
N_KV = 4
HPG = 4
CMP_STRIDE = 16
CMP_BLOCK = 32
SEL_BLOCK = 64
N_SELECT = 16
WINDOW = 512
Q_BLOCK = 128
S5_WIDTH = 512
S5_CH = 16
S5_GROUPS = 32
S5_STATE = 64
D_FF = 4096
D_Q = 1024
D_KV = 256
PAGE_SIZE = 128
EPS = 1e-6
NEG = -1e30
FORCE = 1e4
LANES = 128
VMEM_LIMIT = 56 * 1024 * 1024
SWEEP = 512
WIN_KEYS = WINDOW + Q_BLOCK
Q_SCALE_LOG2 = HEAD_DIM ** -0.5 * math.log2(math.e)


def _cparams(sem):
    return pltpu.CompilerParams(dimension_semantics=sem, vmem_limit_bytes=VMEM_LIMIT)


def _gelu(x):
    return 0.5 * x * (1.0 + jnp.tanh(math.sqrt(2.0 / math.pi) * (x + 0.044715 * (x * x * x))))


def _sigmoid(x):
    return 1.0 / (1.0 + jnp.exp(-x))


def _rms(x, g):
    ms = jnp.mean(x * x, axis=-1, keepdims=True)
    return (x * lax.rsqrt(ms + EPS)) * g


def _split3(x):
    hi = x.astype(BF16)
    r1 = x - hi.astype(F32)
    mid = r1.astype(BF16)
    lo = (r1 - mid.astype(F32)).astype(BF16)
    return hi, mid, lo


WT_Q, WT_GN, WT_KV = 0, D_Q, D_Q + 64
WT_ROWS = WT_KV + 6 * D_KV


def _inproj_prompt_kernel(seq_len, x_ref, g_ref, wr_ref, wt_ref,
                          u_ref, gm_ref, kas_ref, kaw_ref, slab_ref,
                          qt_ref, vts_ref, vtw_ref, gnt_ref, kvct_ref, kvst_ref, kvwt_ref):
    tm = x_ref.shape[0]
    hb = _rms(x_ref[...], g_ref[...]).astype(BF16)

    def rowdot(lo, hi):
        return jnp.dot(hb, wr_ref[:, lo:hi], preferred_element_type=F32)

    zc = rowdot(0, 512)
    for combo in range(4):
        slab_ref[combo] = zc[:, combo * LANES:(combo + 1) * LANES]
    zu = rowdot(512, 1024)
    for m in range(4):
        u_ref[m] = zu[:, m * LANES:(m + 1) * LANES]
    gm_ref[...] = rowdot(1024, 3072)
    row = pl.program_id(0) * tm + lax.broadcasted_iota(jnp.int32, (tm, LANES), 0)
    blk = lax.rem(row, seq_len) // SEL_BLOCK
    lane = lax.broadcasted_iota(jnp.int32, (tm, LANES), 1)
    onehot = jnp.where(lane - HEAD_DIM == blk, 1.0, 0.0)
    zs = rowdot(3072, 3584)
    zw = rowdot(3584, 4096)
    for g in range(N_KV):
        kas_ref[g] = (zs[:, g * LANES:(g + 1) * LANES] + onehot).astype(BF16)
        kaw_ref[g] = zw[:, g * LANES:(g + 1) * LANES].astype(BF16)
    zt = lax.dot_general(wt_ref[...], hb, (((1,), (1,)), ((), ())), preferred_element_type=F32)
    kv0 = WT_KV
    kvct_ref[0] = zt[kv0:kv0 + 512]
    kvst_ref[0] = zt[kv0 + 512:kv0 + 1024]
    kvwt_ref[0] = zt[kv0 + 1024:kv0 + 1536]
    for c in range(tm // LANES):
        sl = slice(c * LANES, (c + 1) * LANES)
        qt_ref[c] = (zt[WT_Q:WT_Q + D_Q, sl] * Q_SCALE_LOG2).astype(BF16)
        gnt_ref[c] = zt[WT_GN:WT_GN + 64, sl]
        vts_ref[c] = zt[kv0 + 768:kv0 + 1024, sl].astype(BF16)
        vtw_ref[c] = zt[kv0 + 1280:kv0 + 1536, sl].astype(BF16)


def _inproj_weights(w_in):
    wq, wkv, wgn, wu, wgm = (w_in[:, :1024], w_in[:, 1024:2560], w_in[:, 2560:2608],
                             w_in[:, 2608:3120], w_in[:, 3120:])
    wkv6 = wkv.reshape(D_MODEL, 3, 2, N_KV, HEAD_DIM)
    zpad = jnp.zeros((D_MODEL, N_KV, HEAD_DIM), F32)
    kaug_s = jnp.concatenate([wkv6[:, 1, 0], zpad], axis=-1).reshape(D_MODEL, N_KV * LANES)
    kaug_w = jnp.concatenate([wkv6[:, 2, 0], zpad], axis=-1).reshape(D_MODEL, N_KV * LANES)
    w_row = jnp.concatenate([wkv[:, 0:512], wu, wgm, kaug_s, kaug_w], axis=1).astype(BF16)
    gn_rows = jnp.pad(wgn.T.reshape(N_KV, HPG * 3, D_MODEL), ((0, 0), (0, 4), (0, 0))).reshape(64, D_MODEL)
    w_t = jnp.concatenate([wq.T, gn_rows, wkv.T], axis=0).astype(BF16)
    return w_row, w_t


def _inproj_prompt(x2, g_pre, w_row, w_t, bsz, seq_len, tm=256):
    n = x2.shape[0]
    nc = n // LANES
    cpt = tm // LANES
    per = seq_len // tm
    row = lambda w: pl.BlockSpec((tm, w), lambda i: (i, 0))
    fmaj = pl.BlockSpec((1, 512, tm), lambda i: (i // per, 0, i % per))
    out_shape = (
        jax.ShapeDtypeStruct((4, n, LANES), F32), jax.ShapeDtypeStruct((n, 2048), F32),
        jax.ShapeDtypeStruct((N_KV, n, LANES), BF16), jax.ShapeDtypeStruct((N_KV, n, LANES), BF16),
        jax.ShapeDtypeStruct((4, n, LANES), F32),
        jax.ShapeDtypeStruct((nc, 1024, LANES), BF16), jax.ShapeDtypeStruct((nc, 256, LANES), BF16),
        jax.ShapeDtypeStruct((nc, 256, LANES), BF16), jax.ShapeDtypeStruct((nc, 64, LANES), F32),
        jax.ShapeDtypeStruct((bsz, 512, seq_len), F32), jax.ShapeDtypeStruct((bsz, 512, seq_len), F32),
        jax.ShapeDtypeStruct((bsz, 512, seq_len), F32),
    )
    out_specs = (
        pl.BlockSpec((4, tm, LANES), lambda i: (0, i, 0)), row(2048),
        pl.BlockSpec((N_KV, tm, LANES), lambda i: (0, i, 0)), pl.BlockSpec((N_KV, tm, LANES), lambda i: (0, i, 0)),
        pl.BlockSpec((4, tm, LANES), lambda i: (0, i, 0)),
        pl.BlockSpec((cpt, 1024, LANES), lambda i: (i, 0, 0)), pl.BlockSpec((cpt, 256, LANES), lambda i: (i, 0, 0)),
        pl.BlockSpec((cpt, 256, LANES), lambda i: (i, 0, 0)), pl.BlockSpec((cpt, 64, LANES), lambda i: (i, 0, 0)),
        fmaj, fmaj, fmaj,
    )
    return pl.pallas_call(
        functools.partial(_inproj_prompt_kernel, seq_len),
        grid=(n // tm,),
        in_specs=[row(D_MODEL), pl.BlockSpec((1, D_MODEL), lambda i: (0, 0)),
                  pl.BlockSpec(w_row.shape, lambda i: (0, 0)), pl.BlockSpec(w_t.shape, lambda i: (0, 0))],
        out_specs=out_specs, out_shape=out_shape,
        compiler_params=_cparams(("parallel",)), name="inproj_prompt",
    )(x2, g_pre.reshape(1, D_MODEL), w_row, w_t)


def _cmp_weights(w1_k, w1_v):
    eye2 = jnp.eye(2, dtype=F32)
    out = []
    for w1 in (w1_k, w1_v):
        w = w1.reshape(2, CMP_STRIDE, HEAD_DIM, HEAD_DIM)
        big = jnp.einsum('fsdh,ij->sidfjh', w, eye2).reshape(CMP_STRIDE * 2 * HEAD_DIM, 2 * 2 * HEAD_DIM)
        out += [big, big]
    return jnp.stack(out).astype(BF16)


def _cmp_y_from_slabs(slab_ref, w_ref, y_ref, pitch=CMP_STRIDE):
    nrows = y_ref.shape[1]
    for combo in range(4):
        xg = jnp.concatenate([slab_ref[combo, pl.ds(s, nrows, stride=pitch), :] for s in range(CMP_STRIDE)],
                             axis=1)
        y_ref[0, :, combo * 256:(combo + 1) * 256] = jnp.dot(xg.astype(BF16), w_ref[combo],
                                                             preferred_element_type=F32)


def _cmp_y_prompt(slabs, bsz, seq_len, wcmp):
    nch = seq_len // CMP_STRIDE
    rb = min(nch, 128)
    per = nch // rb
    return pl.pallas_call(
        _cmp_y_from_slabs,
        grid=(bsz, per),
        in_specs=[pl.BlockSpec((4, rb * CMP_STRIDE, LANES), lambda b, j: (0, b * per + j, 0)),
                  pl.BlockSpec(wcmp.shape, lambda b, j: (0, 0, 0))],
        out_specs=pl.BlockSpec((1, rb, 1024), lambda b, j: (b, j, 0)),
        out_shape=jax.ShapeDtypeStruct((bsz, nch, 1024), F32),
        compiler_params=_cparams(("parallel", "parallel")), name="cmp_y_prompt",
    )(slabs, wcmp)


def _cmp_combine_kernel(y_ref, pe_ref, w1f_ref, w2k_ref, w2vt_ref, kc_ref, vct_ref):
    r = y_ref.shape[1]
    pos = jnp.dot(pe_ref[...], w1f_ref[...], preferred_element_type=F32,
                  precision=lax.Precision.HIGHEST)
    for combo in range(4):
        kv, gp = combo // 2, combo % 2
        first = y_ref[0, :, combo * 256: combo * 256 + LANES]
        second = pltpu.roll(y_ref[0, :, combo * 256 + LANES: combo * 256 + 2 * LANES], r - 1, 0)
        p1 = pos[0:1, kv * HEAD_DIM:(kv + 1) * HEAD_DIM]
        pre = first + second + jnp.concatenate([p1, p1], axis=1)
        act = _gelu(pre).astype(BF16)
        if kv == 0:
            kc = jnp.dot(act, w2k_ref[...], preferred_element_type=F32)
            kc_ref[0, 2 * gp] = kc[:, 0:LANES].astype(BF16)
            kc_ref[0, 2 * gp + 1] = kc[:, LANES:2 * LANES].astype(BF16)
        else:
            vct = lax.dot_general(w2vt_ref[...], act, (((1,), (1,)), ((), ())), preferred_element_type=F32)
            vct_ref[0, 2 * gp] = vct[0:HEAD_DIM].astype(BF16)
            vct_ref[0, 2 * gp + 1] = vct[HEAD_DIM:2 * HEAD_DIM].astype(BF16)


def _cmp_combine(y, pe_k, w1_k, w2_k, pe_v, w1_v, w2_v):
    s, r, _ = y.shape
    pe = jnp.concatenate([pe_k.reshape(1, -1), pe_v.reshape(1, -1)], axis=1)
    pe8 = jnp.pad(pe, ((0, 7), (0, 0)))
    z = jnp.zeros((CMP_BLOCK * HEAD_DIM, HEAD_DIM), F32)
    w1f = jnp.concatenate([jnp.concatenate([w1_k.reshape(-1, HEAD_DIM), z], axis=1),
                           jnp.concatenate([z, w1_v.reshape(-1, HEAD_DIM)], axis=1)], axis=0)
    z64 = jnp.zeros((HEAD_DIM, HEAD_DIM), F32)
    w2k = jnp.concatenate([jnp.concatenate([w2_k, z64, z64, z64], axis=1),
                           jnp.concatenate([z64, z64, w2_k, z64], axis=1)], axis=0).astype(BF16)
    w2vt = jnp.concatenate([jnp.concatenate([w2_v.T, z64], axis=1),
                            jnp.concatenate([z64, w2_v.T], axis=1)], axis=0).astype(BF16)
    full = lambda a: pl.BlockSpec(a.shape, lambda i: (0,) * a.ndim)
    return pl.pallas_call(
        _cmp_combine_kernel,
        grid=(s,),
        in_specs=[pl.BlockSpec((1, r, 1024), lambda i: (i, 0, 0)), full(pe8), full(w1f), full(w2k), full(w2vt)],
        out_specs=(pl.BlockSpec((1, N_KV, r, LANES), lambda i: (i, 0, 0, 0)),
                   pl.BlockSpec((1, N_KV, HEAD_DIM, r), lambda i: (i, 0, 0, 0))),
        out_shape=(jax.ShapeDtypeStruct((s, N_KV, r, LANES), BF16),
                   jax.ShapeDtypeStruct((s, N_KV, HEAD_DIM, r), BF16)),
        compiler_params=_cparams(("parallel",)), name="cmp_combine",
    )(y, pe8, w1f, w2k, w2vt)


def _sel_matrix(nsel, nc, ncp):
    j = np.arange(nsel)
    lo = np.clip((j * SEL_BLOCK - CMP_BLOCK) // CMP_STRIDE + 1, 0, nc)
    hi = np.clip((j * SEL_BLOCK + SEL_BLOCK - 1) // CMP_STRIDE + 1, 0, nc)
    n = np.arange(ncp)
    return ((n[None, :] >= lo[:, None]) & (n[None, :] < hi[:, None])).astype(np.float32)


def _rank_select(score, k_sel):
    nj, nl = score.shape
    sub = 8
    tiles = [score[v * sub:(v + 1) * sub] for v in range(nj // sub)]
    cnts = [jnp.zeros((sub, nl), F32) for _ in tiles]
    jloc = lax.broadcasted_iota(jnp.int32, (sub, nl), 0)
    for i in range(nj):
        bi = jnp.broadcast_to(score[i:i + 1, :], (sub, nl))
        for v, t in enumerate(tiles):
            if v * sub > i:
                inc = jnp.where(bi >= t, 1.0, 0.0)
            elif v * sub + sub - 1 < i:
                inc = jnp.where(bi > t, 1.0, 0.0)
            else:
                inc = jnp.where(jloc > i - v * sub, jnp.where(bi >= t, 1.0, 0.0), jnp.where(bi > t, 1.0, 0.0))
            cnts[v] = cnts[v] + inc
    cnt = jnp.concatenate(cnts, axis=0)
    return jnp.where(cnt < k_sel, 1.0, 0.0)


NSA_TILES_PER_STEP = 4


def _nsa_prompt_kernel(nc_valid, k_sel, qt_ref, gnt_ref, kc_ref, vct_ref, kas_ref, vts_ref, kaw_ref, vtw_ref,
                       a_ref, o_ref, *score_bufs):
    for t in range(NSA_TILES_PER_STEP):
        one = pl.ds(t, 1)
        _nsa_prompt_tile(nc_valid, k_sel, pl.program_id(2) * NSA_TILES_PER_STEP + t,
                         qt_ref.at[one], gnt_ref.at[one], kc_ref, vct_ref, kas_ref, vts_ref, kaw_ref, vtw_ref, a_ref,
                         o_ref.at[pl.ds(t * Q_BLOCK, Q_BLOCK)], score_bufs[2 * t], score_bufs[2 * t + 1])


def _nsa_prompt_tile(nc_valid, k_sel, qb, qt_ref, gnt_ref, kc_ref, vct_ref, kas_ref, vts_ref, kaw_ref, vtw_ref,
                     a_ref, o_ref, sa_ref, sb_ref):
    heads = lambda t: jnp.concatenate([t] * HPG, axis=1)

    def pv_and_sum(vt, p):
        va = jnp.concatenate([vt, jnp.ones((16, vt.shape[1]), BF16)], axis=0)
        out = jnp.dot(va, p.astype(BF16), preferred_element_type=F32)
        return out[0:HEAD_DIM], out[HEAD_DIM:HEAD_DIM + 1]
    qt = qt_ref[0]
    ql = lax.broadcasted_iota(jnp.int32, (1, Q_BLOCK), 1)
    tl2 = lax.broadcasted_iota(jnp.int32, (Q_BLOCK, Q_BLOCK), 0)
    ql2 = lax.broadcasted_iota(jnp.int32, (Q_BLOCK, Q_BLOCK), 1)
    b_diag = jnp.where(tl2 <= ql2, 0.0, NEG)
    b_first = jnp.where(tl2 > ql2, 0.0, NEG)
    zero_q = jnp.zeros((HEAD_DIM, Q_BLOCK), BF16)
    rhs_q = jnp.concatenate(
        [jnp.concatenate([qt[h * HEAD_DIM:(h + 1) * HEAD_DIM], zero_q], axis=0) for h in range(HPG)], axis=1)

    ncp = kc_ref.shape[2]
    assert nc_valid >= ncp - 1
    sc = jnp.dot(kc_ref[0, 0], rhs_q, preferred_element_type=F32)
    edge = lax.shift_right_arithmetic(ql - (CMP_BLOCK - 1), 4)
    nrel = lax.broadcasted_iota(jnp.int32, (ncp, Q_BLOCK), 0) - qb * (Q_BLOCK // CMP_STRIDE)
    sc = sc + heads(jnp.where(nrel <= edge, 0.0, NEG))
    e = jnp.exp2(sc - jnp.max(sc, axis=0, keepdims=True))
    den = jnp.sum(e, axis=0, keepdims=True)
    any_visible = heads(jnp.where(qb * Q_BLOCK + ql >= CMP_BLOCK - 1, 1.0, 0.0))
    p = e * (any_visible / den)
    o_c = jnp.dot(vct_ref[0, 0], p.astype(BF16), preferred_element_type=F32)
    imp = p[:, 0:Q_BLOCK]
    for h in range(1, HPG):
        imp = imp + p[:, h * Q_BLOCK:(h + 1) * Q_BLOCK]
    a = a_ref[...]
    imp_sel = sum(jnp.dot(a, part, preferred_element_type=F32) for part in _split3(imp))

    c0 = jnp.maximum(qb - WINDOW // Q_BLOCK, 0)
    wstart = pl.multiple_of(c0 * Q_BLOCK, Q_BLOCK)
    sw = jnp.dot(kaw_ref[0, pl.ds(wstart, WIN_KEYS), :], rhs_q, preferred_element_type=F32)
    wbias = []
    for i in range(WIN_KEYS // Q_BLOCK):
        d = qb - c0 - i
        wbias.append(jnp.where(d == WINDOW // Q_BLOCK, b_first,
                               jnp.where(d == 0, b_diag, jnp.where(d < 0, NEG, 0.0))))
    sw = sw + heads(jnp.concatenate(wbias, axis=0))
    vtw = jnp.concatenate([vtw_ref[c0 + i] for i in range(WIN_KEYS // Q_BLOCK)], axis=1)
    o_w, lw = pv_and_sum(vtw, jnp.exp2(sw - jnp.max(sw, axis=0, keepdims=True)))
    o_w = o_w / lw

    nsel = imp_sel.shape[0]
    jrow = lax.broadcasted_iota(jnp.int32, (nsel, Q_BLOCK), 0)
    qp1 = qb * Q_BLOCK + lax.broadcasted_iota(jnp.int32, (nsel, Q_BLOCK), 1)
    cur = qp1 // SEL_BLOCK
    forced = (jrow == 0) | (jrow == cur) | (jrow == cur - 1)
    score = jnp.where(jrow <= cur, jnp.where(forced, FORCE, imp_sel), -1.0)
    sel = jnp.where(jrow <= cur, _rank_select(score, k_sel), 0.0)
    if nsel < HEAD_DIM:
        sel = jnp.concatenate([sel, jnp.zeros((HEAD_DIM - nsel, Q_BLOCK), F32)], axis=0)
    mq = (sel - 1.0) * 1e30
    jrow64 = lax.broadcasted_iota(jnp.int32, (HEAD_DIM, Q_BLOCK), 0)
    mq_past = jnp.where(jrow64 >= qb * (Q_BLOCK // SEL_BLOCK), NEG, mq)

    def with_mask(mrows):
        mb = mrows.astype(BF16)
        return jnp.concatenate(
            [jnp.concatenate([qt[h * HEAD_DIM:(h + 1) * HEAD_DIM], mb], axis=0) for h in range(HPG)], axis=1)

    rhs_diag, rhs_past = with_mask(mq), with_mask(mq_past)

    dstart = pl.multiple_of(qb * Q_BLOCK, Q_BLOCK)
    sd = jnp.dot(kas_ref[0, pl.ds(dstart, Q_BLOCK), :], rhs_diag, preferred_element_type=F32) + heads(b_diag)
    m = jnp.max(sd, axis=0, keepdims=True)
    acc, l = pv_and_sum(vts_ref[qb], jnp.exp2(sd - m))

    cps = SWEEP // Q_BLOCK
    n_span = (qb + cps - 1) // cps

    def span_scores(i, buf):
        start = pl.multiple_of(i * SWEEP, SWEEP)
        s = jnp.dot(kas_ref[0, pl.ds(start, SWEEP), :], rhs_past, preferred_element_type=F32)
        buf[...] = s
        return jnp.max(s, axis=0, keepdims=True)

    def span_consume(i, buf, smax, m, l, acc):
        m_new = jnp.maximum(m, smax)
        alpha = jnp.exp2(m - m_new)
        vt = jnp.concatenate([vts_ref[cps * i + k] for k in range(cps)], axis=1)
        pv, psum = pv_and_sum(vt, jnp.exp2(buf[...] - m_new))
        return m_new, l * alpha + psum, acc * alpha + pv

    n_pair = (n_span + 1) // 2

    def pair(k, carry, prefetch):
        m, l, acc, smax0 = carry
        smax1 = span_scores(2 * k + 1, sb_ref)
        m, l, acc = span_consume(2 * k, sa_ref, smax0, m, l, acc)
        smax0 = span_scores(2 * k + 2, sa_ref) if prefetch else smax0
        m, l, acc = span_consume(2 * k + 1, sb_ref, smax1, m, l, acc)
        return m, l, acc, smax0

    carry = lax.fori_loop(0, n_pair - 1, lambda k, c: pair(k, c, True), (m, l, acc, span_scores(0, sa_ref)))
    m, l, acc, _ = pair(jnp.maximum(n_pair - 1, 0), carry, False)
    o_s = acc / l

    gate = _sigmoid(gnt_ref[0])
    outs = []
    for h in range(HPG):
        sl = slice(h * Q_BLOCK, (h + 1) * Q_BLOCK)
        outs.append(gate[3 * h:3 * h + 1] * o_c[:, sl] + gate[3 * h + 1:3 * h + 2] * o_s[:, sl]
                    + gate[3 * h + 2:3 * h + 3] * o_w[:, sl])
    o_t = jnp.concatenate(outs, axis=0)
    o_ref[...] = o_t.T.astype(BF16)


def _nsa_prompt(qt, gnt, kc, vct, kas, vts, kaw, vtw, bsz, seq_len):
    nq = seq_len // Q_BLOCK
    nsel = seq_len // SEL_BLOCK
    nc_valid = seq_len // CMP_STRIDE - 1
    ncp = kc.shape[2]
    k_sel = min(N_SELECT, nsel)
    a = jnp.asarray(_sel_matrix(nsel, nc_valid, ncp), BF16)
    n = bsz * seq_len
    tps = NSA_TILES_PER_STEP
    steps = nq // tps
    score_buf = pltpu.VMEM((SWEEP, HPG * Q_BLOCK), F32)
    return pl.pallas_call(
        functools.partial(_nsa_prompt_kernel, nc_valid, k_sel),
        grid=(bsz, N_KV, steps),
        in_specs=[
            pl.BlockSpec((tps, HPG * HEAD_DIM, LANES), lambda b, g, i: (b * steps + i, g, 0)),
            pl.BlockSpec((tps, 16, LANES), lambda b, g, i: (b * steps + i, g, 0)),
            pl.BlockSpec((1, 1, ncp, LANES), lambda b, g, i: (b, g, 0, 0)),
            pl.BlockSpec((1, 1, HEAD_DIM, ncp), lambda b, g, i: (b, g, 0, 0)),
            pl.BlockSpec((1, seq_len, LANES), lambda b, g, i: (g, b, 0)),
            pl.BlockSpec((nq, HEAD_DIM, LANES), lambda b, g, i: (b, g, 0)),
            pl.BlockSpec((1, seq_len, LANES), lambda b, g, i: (g, b, 0)),
            pl.BlockSpec((nq, HEAD_DIM, LANES), lambda b, g, i: (b, g, 0)),
            pl.BlockSpec(a.shape, lambda b, g, i: (0, 0)),
        ],
        out_specs=pl.BlockSpec((tps * Q_BLOCK, HPG * HEAD_DIM), lambda b, g, i: (b * steps + i, g)),
        out_shape=jax.ShapeDtypeStruct((n, D_Q), BF16),
        scratch_shapes=[score_buf] * (2 * tps),
        compiler_params=_cparams(("parallel", "parallel", "arbitrary")), name="nsa_prompt",
    )(qt, gnt, kc, vct, kas, vts, kaw, vtw, a)


S5_L = 16
S5_W = S5_L * S5_CH
S5_P = 2 * S5_STATE
S5_OCT = LANES // S5_CH
S5_NOCT = S5_GROUPS // S5_OCT
S5_OW = S5_L * LANES
S5_OP = S5_OCT * S5_P


def _s5_prep_kernel(are_ref, aim_ref, ldt_ref, bre_ref, bim_ref, cre_ref, cim_ref, spread_ref,
                    tg_ref, sg_ref, ogt_ref, misc_ref, bs_ref, oct_ref):
    g8 = lax.rem(pl.program_id(0), S5_OCT)
    are, aim = are_ref[0], aim_ref[0]
    dt = jnp.exp(ldt_ref[0])
    mag = jnp.exp(are * dt)
    ar, ai = mag * jnp.cos(aim * dt), mag * jnp.sin(aim * dt)
    den = are * are + aim * aim
    fr = ((ar - 1.0) * are + ai * aim) / den
    fi = (ai * are - (ar - 1.0) * aim) / den
    bre, bim = bre_ref[0], bim_ref[0]
    cre, cim = cre_ref[0], cim_ref[0]
    br, bi = fr * bre - fi * bim, fr * bim + fi * bre
    lo16 = lax.broadcasted_iota(jnp.int32, (S5_CH, S5_P), 1) < S5_STATE
    lo1 = lax.broadcasted_iota(jnp.int32, (1, S5_P), 1) < S5_STATE
    pr, pi = [jnp.ones_like(ar)], [jnp.zeros_like(ar)]
    for _ in range(S5_L):
        pr.append(pr[-1] * ar - pi[-1] * ai)
        pi.append(pr[-2] * ai + pi[-1] * ar)
    cpr = [cre * pr[k] - cim * pi[k] for k in range(S5_L + 1)]
    cpi = [cre * pi[k] + cim * pr[k] for k in range(S5_L + 1)]
    rpack = jnp.concatenate([jnp.where(lo16, cpr[k], cpi[k]) for k in range(S5_L)], axis=0)
    bpack = jnp.where(lo16, br, -bi)
    krow = lax.dot_general(bpack, rpack, (((1,), (1,)), ((), ())), preferred_element_type=F32,
                           precision=lax.Precision.HIGHEST)
    wide = jnp.dot(krow.astype(BF16), spread_ref[...], preferred_element_type=F32)
    wide = pltpu.roll(wide, g8 * S5_CH, 1)
    lane = lax.broadcasted_iota(jnp.int32, (S5_CH, S5_OW), 1)
    zpad = jnp.zeros((S5_CH, S5_OP - S5_P), F32)
    place = lambda blk: pltpu.roll(jnp.concatenate([blk, zpad], axis=1), g8 * S5_P, 1).astype(BF16)
    for j in range(S5_L):
        rows = pl.ds(pl.multiple_of(j * LANES + g8 * S5_CH, S5_CH), S5_CH)
        shifted = wide if j == 0 else pltpu.roll(wide, j * LANES, 1)
        tg_ref[0, rows, :] = jnp.where(lane >= j * LANES, shifted, 0.0).astype(BF16)
        k = S5_L - 1 - j
        sblk = jnp.where(lo16, pr[k] * br - pi[k] * bi, pr[k] * bi + pi[k] * br)
        sg_ref[0, rows, :] = place(sblk)
        if j == S5_L - 1:
            bs_ref[0] = sblk
        ogt_ref[0, rows, :] = place(jnp.where(lo16, cpr[j + 1], -cpi[j + 1]))
    oct_ref[0] = jnp.where(lo16, cre, -cim)
    misc_ref[0] = jnp.concatenate([
        pr[S5_L], jnp.where(lo1, -pi[S5_L], pi[S5_L]), ar, jnp.where(lo1, -ai, ai),
        jnp.zeros((4, S5_P), F32)], axis=0)


def _s5_prep(a_re, a_im, log_dt, b_re, b_im, c_re, c_im):
    g = S5_GROUPS
    dup = lambda a: jnp.concatenate([a, a], axis=-1)
    are, aim = dup(a_re).reshape(g, 1, S5_P), dup(a_im).reshape(g, 1, S5_P)
    ldt = jnp.broadcast_to(log_dt.reshape(g, 1, 1), (g, 1, S5_P))
    bre, bim = dup(jnp.swapaxes(b_re, 1, 2)), dup(jnp.swapaxes(b_im, 1, 2))
    cre, cim = dup(c_re), dup(c_im)
    v1 = pl.BlockSpec((1, 1, S5_P), lambda i: (i, 0, 0))
    v16 = pl.BlockSpec((1, S5_CH, S5_P), lambda i: (i, 0, 0))
    lag_co = np.arange(S5_W)
    spread = np.zeros((S5_W, S5_OW), np.float32)
    spread[lag_co, (lag_co // S5_CH) * LANES + lag_co % S5_CH] = 1.0
    spread = jnp.asarray(spread, BF16)
    octet = lambda w: pl.BlockSpec((1, S5_OW, w), lambda i: (i // S5_OCT, 0, 0))
    return pl.pallas_call(
        _s5_prep_kernel, grid=(g,),
        in_specs=[v1, v1, v1, v16, v16, v16, v16, pl.BlockSpec(spread.shape, lambda i: (0, 0))],
        out_specs=(octet(S5_OW), octet(S5_OP), octet(S5_OP), pl.BlockSpec((1, 8, S5_P), lambda i: (i, 0, 0)),
                   v16, v16),
        out_shape=(jax.ShapeDtypeStruct((S5_NOCT, S5_OW, S5_OW), BF16), jax.ShapeDtypeStruct((S5_NOCT, S5_OW, S5_OP), BF16),
                   jax.ShapeDtypeStruct((S5_NOCT, S5_OW, S5_OP), BF16), jax.ShapeDtypeStruct((g, 8, S5_P), F32),
                   jax.ShapeDtypeStruct((g, S5_CH, S5_P), F32), jax.ShapeDtypeStruct((g, S5_CH, S5_P), F32)),
        compiler_params=_cparams(("arbitrary",)), name="s5_prep",
    )(are, aim, ldt, bre, bim, cre, cim, spread)


def _s5_chunks(u_ref):
    nch = u_ref.shape[1] // S5_L
    return jnp.concatenate([u_ref[0, pl.ds(j, nch, stride=S5_L), :] for j in range(S5_L)], axis=1)


def _s5_sum_kernel(u_ref, sg_ref, s_ref):
    s_ref[0] = jnp.dot(_s5_chunks(u_ref).astype(BF16), sg_ref[0], preferred_element_type=F32)


def _s5_scan_kernel(s_ref, a1_ref, a2_ref, h_ref, last_ref, carry):
    @pl.when(pl.program_id(0) == 0)
    def _():
        carry[...] = jnp.zeros_like(carry)

    a1, a2 = a1_ref[...], a2_ref[...]

    def step(c, h):
        h_ref[c] = h
        return a1 * h + a2 * pltpu.roll(h, S5_STATE, 1) + s_ref[c]

    h = lax.fori_loop(0, s_ref.shape[0], step, carry[...])
    carry[...] = h
    last_ref[...] = h


def _s5_out_kernel(u_ref, h_ref, tg_ref, ogt_ref, d_ref, y_ref):
    u = _s5_chunks(u_ref)
    y = jnp.dot(u.astype(BF16), tg_ref[0], preferred_element_type=F32)
    y = y + lax.dot_general(h_ref[0].astype(BF16), ogt_ref[0], (((1,), (1,)), ((), ())), preferred_element_type=F32)
    y = y + d_ref[0] * u
    nch = y.shape[0]
    for k in range(S5_L):
        y_ref[0, pl.ds(k, nch, stride=S5_L), :] = y[:, k * LANES:(k + 1) * LANES]


def _s5_prompt(uslab, ops, s5_d, bsz, seq_len):
    tg, sg, ogt, misc = ops[0], ops[1], ops[2], ops[3]
    g, nch = S5_GROUPS, seq_len // S5_L
    rows = bsz * nch
    slab = pl.BlockSpec((1, seq_len, LANES), lambda m, b: (m, b, 0))
    per_oct = lambda r, c: pl.BlockSpec((1, r, c), lambda m, b: (m, 0, 0))
    per_row = lambda c: pl.BlockSpec((1, nch, c), lambda m, b: (m, b, 0))
    ssum = pl.pallas_call(
        _s5_sum_kernel, grid=(S5_NOCT, bsz), in_specs=[slab, per_oct(S5_OW, S5_OP)],
        out_specs=per_row(S5_OP), out_shape=jax.ShapeDtypeStruct((S5_NOCT, rows, S5_OP), F32),
        compiler_params=_cparams(("parallel", "parallel")), name="s5_sum",
    )(uslab, sg)
    s_cm = (ssum.reshape(S5_NOCT, bsz, nch, S5_OCT, S5_P).transpose(2, 1, 0, 3, 4).reshape(nch, bsz * g, S5_P))
    a1 = jnp.tile(misc[:, 0, :], (bsz, 1))
    a2 = jnp.tile(misc[:, 1, :], (bsz, 1))
    cb = min(nch, 32)
    hs, last = pl.pallas_call(
        _s5_scan_kernel, grid=(nch // cb,),
        in_specs=[pl.BlockSpec((cb, bsz * g, S5_P), lambda i: (i, 0, 0)),
                  pl.BlockSpec((bsz * g, S5_P), lambda i: (0, 0)), pl.BlockSpec((bsz * g, S5_P), lambda i: (0, 0))],
        out_specs=(pl.BlockSpec((cb, bsz * g, S5_P), lambda i: (i, 0, 0)), pl.BlockSpec((bsz * g, S5_P), lambda i: (0, 0))),
        out_shape=(jax.ShapeDtypeStruct((nch, bsz * g, S5_P), F32), jax.ShapeDtypeStruct((bsz * g, S5_P), F32)),
        scratch_shapes=[pltpu.VMEM((bsz * g, S5_P), F32)],
        compiler_params=_cparams(("arbitrary",)), name="s5_scan",
    )(s_cm, a1, a2)
    h_oct = (hs.reshape(nch, bsz, S5_NOCT, S5_OCT, S5_P).transpose(2, 1, 0, 3, 4).reshape(S5_NOCT, rows, S5_OP))
    dvec = jnp.tile(s5_d.reshape(S5_NOCT, 1, LANES), (1, 1, S5_L))
    yslab = pl.pallas_call(
        _s5_out_kernel, grid=(S5_NOCT, bsz),
        in_specs=[slab, per_row(S5_OP), per_oct(S5_OW, S5_OW), per_oct(S5_OW, S5_OP), per_oct(1, S5_OW)],
        out_specs=slab, out_shape=jax.ShapeDtypeStruct((S5_NOCT, bsz * seq_len, LANES), F32),
        compiler_params=_cparams(("parallel", "parallel")), name="s5_out",
    )(uslab, h_oct, tg, ogt, dvec)
    state = last.reshape(bsz, g, 2, S5_STATE).transpose(0, 1, 3, 2)
    return yslab, state


def _s5_sample_kernel(u_ref, h0_ref, bs_ref, oct_ref, misc_ref, d_ref, y_ref, h1_ref):
    u, h0 = u_ref[0], h0_ref[0]
    hi = lax.Precision.HIGHEST
    bu = jnp.dot(u, bs_ref[0], preferred_element_type=F32, precision=hi)
    h1 = misc_ref[0, 2:3] * h0 + misc_ref[0, 3:4] * pltpu.roll(h0, S5_STATE, 1) + bu
    h1_ref[0] = h1
    y = lax.dot_general(h1, oct_ref[0], (((1,), (1,)), ((), ())), preferred_element_type=F32, precision=hi)
    y_ref[0] = y + d_ref[0] * u


def _s5_sample(u, state, ops, s5_d):
    misc, bs, oct_ = ops[3], ops[4], ops[5]
    s, g = u.shape[0], S5_GROUPS
    ug = u.reshape(s, g, S5_CH).transpose(1, 0, 2)
    h0 = state.astype(F32).transpose(1, 0, 3, 2).reshape(g, s, S5_P)
    gspec = lambda r, c: pl.BlockSpec((1, r, c), lambda i: (i, 0, 0))
    y, h1 = pl.pallas_call(
        _s5_sample_kernel, grid=(g,),
        in_specs=[gspec(s, S5_CH), gspec(s, S5_P), gspec(S5_CH, S5_P), gspec(S5_CH, S5_P), gspec(8, S5_P), gspec(1, S5_CH)],
        out_specs=(gspec(s, S5_CH), gspec(s, S5_P)),
        out_shape=(jax.ShapeDtypeStruct((g, s, S5_CH), F32), jax.ShapeDtypeStruct((g, s, S5_P), F32)),
        compiler_params=_cparams(("parallel",)), name="s5_sample",
    )(ug, h0, bs, oct_, misc, s5_d.reshape(g, 1, S5_CH))
    return (y.transpose(1, 0, 2).reshape(s, S5_WIDTH),
            h1.reshape(g, s, 2, S5_STATE).transpose(1, 0, 3, 2))


def _merge_mlp_kernel(x_ref, o_ref, ys_ref, gm_ref, wglu_ref, bglu_ref, wbn_ref, wbs_ref, wout_ref, wup_ref, wdn_ref,
                      npost_ref, nmpre_ref, nmpost_ref, out_ref):
    dot = lambda a, w_ref: jnp.dot(a.astype(BF16), w_ref[...], preferred_element_type=F32)
    z = _gelu(jnp.concatenate([ys_ref[m] for m in range(S5_NOCT)], axis=1))
    o_s5 = z * _sigmoid(dot(z, wglu_ref) + bglu_ref[...])
    merged = (_sigmoid(gm_ref[:, 0:D_MODEL]) * dot(o_ref[...], wbn_ref)
              + _sigmoid(gm_ref[:, D_MODEL:2 * D_MODEL]) * dot(o_s5, wbs_ref))
    x1 = x_ref[...] + _rms(dot(merged, wout_ref), npost_ref[...])
    hm = _rms(x1, nmpre_ref[...])
    up = jnp.maximum(dot(hm, wup_ref), 0.0)
    f = dot(up * up, wdn_ref)
    out_ref[...] = x1 + _rms(f, nmpost_ref[...])


def _merge_mlp(x2, o_nsa, y_s5, gm, wts, tm=256):
    n = x2.shape[0]
    tm = min(tm, n)
    row = lambda w: pl.BlockSpec((tm, w), lambda i: (i, 0))
    const = lambda a: pl.BlockSpec(a.shape, lambda i: (0, 0), pipeline_mode=pl.Buffered(1))
    return pl.pallas_call(
        _merge_mlp_kernel, grid=(n // tm,),
        in_specs=[row(D_MODEL), row(D_Q), pl.BlockSpec((S5_NOCT, tm, LANES), lambda i: (0, i, 0)), row(2 * D_MODEL)]
                 + [const(a) for a in wts],
        out_specs=row(D_MODEL), out_shape=jax.ShapeDtypeStruct((n, D_MODEL), F32),
        compiler_params=_cparams(("parallel",)), name="merge_mlp",
    )(x2, o_nsa, y_s5, gm, *wts)


def _merge_weights(s5_w_glu, s5_b_glu, w_branch_nsa, w_branch_s5, w_out, w_mlp_up, w_mlp_down,
                   norm_mix_post, norm_mlp_pre, norm_mlp_post):
    r = lambda v: v.reshape(1, -1).astype(F32)
    b = lambda w: w.astype(BF16)
    return (b(s5_w_glu), r(s5_b_glu), b(w_branch_nsa), b(w_branch_s5), b(w_out), b(w_mlp_up), b(w_mlp_down),
            r(norm_mix_post), r(norm_mlp_pre), r(norm_mlp_post))


def _inproj_sample_kernel(x_ref, g_ref, wr_ref, wt_ref, kv_ref, u_ref, gm_ref, qg_ref):
    hb = _rms(x_ref[...], g_ref[...]).astype(BF16)
    u_ref[...] = jnp.dot(hb, wr_ref[:, 512:1024], preferred_element_type=F32)
    gm_ref[...] = jnp.dot(hb, wr_ref[:, 1024:3072], preferred_element_type=F32)
    z = lax.dot_general(hb, wt_ref[...], (((1,), (1,)), ((), ())), preferred_element_type=F32)
    qg_ref[...] = z[:, 0:WT_KV]
    kv_ref[...] = z[:, WT_KV:WT_ROWS]


def _inproj_sample(x2, g_pre, w_row, w_t):
    s = x2.shape[0]
    full = lambda a: pl.BlockSpec(a.shape, lambda i: (0, 0))
    o = lambda w: pl.BlockSpec((s, w), lambda i: (0, 0))
    return pl.pallas_call(
        _inproj_sample_kernel, grid=(1,),
        in_specs=[o(D_MODEL), pl.BlockSpec((1, D_MODEL), lambda i: (0, 0)), full(w_row), full(w_t)],
        out_specs=(o(1536), o(512), o(2048), o(WT_KV)),
        out_shape=(jax.ShapeDtypeStruct((s, 1536), F32), jax.ShapeDtypeStruct((s, 512), F32),
                   jax.ShapeDtypeStruct((s, 2048), F32), jax.ShapeDtypeStruct((s, WT_KV), F32)),
        compiler_params=_cparams(("arbitrary",)), name="inproj_sample",
    )(x2, g_pre.reshape(1, D_MODEL), w_row, w_t)


CMP_PAGES_PER_STEP = 32
CMP_SLAB_PITCH = CMP_STRIDE + 1


def _cmp_y_sample_kernel(pps, pt_ref, *refs):
    x_refs, w_ref, y_ref, slab = refs[:pps], refs[pps], refs[pps + 1], refs[pps + 2]
    cpp = PAGE_SIZE // CMP_STRIDE
    for r, x in enumerate(x_refs):
        for combo in range(4):
            kv, gp = combo // 2, combo % 2
            pair_t = x[0, kv, 2 * gp:2 * gp + 2].reshape(2 * HEAD_DIM, PAGE_SIZE).T
            for n in range(cpp):
                row0 = (r * cpp + n) * CMP_SLAB_PITCH
                slab[combo, row0:row0 + CMP_STRIDE, :] = pair_t[n * CMP_STRIDE:(n + 1) * CMP_STRIDE]
    _cmp_y_from_slabs(slab, w_ref, y_ref, CMP_SLAB_PITCH)


def _cmp_y_sample(cache_t, page_table, wcmp):
    s, n_pages = page_table.shape
    cpp = PAGE_SIZE // CMP_STRIDE
    pps = min(CMP_PAGES_PER_STEP, n_pages)
    assert n_pages % pps == 0
    steps = n_pages // pps

    def page_spec(r):
        return pl.BlockSpec((1, 2, N_KV, HEAD_DIM, PAGE_SIZE), lambda b, j, pt: (pt[b, j * pps + r], 0, 0, 0, 0))

    grid_spec = pltpu.PrefetchScalarGridSpec(
        num_scalar_prefetch=1, grid=(s, steps),
        in_specs=[page_spec(r) for r in range(pps)] + [pl.BlockSpec(wcmp.shape, lambda b, j, pt: (0, 0, 0))],
        out_specs=pl.BlockSpec((1, pps * cpp, 1024), lambda b, j, pt: (b, j, 0)),
        scratch_shapes=[pltpu.VMEM((4, pps * cpp * CMP_SLAB_PITCH, LANES), F32)],
    )
    return pl.pallas_call(
        functools.partial(_cmp_y_sample_kernel, pps), grid_spec=grid_spec,
        out_shape=jax.ShapeDtypeStruct((s, n_pages * cpp, 1024), F32),
        compiler_params=_cparams(("parallel", "parallel")), name="cmp_y_sample",
    )(page_table, *([cache_t] * pps), wcmp)


def _nsa_sample_cmp_kernel(nc_valid, qbd_ref, kc_ref, vct_ref, a_ref, oc_ref, isel_ref):
    ncp = kc_ref.shape[2]
    sc = jnp.dot(kc_ref[0, 0], qbd_ref[0, 0], preferred_element_type=F32)
    for g in range(1, N_KV):
        sc = sc + jnp.dot(kc_ref[0, g], qbd_ref[0, g], preferred_element_type=F32)
    nrow = lax.broadcasted_iota(jnp.int32, (ncp, LANES), 0)
    cmask = nrow < nc_valid
    sc = jnp.where(cmask, sc, NEG)
    e = jnp.where(cmask, jnp.exp(sc - jnp.max(sc, axis=0, keepdims=True)), 0.0)
    p = e / jnp.sum(e, axis=0, keepdims=True)
    pb = p.astype(BF16)
    for g in range(N_KV):
        oc_ref[0, g] = jnp.dot(vct_ref[0, g], pb, preferred_element_type=F32)
    a = a_ref[...]
    r = sum(jnp.dot(a, part, preferred_element_type=F32) for part in _split3(p))
    tot = r
    for h in range(1, HPG):
        tot = tot + pltpu.roll(r, LANES - h * N_KV, 1)
    isel_ref[0] = tot


def _topk_sample_kernel(nsel, cur, k_sel, isel_ref, tri_ref, idx_ref):
    jp = isel_ref.shape[0]
    jrow = lax.broadcasted_iota(jnp.int32, (jp, LANES), 0)
    forced = (jrow == 0) | (jrow == cur) | (jrow == cur - 1)
    score = jnp.where(jrow <= cur, jnp.where(forced, FORCE, isel_ref[...]), -1.0)
    score = jnp.where(jrow < nsel, score, -2.0)
    sel = _rank_select(score, k_sel)
    rank = jnp.dot(tri_ref[...], sel.astype(BF16), preferred_element_type=F32)
    jf = jrow.astype(F32)
    rows = [jnp.sum(jnp.where((sel > 0.5) & (rank == float(r + 1)), jf, 0.0), axis=0, keepdims=True)
            for r in range(k_sel)]
    idx_ref[...] = jnp.concatenate(rows, axis=0).astype(jnp.int32)


def _nsa_sample_attn_kernel(n_cache, tbl_ref, q_ref, *refs):
    blocks = refs[:N_KV * n_cache]
    kvs_ref, win_ref, kvw_ref, kvwc_ref, oc_ref, gate_ref, o_ref, wout_ref = refs[N_KV * n_cache:]
    b = pl.program_id(0)
    nt = (((1,), (1,)), ((), ()))
    lane = lax.broadcasted_iota(jnp.int32, (8, PAGE_SIZE), 1)
    kvw_new = kvw_ref[0]
    kvs_new = kvs_ref[0]
    nwin = win_ref.shape[4]
    wlane = lax.broadcasted_iota(jnp.int32, (HEAD_DIM, nwin), 1)
    for c in range(2 * N_KV):
        shifted = pltpu.roll(win_ref[0, c // N_KV, c % N_KV], nwin - 1, 1)
        wout_ref[0, c // N_KV, c % N_KV] = jnp.where(wlane == nwin - 1, kvwc_ref[0, c], shifted)
    for g in range(N_KV):
        col = g * HEAD_DIM
        q = q_ref[0, g]
        qf = q.astype(F32)
        pages = blocks[g * n_cache:(g + 1) * n_cache]
        kt = jnp.concatenate([pg[0, 0, 0].astype(BF16) for pg in pages], axis=1)
        vt = jnp.concatenate([pg[0, 1, 0].astype(BF16) for pg in pages], axis=1)
        bias = []
        for r in range(n_cache):
            half = lax.rem(tbl_ref[(b * N_KV + g) * n_cache + r], 2)
            bias.append(jnp.where((lane >= half * SEL_BLOCK) & (lane < (half + 1) * SEL_BLOCK), 0.0, NEG))
        s = jnp.dot(q, kt, preferred_element_type=F32) + jnp.concatenate(bias, axis=1)
        kn = kvs_new[:, col:col + HEAD_DIM].astype(BF16).astype(F32)
        vn = kvs_new[:, 256 + col:256 + col + HEAD_DIM].astype(BF16).astype(F32)
        s_new = jnp.sum(qf * kn, axis=1, keepdims=True)
        m_s = jnp.maximum(jnp.max(s, axis=1, keepdims=True), s_new)
        p = jnp.exp(s - m_s)
        p_new = jnp.exp(s_new - m_s)
        l_s = jnp.sum(p, axis=1, keepdims=True) + p_new
        o_s = (lax.dot_general(p.astype(BF16), vt, nt, preferred_element_type=F32)
               + p_new.astype(BF16).astype(F32) * vn) / l_s
        kw = win_ref[0, 0, g].astype(BF16)
        vw = win_ref[0, 1, g].astype(BF16)
        sw = jnp.dot(q, kw, preferred_element_type=F32)
        keep = lax.broadcasted_iota(jnp.int32, sw.shape, 1) >= 1
        sw = jnp.where(keep, sw, NEG)
        kwn = kvw_new[:, col:col + HEAD_DIM].astype(BF16).astype(F32)
        vwn = kvw_new[:, 256 + col:256 + col + HEAD_DIM].astype(BF16).astype(F32)
        sw_new = jnp.sum(qf * kwn, axis=1, keepdims=True)
        mw = jnp.maximum(jnp.max(sw, axis=1, keepdims=True), sw_new)
        pw = jnp.where(keep, jnp.exp(sw - mw), 0.0)
        pw_new = jnp.exp(sw_new - mw)
        lw = jnp.sum(pw, axis=1, keepdims=True) + pw_new
        o_w = (lax.dot_general(pw.astype(BF16), vw, nt, preferred_element_type=F32)
               + pw_new.astype(BF16).astype(F32) * vwn) / lw
        gate = _sigmoid(gate_ref[0, g])
        o_ref[0, g] = gate[0] * oc_ref[0, g] + gate[1] * o_s + gate[2] * o_w


def sample_attention(x_sample, cmp_t, slc_t, win_t, page_table, norm_mix_pre, w_row, w_t,
                     wcmp, cmp_pe_k, cmp_w1_k, cmp_w2_k, cmp_pe_v, cmp_w1_v, cmp_w2_v):
    s = x_sample.shape[0]
    n_pages = page_table.shape[1]
    past = n_pages * PAGE_SIZE
    assert past % SEL_BLOCK == 0 and x_sample.shape[1] == 1
    kv, u, gm, qg = _inproj_sample(x_sample.reshape(s, D_MODEL), norm_mix_pre, w_row, w_t)
    q = qg[:, 0:D_Q].reshape(s, N_KV, HPG, HEAD_DIM) * (HEAD_DIM ** -0.5)
    gn = qg[:, D_Q:].reshape(s, N_KV, 16)[:, :, 0:12].reshape(s, N_KV, HPG, 3)
    kvc_new, kvs_new, kvw_new = kv[:, 0:512], kv[:, 512:1024], kv[:, 1024:1536]

    y = _cmp_y_sample(cmp_t, page_table, wcmp)
    kc, vct = _cmp_combine(y, cmp_pe_k, cmp_w1_k, cmp_w2_k, cmp_pe_v, cmp_w1_v, cmp_w2_v)
    ncp = kc.shape[2]
    nc_valid = (past + 1) // CMP_STRIDE - 1
    nsel = -(-(past + 1) // SEL_BLOCK)
    jp = -(-nsel // 8) * 8
    cur = past // SEL_BLOCK
    k_sel = min(N_SELECT, nsel)
    assert ncp >= nc_valid and (nc_valid - 1) * CMP_STRIDE + CMP_BLOCK - 1 <= past

    qb16 = q.astype(BF16)
    qbd = jnp.zeros((s, N_KV, LANES, LANES), BF16)
    gidx = jnp.arange(N_KV)
    lane_gh = gidx[:, None] + jnp.arange(HPG)[None, :] * N_KV
    qbd = qbd.at[:, gidx[:, None, None], jnp.arange(HEAD_DIM)[None, None, :], lane_gh[:, :, None]].set(qb16)
    pad_rows = lambda a: jnp.pad(a, [(0, 0)] * (a.ndim - 2) + [(0, 8 - HPG), (0, 0)])
    qrow = pad_rows(qb16)

    a = jnp.asarray(np.pad(_sel_matrix(nsel, nc_valid, ncp), ((0, jp - nsel), (0, 0))), BF16)
    oc_t, isel = pl.pallas_call(
        functools.partial(_nsa_sample_cmp_kernel, nc_valid), grid=(s,),
        in_specs=[pl.BlockSpec((1, N_KV, LANES, LANES), lambda b: (b, 0, 0, 0)),
                  pl.BlockSpec((1, N_KV, ncp, LANES), lambda b: (b, 0, 0, 0)),
                  pl.BlockSpec((1, N_KV, HEAD_DIM, ncp), lambda b: (b, 0, 0, 0)),
                  pl.BlockSpec(a.shape, lambda b: (0, 0))],
        out_specs=(pl.BlockSpec((1, N_KV, HEAD_DIM, LANES), lambda b: (b, 0, 0, 0)),
                   pl.BlockSpec((1, jp, LANES), lambda b: (b, 0, 0))),
        out_shape=(jax.ShapeDtypeStruct((s, N_KV, HEAD_DIM, LANES), F32), jax.ShapeDtypeStruct((s, jp, LANES), F32)),
        compiler_params=_cparams(("parallel",)), name="nsa_sample_cmp",
    )(qbd, kc, vct, a)

    assert s * N_KV == LANES
    isel_t = isel[:, :, 0:N_KV].transpose(1, 0, 2).reshape(jp, s * N_KV)
    tri = jnp.asarray(np.tril(np.ones((jp, jp), np.float32)), BF16)
    idx = pl.pallas_call(
        functools.partial(_topk_sample_kernel, nsel, cur, k_sel), grid=(1,),
        in_specs=[pl.BlockSpec((jp, LANES), lambda i: (0, 0)), pl.BlockSpec((jp, jp), lambda i: (0, 0))],
        out_specs=pl.BlockSpec((k_sel, LANES), lambda i: (0, 0)),
        out_shape=jax.ShapeDtypeStruct((k_sel, LANES), jnp.int32),
        compiler_params=_cparams(("arbitrary",)), name="topk_sample",
    )(isel_t, tri)
    n_cache = k_sel - 1
    blk = idx[0:n_cache].T.reshape(s, N_KV, n_cache)
    page = jnp.take_along_axis(page_table, (blk // 2).reshape(s, -1), axis=1).reshape(s, N_KV, n_cache)
    tbl = (page * 2 + blk % 2).astype(jnp.int32).reshape(s * N_KV * n_cache)

    oc_g = oc_t[:, gidx[:, None, None], jnp.arange(HEAD_DIM)[None, None, :], lane_gh[:, :, None]]
    oc_row = pad_rows(oc_g)
    gate_in = pad_rows(jnp.broadcast_to(gn.transpose(0, 1, 3, 2)[..., None], (s, N_KV, 3, HPG, HEAD_DIM)))
    wrows = win_t.shape[4]

    def blk_spec(g, r):
        return pl.BlockSpec((1, 2, 1, HEAD_DIM, PAGE_SIZE),
                            lambda b, t: (t[(b * N_KV + g) * n_cache + r] // 2, 0, g, 0, 0))

    per_b = lambda shape: pl.BlockSpec((1,) + shape, lambda b, t: (b,) + (0,) * len(shape))
    win_shape = (2, N_KV, HEAD_DIM, wrows)
    n_blk = N_KV * n_cache
    grid_spec = pltpu.PrefetchScalarGridSpec(
        num_scalar_prefetch=1, grid=(s,),
        in_specs=[per_b((N_KV, 8, HEAD_DIM))] + [blk_spec(g, r) for g in range(N_KV) for r in range(n_cache)]
                 + [per_b((1, 512)), per_b(win_shape), per_b((1, 512)), per_b((2 * N_KV, HEAD_DIM, 1)),
                    per_b((N_KV, 8, HEAD_DIM)), per_b((N_KV, 3, 8, HEAD_DIM))],
        out_specs=(per_b((N_KV, 8, HEAD_DIM)), per_b(win_shape)),
    )
    o_row, win_out = pl.pallas_call(
        functools.partial(_nsa_sample_attn_kernel, n_cache), grid_spec=grid_spec,
        out_shape=(jax.ShapeDtypeStruct((s, N_KV, 8, HEAD_DIM), F32), jax.ShapeDtypeStruct((s,) + win_shape, F32)),
        compiler_params=_cparams(("parallel",)), name="nsa_sample_attn",
    )(tbl, qrow, *([slc_t] * n_blk), kvs_new.reshape(s, 1, 512), win_t,
      kvw_new.reshape(s, 1, 512), kvw_new.reshape(s, 2 * N_KV, HEAD_DIM, 1), oc_row, gate_in)
    o = o_row[:, :, 0:HPG, :]
    return o.reshape(s, D_Q).astype(BF16), kvc_new, kvs_new, win_out, u, gm


def prompt_attention(x_prompt, norm_mix_pre, w_row, w_t, wcmp, cmp_pe_k, cmp_w1_k, cmp_w2_k, cmp_pe_v, cmp_w1_v, cmp_w2_v):
    bsz, seq_len, _ = x_prompt.shape
    assert seq_len % (2 * SWEEP) == 0 and WIN_KEYS <= seq_len <= SEL_BLOCK * HEAD_DIM
    n = bsz * seq_len
    (u, gm, kas, kaw, slabs, qt, vts, vtw, gnt, kvct, kvst, kvwt) = _inproj_prompt(
        x_prompt.reshape(n, D_MODEL), norm_mix_pre, w_row, w_t, bsz, seq_len)
    y = _cmp_y_prompt(slabs, bsz, seq_len, wcmp)
    kc, vct = _cmp_combine(y, cmp_pe_k, cmp_w1_k, cmp_w2_k, cmp_pe_v, cmp_w1_v, cmp_w2_v)
    o = _nsa_prompt(qt, gnt, kc, vct, kas, vts, kaw, vtw, bsz, seq_len)
    return o, kvct, kvst, kvwt, u, gm


def kernel(x_prompt, x_sample, cache_kv_cmp, cache_kv_slc, state_kv_win, state_s5, page_table, norm_mix_pre, norm_mix_post, norm_mlp_pre, norm_mlp_post, w_in, cmp_pe_k, cmp_w1_k, cmp_w2_k, cmp_pe_v, cmp_w1_v, cmp_w2_v, s5_a_re, s5_a_im, s5_log_dt, s5_b_re, s5_b_im, s5_c_re, s5_c_im, s5_d, s5_w_glu, s5_b_glu, w_branch_nsa, w_branch_s5, w_out, w_mlp_up, w_mlp_down):
    bsz, seq_len, _ = x_prompt.shape
    s = x_sample.shape[0]
    w_row, w_t = _inproj_weights(w_in)
    wcmp = _cmp_weights(cmp_w1_k, cmp_w1_v)
    cmp_w = (cmp_pe_k, cmp_w1_k, cmp_w2_k, cmp_pe_v, cmp_w1_v, cmp_w2_v)
    s5_ops = _s5_prep(s5_a_re, s5_a_im, s5_log_dt, s5_b_re, s5_b_im, s5_c_re, s5_c_im)
    mlp_w = _merge_weights(s5_w_glu, s5_b_glu, w_branch_nsa, w_branch_s5, w_out, w_mlp_up, w_mlp_down,
                           norm_mix_post, norm_mlp_pre, norm_mlp_post)

    o_p, kvc_p, kvs_p, kvw_p, u_p, gm_p = prompt_attention(x_prompt, norm_mix_pre, w_row, w_t, wcmp, *cmp_w)
    ys5_p, s5_p = _s5_prompt(u_p, s5_ops, s5_d, bsz, seq_len)
    y_p = _merge_mlp(x_prompt.reshape(bsz * seq_len, D_MODEL), o_p, ys5_p, gm_p, mlp_w)

    feature_major = lambda c: jnp.transpose(c, (0, 2, 3, 4, 1))
    o_s, kvc_s, kvs_s, win_t, u_s, gm_s = sample_attention(
        x_sample, feature_major(cache_kv_cmp), feature_major(cache_kv_slc), feature_major(state_kv_win),
        page_table, norm_mix_pre, w_row, w_t, wcmp, *cmp_w)
    win_s = jnp.transpose(win_t, (0, 4, 1, 2, 3))
    ys5_s, s5_s = _s5_sample(u_s, state_s5, s5_ops, s5_d)
    ys5_s = ys5_s.reshape(s, S5_NOCT, LANES).transpose(1, 0, 2)
    y_s = _merge_mlp(x_sample.reshape(s, D_MODEL), o_s, ys5_s, gm_s, mlp_w)

    kv5 = lambda a, b, t: a.reshape(b, t, 2, N_KV, HEAD_DIM)
    token_major = lambda a: jnp.transpose(a.reshape(bsz, 2, N_KV, HEAD_DIM, -1), (0, 4, 1, 2, 3))
    win_rows = min(WINDOW, seq_len)
    win_p = token_major(kvw_p[:, :, seq_len - win_rows:])
    if win_rows < WINDOW:
        win_p = jnp.pad(win_p, ((0, 0), (WINDOW - win_rows, 0), (0, 0), (0, 0), (0, 0)))
    return (y_p.reshape(bsz, seq_len, D_MODEL), y_s.reshape(s, 1, D_MODEL),
            token_major(kvc_p), token_major(kvs_p), win_p, s5_p.astype(x_prompt.dtype),
            kv5(kvc_s, s, 1), kv5(kvs_s, s, 1), kv5(win_s, s, state_kv_win.shape[1]), s5_s.astype(state_s5.dtype))
```

```python
import functools
import math

import numpy as np
import jax
import jax.numpy as jnp
from jax import lax
from jax.experimental import pallas as pl
from jax.experimental.pallas import tpu as pltpu

F32 = jnp.float32
BF16 = jnp.bfloat16

D_MODEL = 1024
HEAD_DIM = 64
N_HEADS = 16
N_KV = 4
HPG = 4
CMP_STRIDE = 16
CMP_BLOCK = 32
SEL_BLOCK = 64
N_SELECT = 16
WINDOW = 512
Q_BLOCK = 128
S5_WIDTH = 512
S5_CH = 16
S5_GROUPS = 32
S5_STATE = 64
D_FF = 4096
D_Q = 1024
D_KV = 256
PAGE_SIZE = 128
EPS = 1e-6
NEG = -1e30
FORCE = 1e4
LANES = 128
VMEM_LIMIT = 56 * 1024 * 1024
SWEEP = 512
WIN_KEYS = WINDOW + Q_BLOCK
Q_SCALE_LOG2 = HEAD_DIM ** -0.5 * math.log2(math.e)


def _cparams(sem):
    return pltpu.CompilerParams(dimension_semantics=sem, vmem_limit_bytes=VMEM_LIMIT)


def _gelu(x):
    return 0.5 * x * (1.0 + jnp.tanh(math.sqrt(2.0 / math.pi) * (x + 0.044715 * (x * x * x))))


def _sigmoid(x):
    return 1.0 / (1.0 + jnp.exp(-x))


def _rms(x, g):
    ms = jnp.mean(x * x, axis=-1, keepdims=True)
    return (x * lax.rsqrt(ms + EPS)) * g


def _split3(x):
    hi = x.astype(BF16)
    r1 = x - hi.astype(F32)
    mid = r1.astype(BF16)
    lo = (r1 - mid.astype(F32)).astype(BF16)
    return hi, mid, lo


WT_Q, WT_GN, WT_KV = 0, D_Q, D_Q + 64
WT_ROWS = WT_KV + 6 * D_KV


def _inproj_prompt_kernel(seq_len, x_ref, g_ref, wr_ref, wt_ref,
                          u_ref, gm_ref, kas_ref, kaw_ref, slab_ref,
                          qt_ref, vts_ref, vtw_ref, gnt_ref, kvct_ref, kvst_ref, kvwt_ref):
    tm = x_ref.shape[0]
    hb = _rms(x_ref[...], g_ref[...]).astype(BF16)

    def rowdot(lo, hi):
        return jnp.dot(hb, wr_ref[:, lo:hi], preferred_element_type=F32)

    zc = rowdot(0, 512)
    for combo in range(4):
        slab_ref[combo] = zc[:, combo * LANES:(combo + 1) * LANES]
    zu = rowdot(512, 1024)
    for m in range(4):
        u_ref[m] = zu[:, m * LANES:(m + 1) * LANES]
    gm_ref[...] = rowdot(1024, 3072)
    row = pl.program_id(0) * tm + lax.broadcasted_iota(jnp.int32, (tm, LANES), 0)
    blk = lax.rem(row, seq_len) // SEL_BLOCK
    lane = lax.broadcasted_iota(jnp.int32, (tm, LANES), 1)
    onehot = jnp.where(lane - HEAD_DIM == blk, 1.0, 0.0)
    zs = rowdot(3072, 3584)
    zw = rowdot(3584, 4096)
    for g in range(N_KV):
        kas_ref[g] = (zs[:, g * LANES:(g + 1) * LANES] + onehot).astype(BF16)
        kaw_ref[g] = zw[:, g * LANES:(g + 1) * LANES].astype(BF16)
    zt = lax.dot_general(wt_ref[...], hb, (((1,), (1,)), ((), ())), preferred_element_type=F32)
    kv0 = WT_KV
    kvct_ref[0] = zt[kv0:kv0 + 512]
    kvst_ref[0] = zt[kv0 + 512:kv0 + 1024]
    kvwt_ref[0] = zt[kv0 + 1024:kv0 + 1536]
    for c in range(tm // LANES):
        sl = slice(c * LANES, (c + 1) * LANES)
        qt_ref[c] = (zt[WT_Q:WT_Q + D_Q, sl] * Q_SCALE_LOG2).astype(BF16)
        gnt_ref[c] = zt[WT_GN:WT_GN + 64, sl]
        vts_ref[c] = zt[kv0 + 768:kv0 + 1024, sl].astype(BF16)
        vtw_ref[c] = zt[kv0 + 1280:kv0 + 1536, sl].astype(BF16)


def _inproj_weights(w_in):
    wq, wkv, wgn, wu, wgm = (w_in[:, :1024], w_in[:, 1024:2560], w_in[:, 2560:2608],
                             w_in[:, 2608:3120], w_in[:, 3120:])
    wkv6 = wkv.reshape(D_MODEL, 3, 2, N_KV, HEAD_DIM)
    zpad = jnp.zeros((D_MODEL, N_KV, HEAD_DIM), F32)
    kaug_s = jnp.concatenate([wkv6[:, 1, 0], zpad], axis=-1).reshape(D_MODEL, N_KV * LANES)
    kaug_w = jnp.concatenate([wkv6[:, 2, 0], zpad], axis=-1).reshape(D_MODEL, N_KV * LANES)
    w_row = jnp.concatenate([wkv[:, 0:512], wu, wgm, kaug_s, kaug_w], axis=1).astype(BF16)
    gn_rows = jnp.pad(wgn.T.reshape(N_KV, HPG * 3, D_MODEL), ((0, 0), (0, 4), (0, 0))).reshape(64, D_MODEL)
    w_t = jnp.concatenate([wq.T, gn_rows, wkv.T], axis=0).astype(BF16)
    return w_row, w_t


def _inproj_prompt(x2, g_pre, w_row, w_t, bsz, seq_len, tm=256):
    n = x2.shape[0]
    nc = n // LANES
    cpt = tm // LANES
    per = seq_len // tm
    row = lambda w: pl.BlockSpec((tm, w), lambda i: (i, 0))
    fmaj = pl.BlockSpec((1, 512, tm), lambda i: (i // per, 0, i % per))
    out_shape = (
        jax.ShapeDtypeStruct((4, n, LANES), F32), jax.ShapeDtypeStruct((n, 2048), F32),
        jax.ShapeDtypeStruct((N_KV, n, LANES), BF16), jax.ShapeDtypeStruct((N_KV, n, LANES), BF16),
        jax.ShapeDtypeStruct((4, n, LANES), F32),
        jax.ShapeDtypeStruct((nc, 1024, LANES), BF16), jax.ShapeDtypeStruct((nc, 256, LANES), BF16),
        jax.ShapeDtypeStruct((nc, 256, LANES), BF16), jax.ShapeDtypeStruct((nc, 64, LANES), F32),
        jax.ShapeDtypeStruct((bsz, 512, seq_len), F32), jax.ShapeDtypeStruct((bsz, 512, seq_len), F32),
        jax.ShapeDtypeStruct((bsz, 512, seq_len), F32),
    )
    out_specs = (
        pl.BlockSpec((4, tm, LANES), lambda i: (0, i, 0)), row(2048),
        pl.BlockSpec((N_KV, tm, LANES), lambda i: (0, i, 0)), pl.BlockSpec((N_KV, tm, LANES), lambda i: (0, i, 0)),
        pl.BlockSpec((4, tm, LANES), lambda i: (0, i, 0)),
        pl.BlockSpec((cpt, 1024, LANES), lambda i: (i, 0, 0)), pl.BlockSpec((cpt, 256, LANES), lambda i: (i, 0, 0)),
        pl.BlockSpec((cpt, 256, LANES), lambda i: (i, 0, 0)), pl.BlockSpec((cpt, 64, LANES), lambda i: (i, 0, 0)),
        fmaj, fmaj, fmaj,
    )
    return pl.pallas_call(
        functools.partial(_inproj_prompt_kernel, seq_len),
        grid=(n // tm,),
        in_specs=[row(D_MODEL), pl.BlockSpec((1, D_MODEL), lambda i: (0, 0)),
                  pl.BlockSpec(w_row.shape, lambda i: (0, 0)), pl.BlockSpec(w_t.shape, lambda i: (0, 0))],
        out_specs=out_specs, out_shape=out_shape,
        compiler_params=_cparams(("parallel",)), name="inproj_prompt",
    )(x2, g_pre.reshape(1, D_MODEL), w_row, w_t)


def _cmp_weights(w1_k, w1_v):
    eye2 = jnp.eye(2, dtype=F32)
    out = []
    for w1 in (w1_k, w1_v):
        w = w1.reshape(2, CMP_STRIDE, HEAD_DIM, HEAD_DIM)
        big = jnp.einsum('fsdh,ij->sidfjh', w, eye2).reshape(CMP_STRIDE * 2 * HEAD_DIM, 2 * 2 * HEAD_DIM)
        out += [big, big]
    return jnp.stack(out).astype(BF16)


def _cmp_y_from_slabs(slab_ref, w_ref, y_ref, pitch=CMP_STRIDE):
    nrows = y_ref.shape[1]
    for combo in range(4):
        xg = jnp.concatenate([slab_ref[combo, pl.ds(s, nrows, stride=pitch), :] for s in range(CMP_STRIDE)],
                             axis=1)
        y_ref[0, :, combo * 256:(combo + 1) * 256] = jnp.dot(xg.astype(BF16), w_ref[combo],
                                                             preferred_element_type=F32)


def _cmp_y_prompt(slabs, bsz, seq_len, wcmp):
    nch = seq_len // CMP_STRIDE
    rb = min(nch, 128)
    per = nch // rb
    return pl.pallas_call(
        _cmp_y_from_slabs,
        grid=(bsz, per),
        in_specs=[pl.BlockSpec((4, rb * CMP_STRIDE, LANES), lambda b, j: (0, b * per + j, 0)),
                  pl.BlockSpec(wcmp.shape, lambda b, j: (0, 0, 0))],
        out_specs=pl.BlockSpec((1, rb, 1024), lambda b, j: (b, j, 0)),
        out_shape=jax.ShapeDtypeStruct((bsz, nch, 1024), F32),
        compiler_params=_cparams(("parallel", "parallel")), name="cmp_y_prompt",
    )(slabs, wcmp)


def _cmp_combine_kernel(y_ref, pe_ref, w1f_ref, w2k_ref, w2vt_ref, kc_ref, vct_ref):
    r = y_ref.shape[1]
    pos = jnp.dot(pe_ref[...], w1f_ref[...], preferred_element_type=F32,
                  precision=lax.Precision.HIGHEST)
    for combo in range(4):
        kv, gp = combo // 2, combo % 2
        first = y_ref[0, :, combo * 256: combo * 256 + LANES]
        second = pltpu.roll(y_ref[0, :, combo * 256 + LANES: combo * 256 + 2 * LANES], r - 1, 0)
        p1 = pos[0:1, kv * HEAD_DIM:(kv + 1) * HEAD_DIM]
        pre = first + second + jnp.concatenate([p1, p1], axis=1)
        act = _gelu(pre).astype(BF16)
        if kv == 0:
            kc = jnp.dot(act, w2k_ref[...], preferred_element_type=F32)
            kc_ref[0, 2 * gp] = kc[:, 0:LANES].astype(BF16)
            kc_ref[0, 2 * gp + 1] = kc[:, LANES:2 * LANES].astype(BF16)
        else:
            vct = lax.dot_general(w2vt_ref[...], act, (((1,), (1,)), ((), ())), preferred_element_type=F32)
            vct_ref[0, 2 * gp] = vct[0:HEAD_DIM].astype(BF16)
            vct_ref[0, 2 * gp + 1] = vct[HEAD_DIM:2 * HEAD_DIM].astype(BF16)


def _cmp_combine(y, pe_k, w1_k, w2_k, pe_v, w1_v, w2_v):
    s, r, _ = y.shape
    pe = jnp.concatenate([pe_k.reshape(1, -1), pe_v.reshape(1, -1)], axis=1)
    pe8 = jnp.pad(pe, ((0, 7), (0, 0)))
    z = jnp.zeros((CMP_BLOCK * HEAD_DIM, HEAD_DIM), F32)
    w1f = jnp.concatenate([jnp.concatenate([w1_k.reshape(-1, HEAD_DIM), z], axis=1),
                           jnp.concatenate([z, w1_v.reshape(-1, HEAD_DIM)], axis=1)], axis=0)
    z64 = jnp.zeros((HEAD_DIM, HEAD_DIM), F32)
    w2k = jnp.concatenate([jnp.concatenate([w2_k, z64, z64, z64], axis=1),
                           jnp.concatenate([z64, z64, w2_k, z64], axis=1)], axis=0).astype(BF16)
    w2vt = jnp.concatenate([jnp.concatenate([w2_v.T, z64], axis=1),
                            jnp.concatenate([z64, w2_v.T], axis=1)], axis=0).astype(BF16)
    full = lambda a: pl.BlockSpec(a.shape, lambda i: (0,) * a.ndim)
    return pl.pallas_call(
        _cmp_combine_kernel,
        grid=(s,),
        in_specs=[pl.BlockSpec((1, r, 1024), lambda i: (i, 0, 0)), full(pe8), full(w1f), full(w2k), full(w2vt)],
        out_specs=(pl.BlockSpec((1, N_KV, r, LANES), lambda i: (i, 0, 0, 0)),
                   pl.BlockSpec((1, N_KV, HEAD_DIM, r), lambda i: (i, 0, 0, 0))),
        out_shape=(jax.ShapeDtypeStruct((s, N_KV, r, LANES), BF16),
                   jax.ShapeDtypeStruct((s, N_KV, HEAD_DIM, r), BF16)),
        compiler_params=_cparams(("parallel",)), name="cmp_combine",
    )(y, pe8, w1f, w2k, w2vt)


def _sel_matrix(nsel, nc, ncp):
    j = np.arange(nsel)
    lo = np.clip((j * SEL_BLOCK - CMP_BLOCK) // CMP_STRIDE + 1, 0, nc)
    hi = np.clip((j * SEL_BLOCK + SEL_BLOCK - 1) // CMP_STRIDE + 1, 0, nc)
    n = np.arange(ncp)
    return ((n[None, :] >= lo[:, None]) & (n[None, :] < hi[:, None])).astype(np.float32)


def _rank_select(score, k_sel):
    nj, nl = score.shape
    sub = 8
    tiles = [score[v * sub:(v + 1) * sub] for v in range(nj // sub)]
    cnts = [jnp.zeros((sub, nl), F32) for _ in tiles]
    jloc = lax.broadcasted_iota(jnp.int32, (sub, nl), 0)
    for i in range(nj):
        bi = jnp.broadcast_to(score[i:i + 1, :], (sub, nl))
        for v, t in enumerate(tiles):
            if v * sub > i:
                inc = jnp.where(bi >= t, 1.0, 0.0)
            elif v * sub + sub - 1 < i:
                inc = jnp.where(bi > t, 1.0, 0.0)
            else:
                inc = jnp.where(jloc > i - v * sub, jnp.where(bi >= t, 1.0, 0.0), jnp.where(bi > t, 1.0, 0.0))
            cnts[v] = cnts[v] + inc
    cnt = jnp.concatenate(cnts, axis=0)
    return jnp.where(cnt < k_sel, 1.0, 0.0)


NSA_TILES_PER_STEP = 4


def _nsa_prompt_kernel(nc_valid, k_sel, qt_ref, gnt_ref, kc_ref, vct_ref, kas_ref, vts_ref, kaw_ref, vtw_ref,
                       a_ref, o_ref, *score_bufs):
    for t in range(NSA_TILES_PER_STEP):
        one = pl.ds(t, 1)
        _nsa_prompt_tile(nc_valid, k_sel, pl.program_id(2) * NSA_TILES_PER_STEP + t,
                         qt_ref.at[one], gnt_ref.at[one], kc_ref, vct_ref, kas_ref, vts_ref, kaw_ref, vtw_ref, a_ref,
                         o_ref.at[pl.ds(t * Q_BLOCK, Q_BLOCK)], score_bufs[2 * t], score_bufs[2 * t + 1])


def _nsa_prompt_tile(nc_valid, k_sel, qb, qt_ref, gnt_ref, kc_ref, vct_ref, kas_ref, vts_ref, kaw_ref, vtw_ref,
                     a_ref, o_ref, sa_ref, sb_ref):
    heads = lambda t: jnp.concatenate([t] * HPG, axis=1)

    def pv_and_sum(vt, p):
        va = jnp.concatenate([vt, jnp.ones((16, vt.shape[1]), BF16)], axis=0)
        out = jnp.dot(va, p.astype(BF16), preferred_element_type=F32)
        return out[0:HEAD_DIM], out[HEAD_DIM:HEAD_DIM + 1]
    qt = qt_ref[0]
    ql = lax.broadcasted_iota(jnp.int32, (1, Q_BLOCK), 1)
    tl2 = lax.broadcasted_iota(jnp.int32, (Q_BLOCK, Q_BLOCK), 0)
    ql2 = lax.broadcasted_iota(jnp.int32, (Q_BLOCK, Q_BLOCK), 1)
    b_diag = jnp.where(tl2 <= ql2, 0.0, NEG)
    b_first = jnp.where(tl2 > ql2, 0.0, NEG)
    zero_q = jnp.zeros((HEAD_DIM, Q_BLOCK), BF16)
    rhs_q = jnp.concatenate(
        [jnp.concatenate([qt[h * HEAD_DIM:(h + 1) * HEAD_DIM], zero_q], axis=0) for h in range(HPG)], axis=1)

    ncp = kc_ref.shape[2]
    assert nc_valid >= ncp - 1
    sc = jnp.dot(kc_ref[0, 0], rhs_q, preferred_element_type=F32)
    edge = lax.shift_right_arithmetic(ql - (CMP_BLOCK - 1), 4)
    nrel = lax.broadcasted_iota(jnp.int32, (ncp, Q_BLOCK), 0) - qb * (Q_BLOCK // CMP_STRIDE)
    sc = sc + heads(jnp.where(nrel <= edge, 0.0, NEG))
    e = jnp.exp2(sc - jnp.max(sc, axis=0, keepdims=True))
    den = jnp.sum(e, axis=0, keepdims=True)
    any_visible = heads(jnp.where(qb * Q_BLOCK + ql >= CMP_BLOCK - 1, 1.0, 0.0))
    p = e * (any_visible / den)
    o_c = jnp.dot(vct_ref[0, 0], p.astype(BF16), preferred_element_type=F32)
    imp = p[:, 0:Q_BLOCK]
    for h in range(1, HPG):
        imp = imp + p[:, h * Q_BLOCK:(h + 1) * Q_BLOCK]
    a = a_ref[...]
    imp_sel = sum(jnp.dot(a, part, preferred_element_type=F32) for part in _split3(imp))

    c0 = jnp.maximum(qb - WINDOW // Q_BLOCK, 0)
    wstart = pl.multiple_of(c0 * Q_BLOCK, Q_BLOCK)
    sw = jnp.dot(kaw_ref[0, pl.ds(wstart, WIN_KEYS), :], rhs_q, preferred_element_type=F32)
    wbias = []
    for i in range(WIN_KEYS // Q_BLOCK):
        d = qb - c0 - i
        wbias.append(jnp.where(d == WINDOW // Q_BLOCK, b_first,
                               jnp.where(d == 0, b_diag, jnp.where(d < 0, NEG, 0.0))))
    sw = sw + heads(jnp.concatenate(wbias, axis=0))
    vtw = jnp.concatenate([vtw_ref[c0 + i] for i in range(WIN_KEYS // Q_BLOCK)], axis=1)
    o_w, lw = pv_and_sum(vtw, jnp.exp2(sw - jnp.max(sw, axis=0, keepdims=True)))
    o_w = o_w / lw

    nsel = imp_sel.shape[0]
    jrow = lax.broadcasted_iota(jnp.int32, (nsel, Q_BLOCK), 0)
    qp1 = qb * Q_BLOCK + lax.broadcasted_iota(jnp.int32, (nsel, Q_BLOCK), 1)
    cur = qp1 // SEL_BLOCK
    forced = (jrow == 0) | (jrow == cur) | (jrow == cur - 1)
    score = jnp.where(jrow <= cur, jnp.where(forced, FORCE, imp_sel), -1.0)
    sel = jnp.where(jrow <= cur, _rank_select(score, k_sel), 0.0)
    if nsel < HEAD_DIM:
        sel = jnp.concatenate([sel, jnp.zeros((HEAD_DIM - nsel, Q_BLOCK), F32)], axis=0)
    mq = (sel - 1.0) * 1e30
    jrow64 = lax.broadcasted_iota(jnp.int32, (HEAD_DIM, Q_BLOCK), 0)
    mq_past = jnp.where(jrow64 >= qb * (Q_BLOCK // SEL_BLOCK), NEG, mq)

    def with_mask(mrows):
        mb = mrows.astype(BF16)
        return jnp.concatenate(
            [jnp.concatenate([qt[h * HEAD_DIM:(h + 1) * HEAD_DIM], mb], axis=0) for h in range(HPG)], axis=1)

    rhs_diag, rhs_past = with_mask(mq), with_mask(mq_past)

    dstart = pl.multiple_of(qb * Q_BLOCK, Q_BLOCK)
    sd = jnp.dot(kas_ref[0, pl.ds(dstart, Q_BLOCK), :], rhs_diag, preferred_element_type=F32) + heads(b_diag)
    m = jnp.max(sd, axis=0, keepdims=True)
    acc, l = pv_and_sum(vts_ref[qb], jnp.exp2(sd - m))

    cps = SWEEP // Q_BLOCK
    n_span = (qb + cps - 1) // cps

    def span_scores(i, buf):
        start = pl.multiple_of(i * SWEEP, SWEEP)
        s = jnp.dot(kas_ref[0, pl.ds(start, SWEEP), :], rhs_past, preferred_element_type=F32)
        buf[...] = s
        return jnp.max(s, axis=0, keepdims=True)

    def span_consume(i, buf, smax, m, l, acc):
        m_new = jnp.maximum(m, smax)
        alpha = jnp.exp2(m - m_new)
        vt = jnp.concatenate([vts_ref[cps * i + k] for k in range(cps)], axis=1)
        pv, psum = pv_and_sum(vt, jnp.exp2(buf[...] - m_new))
        return m_new, l * alpha + psum, acc * alpha + pv

    n_pair = (n_span + 1) // 2

    def pair(k, carry, prefetch):
        m, l, acc, smax0 = carry
        smax1 = span_scores(2 * k + 1, sb_ref)
        m, l, acc = span_consume(2 * k, sa_ref, smax0, m, l, acc)
        smax0 = span_scores(2 * k + 2, sa_ref) if prefetch else smax0
        m, l, acc = span_consume(2 * k + 1, sb_ref, smax1, m, l, acc)
        return m, l, acc, smax0

    carry = lax.fori_loop(0, n_pair - 1, lambda k, c: pair(k, c, True), (m, l, acc, span_scores(0, sa_ref)))
    m, l, acc, _ = pair(jnp.maximum(n_pair - 1, 0), carry, False)
    o_s = acc / l

    gate = _sigmoid(gnt_ref[0])
    outs = []
    for h in range(HPG):
        sl = slice(h * Q_BLOCK, (h + 1) * Q_BLOCK)
        outs.append(gate[3 * h:3 * h + 1] * o_c[:, sl] + gate[3 * h + 1:3 * h + 2] * o_s[:, sl]
                    + gate[3 * h + 2:3 * h + 3] * o_w[:, sl])
    o_t = jnp.concatenate(outs, axis=0)
    o_ref[...] = o_t.astype(BF16).T


def _nsa_prompt(qt, gnt, kc, vct, kas, vts, kaw, vtw, bsz, seq_len):
    nq = seq_len // Q_BLOCK
    nsel = seq_len // SEL_BLOCK
    nc_valid = seq_len // CMP_STRIDE - 1
    ncp = kc.shape[2]
    k_sel = min(N_SELECT, nsel)
    a = jnp.asarray(_sel_matrix(nsel, nc_valid, ncp), BF16)
    n = bsz * seq_len
    tps = NSA_TILES_PER_STEP
    steps = nq // tps
    score_buf = pltpu.VMEM((SWEEP, HPG * Q_BLOCK), F32)
    return pl.pallas_call(
        functools.partial(_nsa_prompt_kernel, nc_valid, k_sel),
        grid=(bsz, N_KV, steps),
        in_specs=[
            pl.BlockSpec((tps, HPG * HEAD_DIM, LANES), lambda b, g, i: (b * steps + i, g, 0)),
            pl.BlockSpec((tps, 16, LANES), lambda b, g, i: (b * steps + i, g, 0)),
            pl.BlockSpec((1, 1, ncp, LANES), lambda b, g, i: (b, g, 0, 0)),
            pl.BlockSpec((1, 1, HEAD_DIM, ncp), lambda b, g, i: (b, g, 0, 0)),
            pl.BlockSpec((1, seq_len, LANES), lambda b, g, i: (g, b, 0)),
            pl.BlockSpec((nq, HEAD_DIM, LANES), lambda b, g, i: (b, g, 0)),
            pl.BlockSpec((1, seq_len, LANES), lambda b, g, i: (g, b, 0)),
            pl.BlockSpec((nq, HEAD_DIM, LANES), lambda b, g, i: (b, g, 0)),
            pl.BlockSpec(a.shape, lambda b, g, i: (0, 0)),
        ],
        out_specs=pl.BlockSpec((tps * Q_BLOCK, HPG * HEAD_DIM), lambda b, g, i: (b * steps + i, g)),
        out_shape=jax.ShapeDtypeStruct((n, D_Q), BF16),
        scratch_shapes=[score_buf] * (2 * tps),
        compiler_params=_cparams(("parallel", "parallel", "arbitrary")), name="nsa_prompt",
    )(qt, gnt, kc, vct, kas, vts, kaw, vtw, a)


S5_L = 16
S5_W = S5_L * S5_CH
S5_P = 2 * S5_STATE
S5_OCT = LANES // S5_CH
S5_NOCT = S5_GROUPS // S5_OCT
S5_OW = S5_L * LANES
S5_OP = S5_OCT * S5_P


def _s5_prep_kernel(are_ref, aim_ref, ldt_ref, bre_ref, bim_ref, cre_ref, cim_ref, spread_ref,
                    tg_ref, sg_ref, ogt_ref, misc_ref, bs_ref, oct_ref):
    g8 = lax.rem(pl.program_id(0), S5_OCT)
    are, aim = are_ref[0], aim_ref[0]
    dt = jnp.exp(ldt_ref[0])
    mag = jnp.exp(are * dt)
    ar, ai = mag * jnp.cos(aim * dt), mag * jnp.sin(aim * dt)
    den = are * are + aim * aim
    fr = ((ar - 1.0) * are + ai * aim) / den
    fi = (ai * are - (ar - 1.0) * aim) / den
    bre, bim = bre_ref[0], bim_ref[0]
    cre, cim = cre_ref[0], cim_ref[0]
    br, bi = fr * bre - fi * bim, fr * bim + fi * bre
    lo16 = lax.broadcasted_iota(jnp.int32, (S5_CH, S5_P), 1) < S5_STATE
    lo1 = lax.broadcasted_iota(jnp.int32, (1, S5_P), 1) < S5_STATE
    pr, pi = [jnp.ones_like(ar)], [jnp.zeros_like(ar)]
    for _ in range(S5_L):
        pr.append(pr[-1] * ar - pi[-1] * ai)
        pi.append(pr[-2] * ai + pi[-1] * ar)
    cpr = [cre * pr[k] - cim * pi[k] for k in range(S5_L + 1)]
    cpi = [cre * pi[k] + cim * pr[k] for k in range(S5_L + 1)]
    rpack = jnp.concatenate([jnp.where(lo16, cpr[k], cpi[k]) for k in range(S5_L)], axis=0)
    bpack = jnp.where(lo16, br, -bi)
    krow = lax.dot_general(bpack, rpack, (((1,), (1,)), ((), ())), preferred_element_type=F32,
                           precision=lax.Precision.HIGHEST)
    wide = jnp.dot(krow.astype(BF16), spread_ref[...], preferred_element_type=F32)
    wide = pltpu.roll(wide, g8 * S5_CH, 1)
    lane = lax.broadcasted_iota(jnp.int32, (S5_CH, S5_OW), 1)
    zpad = jnp.zeros((S5_CH, S5_OP - S5_P), F32)
    place = lambda blk: pltpu.roll(jnp.concatenate([blk, zpad], axis=1), g8 * S5_P, 1).astype(BF16)
    for j in range(S5_L):
        rows = pl.ds(pl.multiple_of(j * LANES + g8 * S5_CH, S5_CH), S5_CH)
        shifted = wide if j == 0 else pltpu.roll(wide, j * LANES, 1)
        tg_ref[0, rows, :] = jnp.where(lane >= j * LANES, shifted, 0.0).astype(BF16)
        k = S5_L - 1 - j
        sblk = jnp.where(lo16, pr[k] * br - pi[k] * bi, pr[k] * bi + pi[k] * br)
        sg_ref[0, rows, :] = place(sblk)
        if j == S5_L - 1:
            bs_ref[0] = sblk
        ogt_ref[0, rows, :] = place(jnp.where(lo16, cpr[j + 1], -cpi[j + 1]))
    oct_ref[0] = jnp.where(lo16, cre, -cim)
    misc_ref[0] = jnp.concatenate([
        pr[S5_L], jnp.where(lo1, -pi[S5_L], pi[S5_L]), ar, jnp.where(lo1, -ai, ai),
        jnp.zeros((4, S5_P), F32)], axis=0)


def _s5_prep(a_re, a_im, log_dt, b_re, b_im, c_re, c_im):
    g = S5_GROUPS
    dup = lambda a: jnp.concatenate([a, a], axis=-1)
    are, aim = dup(a_re).reshape(g, 1, S5_P), dup(a_im).reshape(g, 1, S5_P)
    ldt = jnp.broadcast_to(log_dt.reshape(g, 1, 1), (g, 1, S5_P))
    bre, bim = dup(jnp.swapaxes(b_re, 1, 2)), dup(jnp.swapaxes(b_im, 1, 2))
    cre, cim = dup(c_re), dup(c_im)
    v1 = pl.BlockSpec((1, 1, S5_P), lambda i: (i, 0, 0))
    v16 = pl.BlockSpec((1, S5_CH, S5_P), lambda i: (i, 0, 0))
    lag_co = np.arange(S5_W)
    spread = np.zeros((S5_W, S5_OW), np.float32)
    spread[lag_co, (lag_co // S5_CH) * LANES + lag_co % S5_CH] = 1.0
    spread = jnp.asarray(spread, BF16)
    octet = lambda w: pl.BlockSpec((1, S5_OW, w), lambda i: (i // S5_OCT, 0, 0))
    return pl.pallas_call(
        _s5_prep_kernel, grid=(g,),
        in_specs=[v1, v1, v1, v16, v16, v16, v16, pl.BlockSpec(spread.shape, lambda i: (0, 0))],
        out_specs=(octet(S5_OW), octet(S5_OP), octet(S5_OP), pl.BlockSpec((1, 8, S5_P), lambda i: (i, 0, 0)),
                   v16, v16),
        out_shape=(jax.ShapeDtypeStruct((S5_NOCT, S5_OW, S5_OW), BF16), jax.ShapeDtypeStruct((S5_NOCT, S5_OW, S5_OP), BF16),
                   jax.ShapeDtypeStruct((S5_NOCT, S5_OW, S5_OP), BF16), jax.ShapeDtypeStruct((g, 8, S5_P), F32),
                   jax.ShapeDtypeStruct((g, S5_CH, S5_P), F32), jax.ShapeDtypeStruct((g, S5_CH, S5_P), F32)),
        compiler_params=_cparams(("arbitrary",)), name="s5_prep",
    )(are, aim, ldt, bre, bim, cre, cim, spread)


def _s5_chunks(u_ref):
    nch = u_ref.shape[1] // S5_L
    return jnp.concatenate([u_ref[0, pl.ds(j, nch, stride=S5_L), :] for j in range(S5_L)], axis=1)


def _s5_sum_kernel(u_ref, sg_ref, s_ref):
    s_ref[0] = jnp.dot(_s5_chunks(u_ref).astype(BF16), sg_ref[0], preferred_element_type=F32)


def _s5_scan_kernel(s_ref, a1_ref, a2_ref, h_ref, last_ref, carry):
    @pl.when(pl.program_id(0) == 0)
    def _():
        carry[...] = jnp.zeros_like(carry)

    a1, a2 = a1_ref[...], a2_ref[...]

    def step(c, h):
        h_ref[c] = h
        return a1 * h + a2 * pltpu.roll(h, S5_STATE, 1) + s_ref[c]

    h = lax.fori_loop(0, s_ref.shape[0], step, carry[...])
    carry[...] = h
    last_ref[...] = h


def _s5_out_kernel(u_ref, h_ref, tg_ref, ogt_ref, d_ref, y_ref):
    u = _s5_chunks(u_ref)
    y = jnp.dot(u.astype(BF16), tg_ref[0], preferred_element_type=F32)
    y = y + lax.dot_general(h_ref[0].astype(BF16), ogt_ref[0], (((1,), (1,)), ((), ())), preferred_element_type=F32)
    y = y + d_ref[0] * u
    nch = y.shape[0]
    for k in range(S5_L):
        y_ref[0, pl.ds(k, nch, stride=S5_L), :] = y[:, k * LANES:(k + 1) * LANES]


def _s5_prompt(uslab, ops, s5_d, bsz, seq_len):
    tg, sg, ogt, misc = ops[0], ops[1], ops[2], ops[3]
    g, nch = S5_GROUPS, seq_len // S5_L
    rows = bsz * nch
    slab = pl.BlockSpec((1, seq_len, LANES), lambda m, b: (m, b, 0))
    per_oct = lambda r, c: pl.BlockSpec((1, r, c), lambda m, b: (m, 0, 0))
    per_row = lambda c: pl.BlockSpec((1, nch, c), lambda m, b: (m, b, 0))
    ssum = pl.pallas_call(
        _s5_sum_kernel, grid=(S5_NOCT, bsz), in_specs=[slab, per_oct(S5_OW, S5_OP)],
        out_specs=per_row(S5_OP), out_shape=jax.ShapeDtypeStruct((S5_NOCT, rows, S5_OP), F32),
        compiler_params=_cparams(("parallel", "parallel")), name="s5_sum",
    )(uslab, sg)
    s_cm = (ssum.reshape(S5_NOCT, bsz, nch, S5_OCT, S5_P).transpose(2, 1, 0, 3, 4).reshape(nch, bsz * g, S5_P))
    a1 = jnp.tile(misc[:, 0, :], (bsz, 1))
    a2 = jnp.tile(misc[:, 1, :], (bsz, 1))
    cb = min(nch, 32)
    hs, last = pl.pallas_call(
        _s5_scan_kernel, grid=(nch // cb,),
        in_specs=[pl.BlockSpec((cb, bsz * g, S5_P), lambda i: (i, 0, 0)),
                  pl.BlockSpec((bsz * g, S5_P), lambda i: (0, 0)), pl.BlockSpec((bsz * g, S5_P), lambda i: (0, 0))],
        out_specs=(pl.BlockSpec((cb, bsz * g, S5_P), lambda i: (i, 0, 0)), pl.BlockSpec((bsz * g, S5_P), lambda i: (0, 0))),
        out_shape=(jax.ShapeDtypeStruct((nch, bsz * g, S5_P), F32), jax.ShapeDtypeStruct((bsz * g, S5_P), F32)),
        scratch_shapes=[pltpu.VMEM((bsz * g, S5_P), F32)],
        compiler_params=_cparams(("arbitrary",)), name="s5_scan",
    )(s_cm, a1, a2)
    h_oct = (hs.reshape(nch, bsz, S5_NOCT, S5_OCT, S5_P).transpose(2, 1, 0, 3, 4).reshape(S5_NOCT, rows, S5_OP))
    dvec = jnp.tile(s5_d.reshape(S5_NOCT, 1, LANES), (1, 1, S5_L))
    yslab = pl.pallas_call(
        _s5_out_kernel, grid=(S5_NOCT, bsz),
        in_specs=[slab, per_row(S5_OP), per_oct(S5_OW, S5_OW), per_oct(S5_OW, S5_OP), per_oct(1, S5_OW)],
        out_specs=slab, out_shape=jax.ShapeDtypeStruct((S5_NOCT, bsz * seq_len, LANES), F32),
        compiler_params=_cparams(("parallel", "parallel")), name="s5_out",
    )(uslab, h_oct, tg, ogt, dvec)
    state = last.reshape(bsz, g, 2, S5_STATE).transpose(0, 1, 3, 2)
    return yslab, state


def _s5_sample_kernel(u_ref, h0_ref, bs_ref, oct_ref, misc_ref, d_ref, y_ref, h1_ref):
    u, h0 = u_ref[0], h0_ref[0]
    hi = lax.Precision.HIGHEST
    bu = jnp.dot(u, bs_ref[0], preferred_element_type=F32, precision=hi)
    h1 = misc_ref[0, 2:3] * h0 + misc_ref[0, 3:4] * pltpu.roll(h0, S5_STATE, 1) + bu
    h1_ref[0] = h1
    y = lax.dot_general(h1, oct_ref[0], (((1,), (1,)), ((), ())), preferred_element_type=F32, precision=hi)
    y_ref[0] = y + d_ref[0] * u


def _s5_sample(u, state, ops, s5_d):
    misc, bs, oct_ = ops[3], ops[4], ops[5]
    s, g = u.shape[0], S5_GROUPS
    ug = u.reshape(s, g, S5_CH).transpose(1, 0, 2)
    h0 = state.astype(F32).transpose(1, 0, 3, 2).reshape(g, s, S5_P)
    gspec = lambda r, c: pl.BlockSpec((1, r, c), lambda i: (i, 0, 0))
    y, h1 = pl.pallas_call(
        _s5_sample_kernel, grid=(g,),
        in_specs=[gspec(s, S5_CH), gspec(s, S5_P), gspec(S5_CH, S5_P), gspec(S5_CH, S5_P), gspec(8, S5_P), gspec(1, S5_CH)],
        out_specs=(gspec(s, S5_CH), gspec(s, S5_P)),
        out_shape=(jax.ShapeDtypeStruct((g, s, S5_CH), F32), jax.ShapeDtypeStruct((g, s, S5_P), F32)),
        compiler_params=_cparams(("parallel",)), name="s5_sample",
    )(ug, h0, bs, oct_, misc, s5_d.reshape(g, 1, S5_CH))
    return (y.transpose(1, 0, 2).reshape(s, S5_WIDTH),
            h1.reshape(g, s, 2, S5_STATE).transpose(1, 0, 3, 2))


def _merge_mlp_kernel(x_ref, o_ref, ys_ref, gm_ref, wglu_ref, bglu_ref, wbn_ref, wbs_ref, wout_ref, wup_ref, wdn_ref,
                      npost_ref, nmpre_ref, nmpost_ref, out_ref):
    dot = lambda a, w_ref: jnp.dot(a.astype(BF16), w_ref[...], preferred_element_type=F32)
    z = _gelu(jnp.concatenate([ys_ref[m] for m in range(S5_NOCT)], axis=1))
    o_s5 = z * _sigmoid(dot(z, wglu_ref) + bglu_ref[...])
    merged = (_sigmoid(gm_ref[:, 0:D_MODEL]) * dot(o_ref[...], wbn_ref)
              + _sigmoid(gm_ref[:, D_MODEL:2 * D_MODEL]) * dot(o_s5, wbs_ref))
    x1 = x_ref[...] + _rms(dot(merged, wout_ref), npost_ref[...])
    hm = _rms(x1, nmpre_ref[...])
    up = jnp.maximum(dot(hm, wup_ref), 0.0)
    f = dot(up * up, wdn_ref)
    out_ref[...] = x1 + _rms(f, nmpost_ref[...])


def _merge_mlp(x2, o_nsa, y_s5, gm, wts, tm=256):
    n = x2.shape[0]
    tm = min(tm, n)
    row = lambda w: pl.BlockSpec((tm, w), lambda i: (i, 0))
    const = lambda a: pl.BlockSpec(a.shape, lambda i: (0, 0), pipeline_mode=pl.Buffered(1))
    return pl.pallas_call(
        _merge_mlp_kernel, grid=(n // tm,),
        in_specs=[row(D_MODEL), row(D_Q), pl.BlockSpec((S5_NOCT, tm, LANES), lambda i: (0, i, 0)), row(2 * D_MODEL)]
                 + [const(a) for a in wts],
        out_specs=row(D_MODEL), out_shape=jax.ShapeDtypeStruct((n, D_MODEL), F32),
        compiler_params=_cparams(("parallel",)), name="merge_mlp",
    )(x2, o_nsa, y_s5, gm, *wts)


def _merge_weights(s5_w_glu, s5_b_glu, w_branch_nsa, w_branch_s5, w_out, w_mlp_up, w_mlp_down,
                   norm_mix_post, norm_mlp_pre, norm_mlp_post):
    r = lambda v: v.reshape(1, -1).astype(F32)
    b = lambda w: w.astype(BF16)
    return (b(s5_w_glu), r(s5_b_glu), b(w_branch_nsa), b(w_branch_s5), b(w_out), b(w_mlp_up), b(w_mlp_down),
            r(norm_mix_post), r(norm_mlp_pre), r(norm_mlp_post))


def _inproj_sample_kernel(x_ref, g_ref, wr_ref, wt_ref, kv_ref, u_ref, gm_ref, qg_ref):
    hb = _rms(x_ref[...], g_ref[...]).astype(BF16)
    u_ref[...] = jnp.dot(hb, wr_ref[:, 512:1024], preferred_element_type=F32)
    gm_ref[...] = jnp.dot(hb, wr_ref[:, 1024:3072], preferred_element_type=F32)
    z = lax.dot_general(hb, wt_ref[...], (((1,), (1,)), ((), ())), preferred_element_type=F32)
    qg_ref[...] = z[:, 0:WT_KV]
    kv_ref[...] = z[:, WT_KV:WT_ROWS]


def _inproj_sample(x2, g_pre, w_row, w_t):
    s = x2.shape[0]
    full = lambda a: pl.BlockSpec(a.shape, lambda i: (0, 0))
    o = lambda w: pl.BlockSpec((s, w), lambda i: (0, 0))
    return pl.pallas_call(
        _inproj_sample_kernel, grid=(1,),
        in_specs=[o(D_MODEL), pl.BlockSpec((1, D_MODEL), lambda i: (0, 0)), full(w_row), full(w_t)],
        out_specs=(o(1536), o(512), o(2048), o(WT_KV)),
        out_shape=(jax.ShapeDtypeStruct((s, 1536), F32), jax.ShapeDtypeStruct((s, 512), F32),
                   jax.ShapeDtypeStruct((s, 2048), F32), jax.ShapeDtypeStruct((s, WT_KV), F32)),
        compiler_params=_cparams(("arbitrary",)), name="inproj_sample",
    )(x2, g_pre.reshape(1, D_MODEL), w_row, w_t)


CMP_PAGES_PER_STEP = 32
CMP_SLAB_PITCH = CMP_STRIDE + 1


def _cmp_y_sample_kernel(pps, pt_ref, *refs):
    x_refs, w_ref, y_ref, slab = refs[:pps], refs[pps], refs[pps + 1], refs[pps + 2]
    cpp = PAGE_SIZE // CMP_STRIDE
    for r, x in enumerate(x_refs):
        for combo in range(4):
            kv, gp = combo // 2, combo % 2
            pair = x[0, kv, 2 * gp:2 * gp + 2].reshape(2 * HEAD_DIM, PAGE_SIZE)
            pair_t = pair.astype(BF16).T.astype(F32)
            for n in range(cpp):
                row0 = (r * cpp + n) * CMP_SLAB_PITCH
                slab[combo, row0:row0 + CMP_STRIDE, :] = pair_t[n * CMP_STRIDE:(n + 1) * CMP_STRIDE]
    _cmp_y_from_slabs(slab, w_ref, y_ref, CMP_SLAB_PITCH)


def _cmp_y_sample(cache_t, page_table, wcmp):
    s, n_pages = page_table.shape
    cpp = PAGE_SIZE // CMP_STRIDE
    pps = min(CMP_PAGES_PER_STEP, n_pages)
    assert n_pages % pps == 0
    steps = n_pages // pps

    def page_spec(r):
        return pl.BlockSpec((1, 2, N_KV, HEAD_DIM, PAGE_SIZE), lambda b, j, pt: (pt[b, j * pps + r], 0, 0, 0, 0))

    grid_spec = pltpu.PrefetchScalarGridSpec(
        num_scalar_prefetch=1, grid=(s, steps),
        in_specs=[page_spec(r) for r in range(pps)] + [pl.BlockSpec(wcmp.shape, lambda b, j, pt: (0, 0, 0))],
        out_specs=pl.BlockSpec((1, pps * cpp, 1024), lambda b, j, pt: (b, j, 0)),
        scratch_shapes=[pltpu.VMEM((4, pps * cpp * CMP_SLAB_PITCH, LANES), F32)],
    )
    return pl.pallas_call(
        functools.partial(_cmp_y_sample_kernel, pps), grid_spec=grid_spec,
        out_shape=jax.ShapeDtypeStruct((s, n_pages * cpp, 1024), F32),
        compiler_params=_cparams(("parallel", "parallel")), name="cmp_y_sample",
    )(page_table, *([cache_t] * pps), wcmp)


def _nsa_sample_cmp_kernel(nc_valid, qbd_ref, kc_ref, vct_ref, a_ref, oc_ref, isel_ref):
    ncp = kc_ref.shape[2]
    sc = jnp.dot(kc_ref[0, 0], qbd_ref[0, 0], preferred_element_type=F32)
    for g in range(1, N_KV):
        sc = sc + jnp.dot(kc_ref[0, g], qbd_ref[0, g], preferred_element_type=F32)
    nrow = lax.broadcasted_iota(jnp.int32, (ncp, LANES), 0)
    cmask = nrow < nc_valid
    sc = jnp.where(cmask, sc, NEG)
    e = jnp.where(cmask, jnp.exp(sc - jnp.max(sc, axis=0, keepdims=True)), 0.0)
    p = e / jnp.sum(e, axis=0, keepdims=True)
    pb = p.astype(BF16)
    for g in range(N_KV):
        oc_ref[0, g] = jnp.dot(vct_ref[0, g], pb, preferred_element_type=F32)
    a = a_ref[...]
    r = sum(jnp.dot(a, part, preferred_element_type=F32) for part in _split3(p))
    tot = r
    for h in range(1, HPG):
        tot = tot + pltpu.roll(r, LANES - h * N_KV, 1)
    isel_ref[0] = tot


def _topk_sample_kernel(nsel, cur, k_sel, isel_ref, tri_ref, idx_ref):
    jp = isel_ref.shape[0]
    jrow = lax.broadcasted_iota(jnp.int32, (jp, LANES), 0)
    forced = (jrow == 0) | (jrow == cur) | (jrow == cur - 1)
    score = jnp.where(jrow <= cur, jnp.where(forced, FORCE, isel_ref[...]), -1.0)
    score = jnp.where(jrow < nsel, score, -2.0)
    sel = _rank_select(score, k_sel)
    rank = jnp.dot(tri_ref[...], sel.astype(BF16), preferred_element_type=F32)
    jf = jrow.astype(F32)
    rows = [jnp.sum(jnp.where((sel > 0.5) & (rank == float(r + 1)), jf, 0.0), axis=0, keepdims=True)
            for r in range(k_sel)]
    idx_ref[...] = jnp.concatenate(rows, axis=0).astype(jnp.int32)


def _nsa_sample_attn_kernel(n_cache, tbl_ref, q_ref, *refs):
    blocks = refs[:N_KV * n_cache]
    kvs_ref, win_ref, kvw_ref, kvwc_ref, oc_ref, gate_ref, o_ref, wout_ref = refs[N_KV * n_cache:]
    b = pl.program_id(0)
    nt = (((1,), (1,)), ((), ()))
    lane = lax.broadcasted_iota(jnp.int32, (8, PAGE_SIZE), 1)
    kvw_new = kvw_ref[0]
    kvs_new = kvs_ref[0]
    nwin = win_ref.shape[4]
    wlane = lax.broadcasted_iota(jnp.int32, (HEAD_DIM, nwin), 1)
    for c in range(2 * N_KV):
        shifted = pltpu.roll(win_ref[0, c // N_KV, c % N_KV], nwin - 1, 1)
        wout_ref[0, c // N_KV, c % N_KV] = jnp.where(wlane == nwin - 1, kvwc_ref[0, c], shifted)
    for g in range(N_KV):
        col = g * HEAD_DIM
        q = q_ref[0, g]
        qf = q.astype(F32)
        pages = blocks[g * n_cache:(g + 1) * n_cache]
        kt = jnp.concatenate([pg[0, 0, 0].astype(BF16) for pg in pages], axis=1)
        vt = jnp.concatenate([pg[0, 1, 0].astype(BF16) for pg in pages], axis=1)
        bias = []
        for r in range(n_cache):
            half = lax.rem(tbl_ref[(b * N_KV + g) * n_cache + r], 2)
            bias.append(jnp.where((lane >= half * SEL_BLOCK) & (lane < (half + 1) * SEL_BLOCK), 0.0, NEG))
        s = jnp.dot(q, kt, preferred_element_type=F32) + jnp.concatenate(bias, axis=1)
        kn = kvs_new[:, col:col + HEAD_DIM].astype(BF16).astype(F32)
        vn = kvs_new[:, 256 + col:256 + col + HEAD_DIM].astype(BF16).astype(F32)
        s_new = jnp.sum(qf * kn, axis=1, keepdims=True)
        m_s = jnp.maximum(jnp.max(s, axis=1, keepdims=True), s_new)
        p = jnp.exp(s - m_s)
        p_new = jnp.exp(s_new - m_s)
        l_s = jnp.sum(p, axis=1, keepdims=True) + p_new
        o_s = (lax.dot_general(p.astype(BF16), vt, nt, preferred_element_type=F32)
               + p_new.astype(BF16).astype(F32) * vn) / l_s
        kw = win_ref[0, 0, g].astype(BF16)
        vw = win_ref[0, 1, g].astype(BF16)
        sw = jnp.dot(q, kw, preferred_element_type=F32)
        keep = lax.broadcasted_iota(jnp.int32, sw.shape, 1) >= 1
        sw = jnp.where(keep, sw, NEG)
        kwn = kvw_new[:, col:col + HEAD_DIM].astype(BF16).astype(F32)
        vwn = kvw_new[:, 256 + col:256 + col + HEAD_DIM].astype(BF16).astype(F32)
        sw_new = jnp.sum(qf * kwn, axis=1, keepdims=True)
        mw = jnp.maximum(jnp.max(sw, axis=1, keepdims=True), sw_new)
        pw = jnp.where(keep, jnp.exp(sw - mw), 0.0)
        pw_new = jnp.exp(sw_new - mw)
        lw = jnp.sum(pw, axis=1, keepdims=True) + pw_new
        o_w = (lax.dot_general(pw.astype(BF16), vw, nt, preferred_element_type=F32)
               + pw_new.astype(BF16).astype(F32) * vwn) / lw
        gate = _sigmoid(gate_ref[0, g])
        o_ref[0, g] = gate[0] * oc_ref[0, g] + gate[1] * o_s + gate[2] * o_w


def sample_attention(x_sample, cmp_t, slc_t, win_t, page_table, norm_mix_pre, w_row, w_t,
                     wcmp, cmp_pe_k, cmp_w1_k, cmp_w2_k, cmp_pe_v, cmp_w1_v, cmp_w2_v):
    s = x_sample.shape[0]
    n_pages = page_table.shape[1]
    past = n_pages * PAGE_SIZE
    assert past % SEL_BLOCK == 0 and x_sample.shape[1] == 1
    kv, u, gm, qg = _inproj_sample(x_sample.reshape(s, D_MODEL), norm_mix_pre, w_row, w_t)
    q = qg[:, 0:D_Q].reshape(s, N_KV, HPG, HEAD_DIM) * (HEAD_DIM ** -0.5)
    gn = qg[:, D_Q:].reshape(s, N_KV, 16)[:, :, 0:12].reshape(s, N_KV, HPG, 3)
    kvc_new, kvs_new, kvw_new = kv[:, 0:512], kv[:, 512:1024], kv[:, 1024:1536]

    y = _cmp_y_sample(cmp_t, page_table, wcmp)
    kc, vct = _cmp_combine(y, cmp_pe_k, cmp_w1_k, cmp_w2_k, cmp_pe_v, cmp_w1_v, cmp_w2_v)
    ncp = kc.shape[2]
    nc_valid = (past + 1) // CMP_STRIDE - 1
    nsel = -(-(past + 1) // SEL_BLOCK)
    jp = -(-nsel // 8) * 8
    cur = past // SEL_BLOCK
    k_sel = min(N_SELECT, nsel)
    assert ncp >= nc_valid and (nc_valid - 1) * CMP_STRIDE + CMP_BLOCK - 1 <= past

    qb16 = q.astype(BF16)
    qbd = jnp.zeros((s, N_KV, LANES, LANES), BF16)
    gidx = jnp.arange(N_KV)
    lane_gh = gidx[:, None] + jnp.arange(HPG)[None, :] * N_KV
    qbd = qbd.at[:, gidx[:, None, None], jnp.arange(HEAD_DIM)[None, None, :], lane_gh[:, :, None]].set(qb16)
    pad_rows = lambda a: jnp.pad(a, [(0, 0)] * (a.ndim - 2) + [(0, 8 - HPG), (0, 0)])
    qrow = pad_rows(qb16)

    a = jnp.asarray(np.pad(_sel_matrix(nsel, nc_valid, ncp), ((0, jp - nsel), (0, 0))), BF16)
    oc_t, isel = pl.pallas_call(
        functools.partial(_nsa_sample_cmp_kernel, nc_valid), grid=(s,),
        in_specs=[pl.BlockSpec((1, N_KV, LANES, LANES), lambda b: (b, 0, 0, 0)),
                  pl.BlockSpec((1, N_KV, ncp, LANES), lambda b: (b, 0, 0, 0)),
                  pl.BlockSpec((1, N_KV, HEAD_DIM, ncp), lambda b: (b, 0, 0, 0)),
                  pl.BlockSpec(a.shape, lambda b: (0, 0))],
        out_specs=(pl.BlockSpec((1, N_KV, HEAD_DIM, LANES), lambda b: (b, 0, 0, 0)),
                   pl.BlockSpec((1, jp, LANES), lambda b: (b, 0, 0))),
        out_shape=(jax.ShapeDtypeStruct((s, N_KV, HEAD_DIM, LANES), F32), jax.ShapeDtypeStruct((s, jp, LANES), F32)),
        compiler_params=_cparams(("parallel",)), name="nsa_sample_cmp",
    )(qbd, kc, vct, a)

    assert s * N_KV == LANES
    isel_t = isel[:, :, 0:N_KV].transpose(1, 0, 2).reshape(jp, s * N_KV)
    tri = jnp.asarray(np.tril(np.ones((jp, jp), np.float32)), BF16)
    idx = pl.pallas_call(
        functools.partial(_topk_sample_kernel, nsel, cur, k_sel), grid=(1,),
        in_specs=[pl.BlockSpec((jp, LANES), lambda i: (0, 0)), pl.BlockSpec((jp, jp), lambda i: (0, 0))],
        out_specs=pl.BlockSpec((k_sel, LANES), lambda i: (0, 0)),
        out_shape=jax.ShapeDtypeStruct((k_sel, LANES), jnp.int32),
        compiler_params=_cparams(("arbitrary",)), name="topk_sample",
    )(isel_t, tri)
    n_cache = k_sel - 1
    blk = idx[0:n_cache].T.reshape(s, N_KV, n_cache)
    page = jnp.take_along_axis(page_table, (blk // 2).reshape(s, -1), axis=1).reshape(s, N_KV, n_cache)
    tbl = (page * 2 + blk % 2).astype(jnp.int32).reshape(s * N_KV * n_cache)

    oc_g = oc_t[:, gidx[:, None, None], jnp.arange(HEAD_DIM)[None, None, :], lane_gh[:, :, None]]
    oc_row = pad_rows(oc_g)
    gate_in = pad_rows(jnp.broadcast_to(gn.transpose(0, 1, 3, 2)[..., None], (s, N_KV, 3, HPG, HEAD_DIM)))
    wrows = win_t.shape[4]

    def blk_spec(g, r):
        return pl.BlockSpec((1, 2, 1, HEAD_DIM, PAGE_SIZE),
                            lambda b, t: (t[(b * N_KV + g) * n_cache + r] // 2, 0, g, 0, 0))

    per_b = lambda shape: pl.BlockSpec((1,) + shape, lambda b, t: (b,) + (0,) * len(shape))
    win_shape = (2, N_KV, HEAD_DIM, wrows)
    n_blk = N_KV * n_cache
    grid_spec = pltpu.PrefetchScalarGridSpec(
        num_scalar_prefetch=1, grid=(s,),
        in_specs=[per_b((N_KV, 8, HEAD_DIM))] + [blk_spec(g, r) for g in range(N_KV) for r in range(n_cache)]
                 + [per_b((1, 512)), per_b(win_shape), per_b((1, 512)), per_b((2 * N_KV, HEAD_DIM, 1)),
                    per_b((N_KV, 8, HEAD_DIM)), per_b((N_KV, 3, 8, HEAD_DIM))],
        out_specs=(per_b((N_KV, 8, HEAD_DIM)), per_b(win_shape)),
    )
    o_row, win_out = pl.pallas_call(
        functools.partial(_nsa_sample_attn_kernel, n_cache), grid_spec=grid_spec,
        out_shape=(jax.ShapeDtypeStruct((s, N_KV, 8, HEAD_DIM), F32), jax.ShapeDtypeStruct((s,) + win_shape, F32)),
        compiler_params=_cparams(("parallel",)), name="nsa_sample_attn",
    )(tbl, qrow, *([slc_t] * n_blk), kvs_new.reshape(s, 1, 512), win_t,
      kvw_new.reshape(s, 1, 512), kvw_new.reshape(s, 2 * N_KV, HEAD_DIM, 1), oc_row, gate_in)
    o = o_row[:, :, 0:HPG, :]
    return o.reshape(s, D_Q).astype(BF16), kvc_new, kvs_new, win_out, u, gm


def prompt_attention(x_prompt, norm_mix_pre, w_row, w_t, wcmp, cmp_pe_k, cmp_w1_k, cmp_w2_k, cmp_pe_v, cmp_w1_v, cmp_w2_v):
    bsz, seq_len, _ = x_prompt.shape
    assert seq_len % (2 * SWEEP) == 0 and WIN_KEYS <= seq_len <= SEL_BLOCK * HEAD_DIM
    n = bsz * seq_len
    (u, gm, kas, kaw, slabs, qt, vts, vtw, gnt, kvct, kvst, kvwt) = _inproj_prompt(
        x_prompt.reshape(n, D_MODEL), norm_mix_pre, w_row, w_t, bsz, seq_len)
    y = _cmp_y_prompt(slabs, bsz, seq_len, wcmp)
    kc, vct = _cmp_combine(y, cmp_pe_k, cmp_w1_k, cmp_w2_k, cmp_pe_v, cmp_w1_v, cmp_w2_v)
    o = _nsa_prompt(qt, gnt, kc, vct, kas, vts, kaw, vtw, bsz, seq_len)
    return o, kvct, kvst, kvwt, u, gm


def kernel(x_prompt, x_sample, cache_kv_cmp, cache_kv_slc, state_kv_win, state_s5, page_table, norm_mix_pre, norm_mix_post, norm_mlp_pre, norm_mlp_post, w_in, cmp_pe_k, cmp_w1_k, cmp_w2_k, cmp_pe_v, cmp_w1_v, cmp_w2_v, s5_a_re, s5_a_im, s5_log_dt, s5_b_re, s5_b_im, s5_c_re, s5_c_im, s5_d, s5_w_glu, s5_b_glu, w_branch_nsa, w_branch_s5, w_out, w_mlp_up, w_mlp_down):
    bsz, seq_len, _ = x_prompt.shape
    s = x_sample.shape[0]
    w_row, w_t = _inproj_weights(w_in)
    wcmp = _cmp_weights(cmp_w1_k, cmp_w1_v)
    cmp_w = (cmp_pe_k, cmp_w1_k, cmp_w2_k, cmp_pe_v, cmp_w1_v, cmp_w2_v)
    s5_ops = _s5_prep(s5_a_re, s5_a_im, s5_log_dt, s5_b_re, s5_b_im, s5_c_re, s5_c_im)
    mlp_w = _merge_weights(s5_w_glu, s5_b_glu, w_branch_nsa, w_branch_s5, w_out, w_mlp_up, w_mlp_down,
                           norm_mix_post, norm_mlp_pre, norm_mlp_post)

    o_p, kvc_p, kvs_p, kvw_p, u_p, gm_p = prompt_attention(x_prompt, norm_mix_pre, w_row, w_t, wcmp, *cmp_w)
    ys5_p, s5_p = _s5_prompt(u_p, s5_ops, s5_d, bsz, seq_len)
    y_p = _merge_mlp(x_prompt.reshape(bsz * seq_len, D_MODEL), o_p, ys5_p, gm_p, mlp_w)

    feature_major = lambda c: jnp.transpose(c, (0, 2, 3, 4, 1))
    o_s, kvc_s, kvs_s, win_t, u_s, gm_s = sample_attention(
        x_sample, feature_major(cache_kv_cmp), feature_major(cache_kv_slc), feature_major(state_kv_win),
        page_table, norm_mix_pre, w_row, w_t, wcmp, *cmp_w)
    win_s = jnp.transpose(win_t, (0, 4, 1, 2, 3))
    ys5_s, s5_s = _s5_sample(u_s, state_s5, s5_ops, s5_d)
    ys5_s = ys5_s.reshape(s, S5_NOCT, LANES).transpose(1, 0, 2)
    y_s = _merge_mlp(x_sample.reshape(s, D_MODEL), o_s, ys5_s, gm_s, mlp_w)

    kv5 = lambda a, b, t: a.reshape(b, t, 2, N_KV, HEAD_DIM)
    token_major = lambda a: jnp.transpose(a.reshape(bsz, 2, N_KV, HEAD_DIM, -1), (0, 4, 1, 2, 3))
    win_rows = min(WINDOW, seq_len)
    win_p = token_major(kvw_p[:, :, seq_len - win_rows:])
    if win_rows < WINDOW:
        win_p = jnp.pad(win_p, ((0, 0), (WINDOW - win_rows, 0), (0, 0), (0, 0), (0, 0)))
    return (y_p.reshape(bsz, seq_len, D_MODEL), y_s.reshape(s, 1, D_MODEL),
            token_major(kvc_p), token_major(kvs_p), win_p, s5_p.astype(x_prompt.dtype),
            kv5(kvc_s, s, 1), kv5(kvs_s, s, 1), kv5(win_s, s, state_kv_win.shape[1]), s5_s.astype(state_s5.dtype))
```

```python
import functools
import math

import numpy as np
import jax
import jax.numpy as jnp
from jax import lax
from jax.experimental import pallas as pl
from jax.experimental.pallas import tpu as pltpu

F32 = jnp.float32
BF16 = jnp.bfloat16

D_MODEL = 1024
HEAD_DIM = 64
N_HEADS = 16
N_KV = 4
HPG = 4
CMP_STRIDE = 16
CMP_BLOCK = 32
SEL_BLOCK = 64
N_SELECT = 16
WINDOW = 512
Q_BLOCK = 128
S5_WIDTH = 512
S5_CH = 16
S5_GROUPS = 32
S5_STATE = 64
D_FF = 4096
D_Q = 1024
D_KV = 256
PAGE_SIZE = 128
EPS = 1e-6
NEG = -1e30
FORCE = 1e4
LANES = 128
VMEM_LIMIT = 56 * 1024 * 1024
SWEEP = 512
WIN_KEYS = WINDOW + Q_BLOCK
Q_SCALE_LOG2 = HEAD_DIM ** -0.5 * math.log2(math.e)


def _cparams(sem):
    return pltpu.CompilerParams(dimension_semantics=sem, vmem_limit_bytes=VMEM_LIMIT)


def _gelu(x):
    return 0.5 * x * (1.0 + jnp.tanh(math.sqrt(2.0 / math.pi) * (x + 0.044715 * (x * x * x))))


def _sigmoid(x):
    return 1.0 / (1.0 + jnp.exp(-x))


def _rms(x, g):
    ms = jnp.mean(x * x, axis=-1, keepdims=True)
    return (x * lax.rsqrt(ms + EPS)) * g


def _split3(x):
    hi = x.astype(BF16)
    r1 = x - hi.astype(F32)
    mid = r1.astype(BF16)
    lo = (r1 - mid.astype(F32)).astype(BF16)
    return hi, mid, lo


WT_Q, WT_GN, WT_KV = 0, D_Q, D_Q + 64
WT_ROWS = WT_KV + 6 * D_KV


def _inproj_prompt_kernel(seq_len, x_ref, g_ref, wr_ref, wt_ref,
                          u_ref, gm_ref, kas_ref, kaw_ref, slab_ref,
                          qt_ref, vts_ref, vtw_ref, gnt_ref, kvct_ref, kvst_ref, kvwt_ref):
    tm = x_ref.shape[0]
    hb = _rms(x_ref[...], g_ref[...]).astype(BF16)

    def rowdot(lo, hi):
        return jnp.dot(hb, wr_ref[:, lo:hi], preferred_element_type=F32)

    zc = rowdot(0, 512)
    for combo in range(4):
        slab_ref[combo] = zc[:, combo * LANES:(combo + 1) * LANES]
    zu = rowdot(512, 1024)
    for m in range(4):
        u_ref[m] = zu[:, m * LANES:(m + 1) * LANES]
    gm_ref[...] = rowdot(1024, 3072)
    row = pl.program_id(0) * tm + lax.broadcasted_iota(jnp.int32, (tm, LANES), 0)
    blk = lax.rem(row, seq_len) // SEL_BLOCK
    lane = lax.broadcasted_iota(jnp.int32, (tm, LANES), 1)
    onehot = jnp.where(lane - HEAD_DIM == blk, 1.0, 0.0)
    zs = rowdot(3072, 3584)
    zw = rowdot(3584, 4096)
    for g in range(N_KV):
        kas_ref[g] = (zs[:, g * LANES:(g + 1) * LANES] + onehot).astype(BF16)
        kaw_ref[g] = zw[:, g * LANES:(g + 1) * LANES].astype(BF16)
    zt = lax.dot_general(wt_ref[...], hb, (((1,), (1,)), ((), ())), preferred_element_type=F32)
    kv0 = WT_KV
    kvct_ref[0] = zt[kv0:kv0 + 512]
    kvst_ref[0] = zt[kv0 + 512:kv0 + 1024]
    kvwt_ref[0] = zt[kv0 + 1024:kv0 + 1536]
    for c in range(tm // LANES):
        sl = slice(c * LANES, (c + 1) * LANES)
        qt_ref[c] = (zt[WT_Q:WT_Q + D_Q, sl] * Q_SCALE_LOG2).astype(BF16)
        gnt_ref[c] = zt[WT_GN:WT_GN + 64, sl]
        vts_ref[c] = zt[kv0 + 768:kv0 + 1024, sl].astype(BF16)
        vtw_ref[c] = zt[kv0 + 1280:kv0 + 1536, sl].astype(BF16)


def _inproj_weights(w_in):
    wq, wkv, wgn, wu, wgm = (w_in[:, :1024], w_in[:, 1024:2560], w_in[:, 2560:2608],
                             w_in[:, 2608:3120], w_in[:, 3120:])
    wkv6 = wkv.reshape(D_MODEL, 3, 2, N_KV, HEAD_DIM)
    zpad = jnp.zeros((D_MODEL, N_KV, HEAD_DIM), F32)
    kaug_s = jnp.concatenate([wkv6[:, 1, 0], zpad], axis=-1).reshape(D_MODEL, N_KV * LANES)
    kaug_w = jnp.concatenate([wkv6[:, 2, 0], zpad], axis=-1).reshape(D_MODEL, N_KV * LANES)
    w_row = jnp.concatenate([wkv[:, 0:512], wu, wgm, kaug_s, kaug_w], axis=1).astype(BF16)
    gn_rows = jnp.pad(wgn.T.reshape(N_KV, HPG * 3, D_MODEL), ((0, 0), (0, 4), (0, 0))).reshape(64, D_MODEL)
    w_t = jnp.concatenate([wq.T, gn_rows, wkv.T], axis=0).astype(BF16)
    return w_row, w_t


def _inproj_prompt(x2, g_pre, w_row, w_t, bsz, seq_len, tm=256):
    n = x2.shape[0]
    nc = n // LANES
    cpt = tm // LANES
    per = seq_len // tm
    row = lambda w: pl.BlockSpec((tm, w), lambda i: (i, 0))
    fmaj = pl.BlockSpec((1, 512, tm), lambda i: (i // per, 0, i % per))
    out_shape = (
        jax.ShapeDtypeStruct((4, n, LANES), F32), jax.ShapeDtypeStruct((n, 2048), F32),
        jax.ShapeDtypeStruct((N_KV, n, LANES), BF16), jax.ShapeDtypeStruct((N_KV, n, LANES), BF16),
        jax.ShapeDtypeStruct((4, n, LANES), F32),
        jax.ShapeDtypeStruct((nc, 1024, LANES), BF16), jax.ShapeDtypeStruct((nc, 256, LANES), BF16),
        jax.ShapeDtypeStruct((nc, 256, LANES), BF16), jax.ShapeDtypeStruct((nc, 64, LANES), F32),
        jax.ShapeDtypeStruct((bsz, 512, seq_len), F32), jax.ShapeDtypeStruct((bsz, 512, seq_len), F32),
        jax.ShapeDtypeStruct((bsz, 512, seq_len), F32),
    )
    out_specs = (
        pl.BlockSpec((4, tm, LANES), lambda i: (0, i, 0)), row(2048),
        pl.BlockSpec((N_KV, tm, LANES), lambda i: (0, i, 0)), pl.BlockSpec((N_KV, tm, LANES), lambda i: (0, i, 0)),
        pl.BlockSpec((4, tm, LANES), lambda i: (0, i, 0)),
        pl.BlockSpec((cpt, 1024, LANES), lambda i: (i, 0, 0)), pl.BlockSpec((cpt, 256, LANES), lambda i: (i, 0, 0)),
        pl.BlockSpec((cpt, 256, LANES), lambda i: (i, 0, 0)), pl.BlockSpec((cpt, 64, LANES), lambda i: (i, 0, 0)),
        fmaj, fmaj, fmaj,
    )
    return pl.pallas_call(
        functools.partial(_inproj_prompt_kernel, seq_len),
        grid=(n // tm,),
        in_specs=[row(D_MODEL), pl.BlockSpec((1, D_MODEL), lambda i: (0, 0)),
                  pl.BlockSpec(w_row.shape, lambda i: (0, 0)), pl.BlockSpec(w_t.shape, lambda i: (0, 0))],
        out_specs=out_specs, out_shape=out_shape,
        compiler_params=_cparams(("parallel",)), name="inproj_prompt",
    )(x2, g_pre.reshape(1, D_MODEL), w_row, w_t)


def _cmp_weights(w1_k, w1_v):
    eye2 = jnp.eye(2, dtype=F32)
    out = []
    for w1 in (w1_k, w1_v):
        w = w1.reshape(2, CMP_STRIDE, HEAD_DIM, HEAD_DIM)
        big = jnp.einsum('fsdh,ij->sidfjh', w, eye2).reshape(CMP_STRIDE * 2 * HEAD_DIM, 2 * 2 * HEAD_DIM)
        out += [big, big]
    return jnp.stack(out).astype(BF16)


def _cmp_y_from_slabs(slab_ref, w_ref, y_ref, pitch=CMP_STRIDE):
    nrows = y_ref.shape[1]
    for combo in range(4):
        xg = jnp.concatenate([slab_ref[combo, pl.ds(s, nrows, stride=pitch), :] for s in range(CMP_STRIDE)],
                             axis=1)
        y_ref[0, :, combo * 256:(combo + 1) * 256] = jnp.dot(xg.astype(BF16), w_ref[combo],
                                                             preferred_element_type=F32)


def _cmp_y_prompt(slabs, bsz, seq_len, wcmp):
    nch = seq_len // CMP_STRIDE
    rb = min(nch, 128)
    per = nch // rb
    return pl.pallas_call(
        _cmp_y_from_slabs,
        grid=(bsz, per),
        in_specs=[pl.BlockSpec((4, rb * CMP_STRIDE, LANES), lambda b, j: (0, b * per + j, 0)),
                  pl.BlockSpec(wcmp.shape, lambda b, j: (0, 0, 0))],
        out_specs=pl.BlockSpec((1, rb, 1024), lambda b, j: (b, j, 0)),
        out_shape=jax.ShapeDtypeStruct((bsz, nch, 1024), F32),
        compiler_params=_cparams(("parallel", "parallel")), name="cmp_y_prompt",
    )(slabs, wcmp)


def _cmp_combine_kernel(y_ref, pe_ref, w1f_ref, w2k_ref, w2vt_ref, kc_ref, vct_ref):
    r = y_ref.shape[1]
    pos = jnp.dot(pe_ref[...], w1f_ref[...], preferred_element_type=F32,
                  precision=lax.Precision.HIGHEST)
    for combo in range(4):
        kv, gp = combo // 2, combo % 2
        first = y_ref[0, :, combo * 256: combo * 256 + LANES]
        second = pltpu.roll(y_ref[0, :, combo * 256 + LANES: combo * 256 + 2 * LANES], r - 1, 0)
        p1 = pos[0:1, kv * HEAD_DIM:(kv + 1) * HEAD_DIM]
        pre = first + second + jnp.concatenate([p1, p1], axis=1)
        act = _gelu(pre).astype(BF16)
        if kv == 0:
            kc = jnp.dot(act, w2k_ref[...], preferred_element_type=F32)
            kc_ref[0, 2 * gp] = kc[:, 0:LANES].astype(BF16)
            kc_ref[0, 2 * gp + 1] = kc[:, LANES:2 * LANES].astype(BF16)
        else:
            vct = lax.dot_general(w2vt_ref[...], act, (((1,), (1,)), ((), ())), preferred_element_type=F32)
            vct_ref[0, 2 * gp] = vct[0:HEAD_DIM].astype(BF16)
            vct_ref[0, 2 * gp + 1] = vct[HEAD_DIM:2 * HEAD_DIM].astype(BF16)


def _cmp_combine(y, pe_k, w1_k, w2_k, pe_v, w1_v, w2_v):
    s, r, _ = y.shape
    pe = jnp.concatenate([pe_k.reshape(1, -1), pe_v.reshape(1, -1)], axis=1)
    pe8 = jnp.pad(pe, ((0, 7), (0, 0)))
    z = jnp.zeros((CMP_BLOCK * HEAD_DIM, HEAD_DIM), F32)
    w1f = jnp.concatenate([jnp.concatenate([w1_k.reshape(-1, HEAD_DIM), z], axis=1),
                           jnp.concatenate([z, w1_v.reshape(-1, HEAD_DIM)], axis=1)], axis=0)
    z64 = jnp.zeros((HEAD_DIM, HEAD_DIM), F32)
    w2k = jnp.concatenate([jnp.concatenate([w2_k, z64, z64, z64], axis=1),
                           jnp.concatenate([z64, z64, w2_k, z64], axis=1)], axis=0).astype(BF16)
    w2vt = jnp.concatenate([jnp.concatenate([w2_v.T, z64], axis=1),
                            jnp.concatenate([z64, w2_v.T], axis=1)], axis=0).astype(BF16)
    full = lambda a: pl.BlockSpec(a.shape, lambda i: (0,) * a.ndim)
    return pl.pallas_call(
        _cmp_combine_kernel,
        grid=(s,),
        in_specs=[pl.BlockSpec((1, r, 1024), lambda i: (i, 0, 0)), full(pe8), full(w1f), full(w2k), full(w2vt)],
        out_specs=(pl.BlockSpec((1, N_KV, r, LANES), lambda i: (i, 0, 0, 0)),
                   pl.BlockSpec((1, N_KV, HEAD_DIM, r), lambda i: (i, 0, 0, 0))),
        out_shape=(jax.ShapeDtypeStruct((s, N_KV, r, LANES), BF16),
                   jax.ShapeDtypeStruct((s, N_KV, HEAD_DIM, r), BF16)),
        compiler_params=_cparams(("parallel",)), name="cmp_combine",
    )(y, pe8, w1f, w2k, w2vt)


def _sel_matrix(nsel, nc, ncp):
    j = np.arange(nsel)
    lo = np.clip((j * SEL_BLOCK - CMP_BLOCK) // CMP_STRIDE + 1, 0, nc)
    hi = np.clip((j * SEL_BLOCK + SEL_BLOCK - 1) // CMP_STRIDE + 1, 0, nc)
    n = np.arange(ncp)
    return ((n[None, :] >= lo[:, None]) & (n[None, :] < hi[:, None])).astype(np.float32)


def _rank_select(score, k_sel):
    nj, nl = score.shape
    sub = 8
    tiles = [score[v * sub:(v + 1) * sub] for v in range(nj // sub)]
    cnts = [jnp.zeros((sub, nl), F32) for _ in tiles]
    jloc = lax.broadcasted_iota(jnp.int32, (sub, nl), 0)
    for i in range(nj):
        bi = jnp.broadcast_to(score[i:i + 1, :], (sub, nl))
        for v, t in enumerate(tiles):
            if v * sub > i:
                inc = jnp.where(bi >= t, 1.0, 0.0)
            elif v * sub + sub - 1 < i:
                inc = jnp.where(bi > t, 1.0, 0.0)
            else:
                inc = jnp.where(jloc > i - v * sub, jnp.where(bi >= t, 1.0, 0.0), jnp.where(bi > t, 1.0, 0.0))
            cnts[v] = cnts[v] + inc
    cnt = jnp.concatenate(cnts, axis=0)
    return jnp.where(cnt < k_sel, 1.0, 0.0)


NSA_TILES_PER_STEP = 8


def _nsa_prompt_kernel(nc_valid, k_sel, qt_ref, gnt_ref, kc_ref, vct_ref, kas_ref, vts_ref, kaw_ref, vtw_ref,
                       a_ref, o_ref, *score_bufs):
    for t in range(NSA_TILES_PER_STEP):
        one = pl.ds(t, 1)
        _nsa_prompt_tile(nc_valid, k_sel, pl.program_id(2) * NSA_TILES_PER_STEP + t,
                         qt_ref.at[one], gnt_ref.at[one], kc_ref, vct_ref, kas_ref, vts_ref, kaw_ref, vtw_ref, a_ref,
                         o_ref.at[pl.ds(t * Q_BLOCK, Q_BLOCK)], score_bufs[2 * t], score_bufs[2 * t + 1])


def _nsa_prompt_tile(nc_valid, k_sel, qb, qt_ref, gnt_ref, kc_ref, vct_ref, kas_ref, vts_ref, kaw_ref, vtw_ref,
                     a_ref, o_ref, sa_ref, sb_ref):
    heads = lambda t: jnp.concatenate([t] * HPG, axis=1)

    def pv_and_sum(vt, p):
        va = jnp.concatenate([vt, jnp.ones((16, vt.shape[1]), BF16)], axis=0)
        out = jnp.dot(va, p.astype(BF16), preferred_element_type=F32)
        return out[0:HEAD_DIM], out[HEAD_DIM:HEAD_DIM + 1]
    qt = qt_ref[0]
    ql = lax.broadcasted_iota(jnp.int32, (1, Q_BLOCK), 1)
    tl2 = lax.broadcasted_iota(jnp.int32, (Q_BLOCK, Q_BLOCK), 0)
    ql2 = lax.broadcasted_iota(jnp.int32, (Q_BLOCK, Q_BLOCK), 1)
    b_diag = jnp.where(tl2 <= ql2, 0.0, NEG)
    b_first = jnp.where(tl2 > ql2, 0.0, NEG)
    zero_q = jnp.zeros((HEAD_DIM, Q_BLOCK), BF16)
    rhs_q = jnp.concatenate(
        [jnp.concatenate([qt[h * HEAD_DIM:(h + 1) * HEAD_DIM], zero_q], axis=0) for h in range(HPG)], axis=1)

    ncp = kc_ref.shape[2]
    assert nc_valid >= ncp - 1
    sc = jnp.dot(kc_ref[0, 0], rhs_q, preferred_element_type=F32)
    edge = lax.shift_right_arithmetic(ql - (CMP_BLOCK - 1), 4)
    nrel = lax.broadcasted_iota(jnp.int32, (ncp, Q_BLOCK), 0) - qb * (Q_BLOCK // CMP_STRIDE)
    sc = sc + heads(jnp.where(nrel <= edge, 0.0, NEG))
    e = jnp.exp2(sc - jnp.max(sc, axis=0, keepdims=True))
    den = jnp.sum(e, axis=0, keepdims=True)
    any_visible = heads(jnp.where(qb * Q_BLOCK + ql >= CMP_BLOCK - 1, 1.0, 0.0))
    p = e * (any_visible / den)
    o_c = jnp.dot(vct_ref[0, 0], p.astype(BF16), preferred_element_type=F32)
    imp = p[:, 0:Q_BLOCK]
    for h in range(1, HPG):
        imp = imp + p[:, h * Q_BLOCK:(h + 1) * Q_BLOCK]
    a = a_ref[...]
    imp_sel = sum(jnp.dot(a, part, preferred_element_type=F32) for part in _split3(imp))

    c0 = jnp.maximum(qb - WINDOW // Q_BLOCK, 0)
    wstart = pl.multiple_of(c0 * Q_BLOCK, Q_BLOCK)
    sw = jnp.dot(kaw_ref[0, pl.ds(wstart, WIN_KEYS), :], rhs_q, preferred_element_type=F32)
    wbias = []
    for i in range(WIN_KEYS // Q_BLOCK):
        d = qb - c0 - i
        wbias.append(jnp.where(d == WINDOW // Q_BLOCK, b_first,
                               jnp.where(d == 0, b_diag, jnp.where(d < 0, NEG, 0.0))))
    sw = sw + heads(jnp.concatenate(wbias, axis=0))
    vtw = jnp.concatenate([vtw_ref[c0 + i] for i in range(WIN_KEYS // Q_BLOCK)], axis=1)
    o_w, lw = pv_and_sum(vtw, jnp.exp2(sw - jnp.max(sw, axis=0, keepdims=True)))
    o_w = o_w / lw

    nsel = imp_sel.shape[0]
    jrow = lax.broadcasted_iota(jnp.int32, (nsel, Q_BLOCK), 0)
    qp1 = qb * Q_BLOCK + lax.broadcasted_iota(jnp.int32, (nsel, Q_BLOCK), 1)
    cur = qp1 // SEL_BLOCK
    forced = (jrow == 0) | (jrow == cur) | (jrow == cur - 1)
    score = jnp.where(jrow <= cur, jnp.where(forced, FORCE, imp_sel), -1.0)
    sel = jnp.where(jrow <= cur, _rank_select(score, k_sel), 0.0)
    if nsel < HEAD_DIM:
        sel = jnp.concatenate([sel, jnp.zeros((HEAD_DIM - nsel, Q_BLOCK), F32)], axis=0)
    mq = (sel - 1.0) * 1e30
    jrow64 = lax.broadcasted_iota(jnp.int32, (HEAD_DIM, Q_BLOCK), 0)
    mq_past = jnp.where(jrow64 >= qb * (Q_BLOCK // SEL_BLOCK), NEG, mq)

    def with_mask(mrows):
        mb = mrows.astype(BF16)
        return jnp.concatenate(
            [jnp.concatenate([qt[h * HEAD_DIM:(h + 1) * HEAD_DIM], mb], axis=0) for h in range(HPG)], axis=1)

    rhs_diag, rhs_past = with_mask(mq), with_mask(mq_past)

    dstart = pl.multiple_of(qb * Q_BLOCK, Q_BLOCK)
    sd = jnp.dot(kas_ref[0, pl.ds(dstart, Q_BLOCK), :], rhs_diag, preferred_element_type=F32) + heads(b_diag)
    m = jnp.max(sd, axis=0, keepdims=True)
    acc, l = pv_and_sum(vts_ref[qb], jnp.exp2(sd - m))

    cps = SWEEP // Q_BLOCK
    n_span = (qb + cps - 1) // cps

    def span_scores(i, buf):
        start = pl.multiple_of(i * SWEEP, SWEEP)
        s = jnp.dot(kas_ref[0, pl.ds(start, SWEEP), :], rhs_past, preferred_element_type=F32)
        buf[...] = s
        return jnp.max(s, axis=0, keepdims=True)

    def span_consume(i, buf, smax, m, l, acc):
        m_new = jnp.maximum(m, smax)
        alpha = jnp.exp2(m - m_new)
        vt = jnp.concatenate([vts_ref[cps * i + k] for k in range(cps)], axis=1)
        pv, psum = pv_and_sum(vt, jnp.exp2(buf[...] - m_new))
        return m_new, l * alpha + psum, acc * alpha + pv

    n_pair = (n_span + 1) // 2

    def pair(k, carry, prefetch):
        m, l, acc, smax0 = carry
        smax1 = span_scores(2 * k + 1, sb_ref)
        m, l, acc = span_consume(2 * k, sa_ref, smax0, m, l, acc)
        smax0 = span_scores(2 * k + 2, sa_ref) if prefetch else smax0
        m, l, acc = span_consume(2 * k + 1, sb_ref, smax1, m, l, acc)
        return m, l, acc, smax0

    carry = lax.fori_loop(0, n_pair - 1, lambda k, c: pair(k, c, True), (m, l, acc, span_scores(0, sa_ref)))
    m, l, acc, _ = pair(jnp.maximum(n_pair - 1, 0), carry, False)
    o_s = acc / l

    gate = _sigmoid(gnt_ref[0])
    outs = []
    for h in range(HPG):
        sl = slice(h * Q_BLOCK, (h + 1) * Q_BLOCK)
        outs.append(gate[3 * h:3 * h + 1] * o_c[:, sl] + gate[3 * h + 1:3 * h + 2] * o_s[:, sl]
                    + gate[3 * h + 2:3 * h + 3] * o_w[:, sl])
    o_t = jnp.concatenate(outs, axis=0)
    o_ref[...] = o_t.astype(BF16).T


def _nsa_prompt(qt, gnt, kc, vct, kas, vts, kaw, vtw, bsz, seq_len):
    nq = seq_len // Q_BLOCK
    nsel = seq_len // SEL_BLOCK
    nc_valid = seq_len // CMP_STRIDE - 1
    ncp = kc.shape[2]
    k_sel = min(N_SELECT, nsel)
    a = jnp.asarray(_sel_matrix(nsel, nc_valid, ncp), BF16)
    n = bsz * seq_len
    tps = NSA_TILES_PER_STEP
    steps = nq // tps
    score_buf = pltpu.VMEM((SWEEP, HPG * Q_BLOCK), F32)
    return pl.pallas_call(
        functools.partial(_nsa_prompt_kernel, nc_valid, k_sel),
        grid=(bsz, N_KV, steps),
        in_specs=[
            pl.BlockSpec((tps, HPG * HEAD_DIM, LANES), lambda b, g, i: (b * steps + i, g, 0)),
            pl.BlockSpec((tps, 16, LANES), lambda b, g, i: (b * steps + i, g, 0)),
            pl.BlockSpec((1, 1, ncp, LANES), lambda b, g, i: (b, g, 0, 0)),
            pl.BlockSpec((1, 1, HEAD_DIM, ncp), lambda b, g, i: (b, g, 0, 0)),
            pl.BlockSpec((1, seq_len, LANES), lambda b, g, i: (g, b, 0)),
            pl.BlockSpec((nq, HEAD_DIM, LANES), lambda b, g, i: (b, g, 0)),
            pl.BlockSpec((1, seq_len, LANES), lambda b, g, i: (g, b, 0)),
            pl.BlockSpec((nq, HEAD_DIM, LANES), lambda b, g, i: (b, g, 0)),
            pl.BlockSpec(a.shape, lambda b, g, i: (0, 0)),
        ],
        out_specs=pl.BlockSpec((tps * Q_BLOCK, HPG * HEAD_DIM), lambda b, g, i: (b * steps + i, g)),
        out_shape=jax.ShapeDtypeStruct((n, D_Q), BF16),
        scratch_shapes=[score_buf] * (2 * tps),
        compiler_params=_cparams(("parallel", "parallel", "arbitrary")), name="nsa_prompt",
    )(qt, gnt, kc, vct, kas, vts, kaw, vtw, a)


S5_L = 16
S5_W = S5_L * S5_CH
S5_P = 2 * S5_STATE
S5_OCT = LANES // S5_CH
S5_NOCT = S5_GROUPS // S5_OCT
S5_OW = S5_L * LANES
S5_OP = S5_OCT * S5_P


def _s5_prep_kernel(are_ref, aim_ref, ldt_ref, bre_ref, bim_ref, cre_ref, cim_ref, spread_ref,
                    tg_ref, sg_ref, ogt_ref, misc_ref, bs_ref, oct_ref):
    g8 = lax.rem(pl.program_id(0), S5_OCT)
    are, aim = are_ref[0], aim_ref[0]
    dt = jnp.exp(ldt_ref[0])
    mag = jnp.exp(are * dt)
    ar, ai = mag * jnp.cos(aim * dt), mag * jnp.sin(aim * dt)
    den = are * are + aim * aim
    fr = ((ar - 1.0) * are + ai * aim) / den
    fi = (ai * are - (ar - 1.0) * aim) / den
    bre, bim = bre_ref[0], bim_ref[0]
    cre, cim = cre_ref[0], cim_ref[0]
    br, bi = fr * bre - fi * bim, fr * bim + fi * bre
    lo16 = lax.broadcasted_iota(jnp.int32, (S5_CH, S5_P), 1) < S5_STATE
    lo1 = lax.broadcasted_iota(jnp.int32, (1, S5_P), 1) < S5_STATE
    pr, pi = [jnp.ones_like(ar)], [jnp.zeros_like(ar)]
    for _ in range(S5_L):
        pr.append(pr[-1] * ar - pi[-1] * ai)
        pi.append(pr[-2] * ai + pi[-1] * ar)
    cpr = [cre * pr[k] - cim * pi[k] for k in range(S5_L + 1)]
    cpi = [cre * pi[k] + cim * pr[k] for k in range(S5_L + 1)]
    rpack = jnp.concatenate([jnp.where(lo16, cpr[k], cpi[k]) for k in range(S5_L)], axis=0)
    bpack = jnp.where(lo16, br, -bi)
    krow = lax.dot_general(bpack, rpack, (((1,), (1,)), ((), ())), preferred_element_type=F32,
                           precision=lax.Precision.HIGHEST)
    wide = jnp.dot(krow.astype(BF16), spread_ref[...], preferred_element_type=F32)
    wide = pltpu.roll(wide, g8 * S5_CH, 1)
    lane = lax.broadcasted_iota(jnp.int32, (S5_CH, S5_OW), 1)
    zpad = jnp.zeros((S5_CH, S5_OP - S5_P), F32)
    place = lambda blk: pltpu.roll(jnp.concatenate([blk, zpad], axis=1), g8 * S5_P, 1).astype(BF16)
    for j in range(S5_L):
        rows = pl.ds(pl.multiple_of(j * LANES + g8 * S5_CH, S5_CH), S5_CH)
        shifted = wide if j == 0 else pltpu.roll(wide, j * LANES, 1)
        tg_ref[0, rows, :] = jnp.where(lane >= j * LANES, shifted, 0.0).astype(BF16)
        k = S5_L - 1 - j
        sblk = jnp.where(lo16, pr[k] * br - pi[k] * bi, pr[k] * bi + pi[k] * br)
        sg_ref[0, rows, :] = place(sblk)
        if j == S5_L - 1:
            bs_ref[0] = sblk
        ogt_ref[0, rows, :] = place(jnp.where(lo16, cpr[j + 1], -cpi[j + 1]))
    oct_ref[0] = jnp.where(lo16, cre, -cim)
    misc_ref[0] = jnp.concatenate([
        pr[S5_L], jnp.where(lo1, -pi[S5_L], pi[S5_L]), ar, jnp.where(lo1, -ai, ai),
        jnp.zeros((4, S5_P), F32)], axis=0)


def _s5_prep(a_re, a_im, log_dt, b_re, b_im, c_re, c_im):
    g = S5_GROUPS
    dup = lambda a: jnp.concatenate([a, a], axis=-1)
    are, aim = dup(a_re).reshape(g, 1, S5_P), dup(a_im).reshape(g, 1, S5_P)
    ldt = jnp.broadcast_to(log_dt.reshape(g, 1, 1), (g, 1, S5_P))
    bre, bim = dup(jnp.swapaxes(b_re, 1, 2)), dup(jnp.swapaxes(b_im, 1, 2))
    cre, cim = dup(c_re), dup(c_im)
    v1 = pl.BlockSpec((1, 1, S5_P), lambda i: (i, 0, 0))
    v16 = pl.BlockSpec((1, S5_CH, S5_P), lambda i: (i, 0, 0))
    lag_co = np.arange(S5_W)
    spread = np.zeros((S5_W, S5_OW), np.float32)
    spread[lag_co, (lag_co // S5_CH) * LANES + lag_co % S5_CH] = 1.0
    spread = jnp.asarray(spread, BF16)
    octet = lambda w: pl.BlockSpec((1, S5_OW, w), lambda i: (i // S5_OCT, 0, 0))
    return pl.pallas_call(
        _s5_prep_kernel, grid=(g,),
        in_specs=[v1, v1, v1, v16, v16, v16, v16, pl.BlockSpec(spread.shape, lambda i: (0, 0))],
        out_specs=(octet(S5_OW), octet(S5_OP), octet(S5_OP), pl.BlockSpec((1, 8, S5_P), lambda i: (i, 0, 0)),
                   v16, v16),
        out_shape=(jax.ShapeDtypeStruct((S5_NOCT, S5_OW, S5_OW), BF16), jax.ShapeDtypeStruct((S5_NOCT, S5_OW, S5_OP), BF16),
                   jax.ShapeDtypeStruct((S5_NOCT, S5_OW, S5_OP), BF16), jax.ShapeDtypeStruct((g, 8, S5_P), F32),
                   jax.ShapeDtypeStruct((g, S5_CH, S5_P), F32), jax.ShapeDtypeStruct((g, S5_CH, S5_P), F32)),
        compiler_params=_cparams(("arbitrary",)), name="s5_prep",
    )(are, aim, ldt, bre, bim, cre, cim, spread)


def _s5_chunks(u_ref):
    nch = u_ref.shape[1] // S5_L
    return jnp.concatenate([u_ref[0, pl.ds(j, nch, stride=S5_L), :] for j in range(S5_L)], axis=1)


def _s5_sum_kernel(u_ref, sg_ref, s_ref):
    s_ref[0] = jnp.dot(_s5_chunks(u_ref).astype(BF16), sg_ref[0], preferred_element_type=F32)


def _s5_scan_kernel(s_ref, a1_ref, a2_ref, h_ref, last_ref, carry):
    @pl.when(pl.program_id(0) == 0)
    def _():
        carry[...] = jnp.zeros_like(carry)

    a1, a2 = a1_ref[...], a2_ref[...]

    def step(c, h):
        h_ref[c] = h
        return a1 * h + a2 * pltpu.roll(h, S5_STATE, 1) + s_ref[c]

    h = lax.fori_loop(0, s_ref.shape[0], step, carry[...])
    carry[...] = h
    last_ref[...] = h


def _s5_out_kernel(u_ref, h_ref, tg_ref, ogt_ref, d_ref, y_ref):
    u = _s5_chunks(u_ref)
    y = jnp.dot(u.astype(BF16), tg_ref[0], preferred_element_type=F32)
    y = y + lax.dot_general(h_ref[0].astype(BF16), ogt_ref[0], (((1,), (1,)), ((), ())), preferred_element_type=F32)
    y = y + d_ref[0] * u
    nch = y.shape[0]
    for k in range(S5_L):
        y_ref[0, pl.ds(k, nch, stride=S5_L), :] = y[:, k * LANES:(k + 1) * LANES]


def _s5_prompt(uslab, ops, s5_d, bsz, seq_len):
    tg, sg, ogt, misc = ops[0], ops[1], ops[2], ops[3]
    g, nch = S5_GROUPS, seq_len // S5_L
    rows = bsz * nch
    slab = pl.BlockSpec((1, seq_len, LANES), lambda m, b: (m, b, 0))
    per_oct = lambda r, c: pl.BlockSpec((1, r, c), lambda m, b: (m, 0, 0))
    per_row = lambda c: pl.BlockSpec((1, nch, c), lambda m, b: (m, b, 0))
    ssum = pl.pallas_call(
        _s5_sum_kernel, grid=(S5_NOCT, bsz), in_specs=[slab, per_oct(S5_OW, S5_OP)],
        out_specs=per_row(S5_OP), out_shape=jax.ShapeDtypeStruct((S5_NOCT, rows, S5_OP), F32),
        compiler_params=_cparams(("parallel", "parallel")), name="s5_sum",
    )(uslab, sg)
    s_cm = (ssum.reshape(S5_NOCT, bsz, nch, S5_OCT, S5_P).transpose(2, 1, 0, 3, 4).reshape(nch, bsz * g, S5_P))
    a1 = jnp.tile(misc[:, 0, :], (bsz, 1))
    a2 = jnp.tile(misc[:, 1, :], (bsz, 1))
    cb = min(nch, 32)
    hs, last = pl.pallas_call(
        _s5_scan_kernel, grid=(nch // cb,),
        in_specs=[pl.BlockSpec((cb, bsz * g, S5_P), lambda i: (i, 0, 0)),
                  pl.BlockSpec((bsz * g, S5_P), lambda i: (0, 0)), pl.BlockSpec((bsz * g, S5_P), lambda i: (0, 0))],
        out_specs=(pl.BlockSpec((cb, bsz * g, S5_P), lambda i: (i, 0, 0)), pl.BlockSpec((bsz * g, S5_P), lambda i: (0, 0))),
        out_shape=(jax.ShapeDtypeStruct((nch, bsz * g, S5_P), F32), jax.ShapeDtypeStruct((bsz * g, S5_P), F32)),
        scratch_shapes=[pltpu.VMEM((bsz * g, S5_P), F32)],
        compiler_params=_cparams(("arbitrary",)), name="s5_scan",
    )(s_cm, a1, a2)
    h_oct = (hs.reshape(nch, bsz, S5_NOCT, S5_OCT, S5_P).transpose(2, 1, 0, 3, 4).reshape(S5_NOCT, rows, S5_OP))
    dvec = jnp.tile(s5_d.reshape(S5_NOCT, 1, LANES), (1, 1, S5_L))
    yslab = pl.pallas_call(
        _s5_out_kernel, grid=(S5_NOCT, bsz),
        in_specs=[slab, per_row(S5_OP), per_oct(S5_OW, S5_OW), per_oct(S5_OW, S5_OP), per_oct(1, S5_OW)],
        out_specs=slab, out_shape=jax.ShapeDtypeStruct((S5_NOCT, bsz * seq_len, LANES), F32),
        compiler_params=_cparams(("parallel", "parallel")), name="s5_out",
    )(uslab, h_oct, tg, ogt, dvec)
    state = last.reshape(bsz, g, 2, S5_STATE).transpose(0, 1, 3, 2)
    return yslab, state


def _s5_sample_kernel(u_ref, h0_ref, bs_ref, oct_ref, misc_ref, d_ref, y_ref, h1_ref):
    u, h0 = u_ref[0], h0_ref[0]
    hi = lax.Precision.HIGHEST
    bu = jnp.dot(u, bs_ref[0], preferred_element_type=F32, precision=hi)
    h1 = misc_ref[0, 2:3] * h0 + misc_ref[0, 3:4] * pltpu.roll(h0, S5_STATE, 1) + bu
    h1_ref[0] = h1
    y = lax.dot_general(h1, oct_ref[0], (((1,), (1,)), ((), ())), preferred_element_type=F32, precision=hi)
    y_ref[0] = y + d_ref[0] * u


def _s5_sample(u, state, ops, s5_d):
    misc, bs, oct_ = ops[3], ops[4], ops[5]
    s, g = u.shape[0], S5_GROUPS
    ug = u.reshape(s, g, S5_CH).transpose(1, 0, 2)
    h0 = state.astype(F32).transpose(1, 0, 3, 2).reshape(g, s, S5_P)
    gspec = lambda r, c: pl.BlockSpec((1, r, c), lambda i: (i, 0, 0))
    y, h1 = pl.pallas_call(
        _s5_sample_kernel, grid=(g,),
        in_specs=[gspec(s, S5_CH), gspec(s, S5_P), gspec(S5_CH, S5_P), gspec(S5_CH, S5_P), gspec(8, S5_P), gspec(1, S5_CH)],
        out_specs=(gspec(s, S5_CH), gspec(s, S5_P)),
        out_shape=(jax.ShapeDtypeStruct((g, s, S5_CH), F32), jax.ShapeDtypeStruct((g, s, S5_P), F32)),
        compiler_params=_cparams(("parallel",)), name="s5_sample",
    )(ug, h0, bs, oct_, misc, s5_d.reshape(g, 1, S5_CH))
    return (y.transpose(1, 0, 2).reshape(s, S5_WIDTH),
            h1.reshape(g, s, 2, S5_STATE).transpose(1, 0, 3, 2))


def _merge_mlp_kernel(x_ref, o_ref, ys_ref, gm_ref, wglu_ref, bglu_ref, wbn_ref, wbs_ref, wout_ref, wup_ref, wdn_ref,
                      npost_ref, nmpre_ref, nmpost_ref, out_ref):
    dot = lambda a, w_ref: jnp.dot(a.astype(BF16), w_ref[...], preferred_element_type=F32)
    z = _gelu(jnp.concatenate([ys_ref[m] for m in range(S5_NOCT)], axis=1))
    o_s5 = z * _sigmoid(dot(z, wglu_ref) + bglu_ref[...])
    merged = (_sigmoid(gm_ref[:, 0:D_MODEL]) * dot(o_ref[...], wbn_ref)
              + _sigmoid(gm_ref[:, D_MODEL:2 * D_MODEL]) * dot(o_s5, wbs_ref))
    x1 = x_ref[...] + _rms(dot(merged, wout_ref), npost_ref[...])
    hm = _rms(x1, nmpre_ref[...])
    up = jnp.maximum(dot(hm, wup_ref), 0.0)
    f = dot(up * up, wdn_ref)
    out_ref[...] = x1 + _rms(f, nmpost_ref[...])


def _merge_mlp(x2, o_nsa, y_s5, gm, wts, tm=256):
    n = x2.shape[0]
    tm = min(tm, n)
    row = lambda w: pl.BlockSpec((tm, w), lambda i: (i, 0))
    const = lambda a: pl.BlockSpec(a.shape, lambda i: (0, 0), pipeline_mode=pl.Buffered(1))
    return pl.pallas_call(
        _merge_mlp_kernel, grid=(n // tm,),
        in_specs=[row(D_MODEL), row(D_Q), pl.BlockSpec((S5_NOCT, tm, LANES), lambda i: (0, i, 0)), row(2 * D_MODEL)]
                 + [const(a) for a in wts],
        out_specs=row(D_MODEL), out_shape=jax.ShapeDtypeStruct((n, D_MODEL), F32),
        compiler_params=_cparams(("parallel",)), name="merge_mlp",
    )(x2, o_nsa, y_s5, gm, *wts)


def _merge_weights(s5_w_glu, s5_b_glu, w_branch_nsa, w_branch_s5, w_out, w_mlp_up, w_mlp_down,
                   norm_mix_post, norm_mlp_pre, norm_mlp_post):
    r = lambda v: v.reshape(1, -1).astype(F32)
    b = lambda w: w.astype(BF16)
    return (b(s5_w_glu), r(s5_b_glu), b(w_branch_nsa), b(w_branch_s5), b(w_out), b(w_mlp_up), b(w_mlp_down),
            r(norm_mix_post), r(norm_mlp_pre), r(norm_mlp_post))


def _inproj_sample_kernel(x_ref, g_ref, wr_ref, wt_ref, kv_ref, u_ref, gm_ref, qg_ref):
    hb = _rms(x_ref[...], g_ref[...]).astype(BF16)
    u_ref[...] = jnp.dot(hb, wr_ref[:, 512:1024], preferred_element_type=F32)
    gm_ref[...] = jnp.dot(hb, wr_ref[:, 1024:3072], preferred_element_type=F32)
    z = lax.dot_general(hb, wt_ref[...], (((1,), (1,)), ((), ())), preferred_element_type=F32)
    qg_ref[...] = z[:, 0:WT_KV]
    kv_ref[...] = z[:, WT_KV:WT_ROWS]


def _inproj_sample(x2, g_pre, w_row, w_t):
    s = x2.shape[0]
    full = lambda a: pl.BlockSpec(a.shape, lambda i: (0, 0))
    o = lambda w: pl.BlockSpec((s, w), lambda i: (0, 0))
    return pl.pallas_call(
        _inproj_sample_kernel, grid=(1,),
        in_specs=[o(D_MODEL), pl.BlockSpec((1, D_MODEL), lambda i: (0, 0)), full(w_row), full(w_t)],
        out_specs=(o(1536), o(512), o(2048), o(WT_KV)),
        out_shape=(jax.ShapeDtypeStruct((s, 1536), F32), jax.ShapeDtypeStruct((s, 512), F32),
                   jax.ShapeDtypeStruct((s, 2048), F32), jax.ShapeDtypeStruct((s, WT_KV), F32)),
        compiler_params=_cparams(("arbitrary",)), name="inproj_sample",
    )(x2, g_pre.reshape(1, D_MODEL), w_row, w_t)


CMP_PAGES_PER_STEP = 32
CMP_SLAB_PITCH = CMP_STRIDE + 1


def _cmp_y_sample_kernel(pps, pt_ref, *refs):
    x_refs, w_ref, y_ref, slab = refs[:pps], refs[pps], refs[pps + 1], refs[pps + 2]
    cpp = PAGE_SIZE // CMP_STRIDE
    for r, x in enumerate(x_refs):
        for combo in range(4):
            kv, gp = combo // 2, combo % 2
            pair = x[0, kv, 2 * gp:2 * gp + 2].reshape(2 * HEAD_DIM, PAGE_SIZE)
            pair_t = pair.astype(BF16).T.astype(F32)
            for n in range(cpp):
                row0 = (r * cpp + n) * CMP_SLAB_PITCH
                slab[combo, row0:row0 + CMP_STRIDE, :] = pair_t[n * CMP_STRIDE:(n + 1) * CMP_STRIDE]
    _cmp_y_from_slabs(slab, w_ref, y_ref, CMP_SLAB_PITCH)


def _cmp_y_sample(cache_t, page_table, wcmp):
    s, n_pages = page_table.shape
    cpp = PAGE_SIZE // CMP_STRIDE
    pps = min(CMP_PAGES_PER_STEP, n_pages)
    assert n_pages % pps == 0
    steps = n_pages // pps

    def page_spec(r):
        return pl.BlockSpec((1, 2, N_KV, HEAD_DIM, PAGE_SIZE), lambda b, j, pt: (pt[b, j * pps + r], 0, 0, 0, 0))

    grid_spec = pltpu.PrefetchScalarGridSpec(
        num_scalar_prefetch=1, grid=(s, steps),
        in_specs=[page_spec(r) for r in range(pps)] + [pl.BlockSpec(wcmp.shape, lambda b, j, pt: (0, 0, 0))],
        out_specs=pl.BlockSpec((1, pps * cpp, 1024), lambda b, j, pt: (b, j, 0)),
        scratch_shapes=[pltpu.VMEM((4, pps * cpp * CMP_SLAB_PITCH, LANES), F32)],
    )
    return pl.pallas_call(
        functools.partial(_cmp_y_sample_kernel, pps), grid_spec=grid_spec,
        out_shape=jax.ShapeDtypeStruct((s, n_pages * cpp, 1024), F32),
        compiler_params=_cparams(("parallel", "parallel")), name="cmp_y_sample",
    )(page_table, *([cache_t] * pps), wcmp)


def _nsa_sample_cmp_kernel(nc_valid, qbd_ref, kc_ref, vct_ref, a_ref, oc_ref, isel_ref):
    ncp = kc_ref.shape[2]
    sc = jnp.dot(kc_ref[0, 0], qbd_ref[0, 0], preferred_element_type=F32)
    for g in range(1, N_KV):
        sc = sc + jnp.dot(kc_ref[0, g], qbd_ref[0, g], preferred_element_type=F32)
    nrow = lax.broadcasted_iota(jnp.int32, (ncp, LANES), 0)
    cmask = nrow < nc_valid
    sc = jnp.where(cmask, sc, NEG)
    e = jnp.where(cmask, jnp.exp(sc - jnp.max(sc, axis=0, keepdims=True)), 0.0)
    p = e / jnp.sum(e, axis=0, keepdims=True)
    pb = p.astype(BF16)
    for g in range(N_KV):
        oc_ref[0, g] = jnp.dot(vct_ref[0, g], pb, preferred_element_type=F32)
    a = a_ref[...]
    r = sum(jnp.dot(a, part, preferred_element_type=F32) for part in _split3(p))
    tot = r
    for h in range(1, HPG):
        tot = tot + pltpu.roll(r, LANES - h * N_KV, 1)
    isel_ref[0] = tot


def _topk_sample_kernel(nsel, cur, k_sel, isel_ref, tri_ref, idx_ref):
    jp = isel_ref.shape[0]
    jrow = lax.broadcasted_iota(jnp.int32, (jp, LANES), 0)
    forced = (jrow == 0) | (jrow == cur) | (jrow == cur - 1)
    score = jnp.where(jrow <= cur, jnp.where(forced, FORCE, isel_ref[...]), -1.0)
    score = jnp.where(jrow < nsel, score, -2.0)
    sel = _rank_select(score, k_sel)
    rank = jnp.dot(tri_ref[...], sel.astype(BF16), preferred_element_type=F32)
    jf = jrow.astype(F32)
    rows = [jnp.sum(jnp.where((sel > 0.5) & (rank == float(r + 1)), jf, 0.0), axis=0, keepdims=True)
            for r in range(k_sel)]
    idx_ref[...] = jnp.concatenate(rows, axis=0).astype(jnp.int32)


def _nsa_sample_attn_kernel(n_cache, tbl_ref, q_ref, *refs):
    blocks = refs[:N_KV * n_cache]
    kvs_ref, win_ref, kvw_ref, kvwc_ref, oc_ref, gate_ref, o_ref, wout_ref = refs[N_KV * n_cache:]
    b = pl.program_id(0)
    nt = (((1,), (1,)), ((), ()))
    lane = lax.broadcasted_iota(jnp.int32, (8, PAGE_SIZE), 1)
    kvw_new = kvw_ref[0]
    kvs_new = kvs_ref[0]
    nwin = win_ref.shape[4]
    wlane = lax.broadcasted_iota(jnp.int32, (HEAD_DIM, nwin), 1)
    for c in range(2 * N_KV):
        shifted = pltpu.roll(win_ref[0, c // N_KV, c % N_KV], nwin - 1, 1)
        wout_ref[0, c // N_KV, c % N_KV] = jnp.where(wlane == nwin - 1, kvwc_ref[0, c], shifted)
    for g in range(N_KV):
        col = g * HEAD_DIM
        q = q_ref[0, g]
        qf = q.astype(F32)
        pages = blocks[g * n_cache:(g + 1) * n_cache]
        kt = jnp.concatenate([pg[0, 0, 0].astype(BF16) for pg in pages], axis=1)
        vt = jnp.concatenate([pg[0, 1, 0].astype(BF16) for pg in pages], axis=1)
        bias = []
        for r in range(n_cache):
            half = lax.rem(tbl_ref[(b * N_KV + g) * n_cache + r], 2)
            bias.append(jnp.where((lane >= half * SEL_BLOCK) & (lane < (half + 1) * SEL_BLOCK), 0.0, NEG))
        s = jnp.dot(q, kt, preferred_element_type=F32) + jnp.concatenate(bias, axis=1)
        kn = kvs_new[:, col:col + HEAD_DIM].astype(BF16).astype(F32)
        vn = kvs_new[:, 256 + col:256 + col + HEAD_DIM].astype(BF16).astype(F32)
        s_new = jnp.sum(qf * kn, axis=1, keepdims=True)
        m_s = jnp.maximum(jnp.max(s, axis=1, keepdims=True), s_new)
        p = jnp.exp(s - m_s)
        p_new = jnp.exp(s_new - m_s)
        l_s = jnp.sum(p, axis=1, keepdims=True) + p_new
        o_s = (lax.dot_general(p.astype(BF16), vt, nt, preferred_element_type=F32)
               + p_new.astype(BF16).astype(F32) * vn) / l_s
        kw = win_ref[0, 0, g].astype(BF16)
        vw = win_ref[0, 1, g].astype(BF16)
        sw = jnp.dot(q, kw, preferred_element_type=F32)
        keep = lax.broadcasted_iota(jnp.int32, sw.shape, 1) >= 1
        sw = jnp.where(keep, sw, NEG)
        kwn = kvw_new[:, col:col + HEAD_DIM].astype(BF16).astype(F32)
        vwn = kvw_new[:, 256 + col:256 + col + HEAD_DIM].astype(BF16).astype(F32)
        sw_new = jnp.sum(qf * kwn, axis=1, keepdims=True)
        mw = jnp.maximum(jnp.max(sw, axis=1, keepdims=True), sw_new)
        pw = jnp.where(keep, jnp.exp(sw - mw), 0.0)
        pw_new = jnp.exp(sw_new - mw)
        lw = jnp.sum(pw, axis=1, keepdims=True) + pw_new
        o_w = (lax.dot_general(pw.astype(BF16), vw, nt, preferred_element_type=F32)
               + pw_new.astype(BF16).astype(F32) * vwn) / lw
        gate = _sigmoid(gate_ref[0, g])
        o_ref[0, g] = gate[0] * oc_ref[0, g] + gate[1] * o_s + gate[2] * o_w


def sample_attention(x_sample, cmp_t, slc_t, win_t, page_table, norm_mix_pre, w_row, w_t,
                     wcmp, cmp_pe_k, cmp_w1_k, cmp_w2_k, cmp_pe_v, cmp_w1_v, cmp_w2_v):
    s = x_sample.shape[0]
    n_pages = page_table.shape[1]
    past = n_pages * PAGE_SIZE
    assert past % SEL_BLOCK == 0 and x_sample.shape[1] == 1
    kv, u, gm, qg = _inproj_sample(x_sample.reshape(s, D_MODEL), norm_mix_pre, w_row, w_t)
    q = qg[:, 0:D_Q].reshape(s, N_KV, HPG, HEAD_DIM) * (HEAD_DIM ** -0.5)
    gn = qg[:, D_Q:].reshape(s, N_KV, 16)[:, :, 0:12].reshape(s, N_KV, HPG, 3)
    kvc_new, kvs_new, kvw_new = kv[:, 0:512], kv[:, 512:1024], kv[:, 1024:1536]

    y = _cmp_y_sample(cmp_t, page_table, wcmp)
    kc, vct = _cmp_combine(y, cmp_pe_k, cmp_w1_k, cmp_w2_k, cmp_pe_v, cmp_w1_v, cmp_w2_v)
    ncp = kc.shape[2]
    nc_valid = (past + 1) // CMP_STRIDE - 1
    nsel = -(-(past + 1) // SEL_BLOCK)
    jp = -(-nsel // 8) * 8
    cur = past // SEL_BLOCK
    k_sel = min(N_SELECT, nsel)
    assert ncp >= nc_valid and (nc_valid - 1) * CMP_STRIDE + CMP_BLOCK - 1 <= past

    qb16 = q.astype(BF16)
    qbd = jnp.zeros((s, N_KV, LANES, LANES), BF16)
    gidx = jnp.arange(N_KV)
    lane_gh = gidx[:, None] + jnp.arange(HPG)[None, :] * N_KV
    qbd = qbd.at[:, gidx[:, None, None], jnp.arange(HEAD_DIM)[None, None, :], lane_gh[:, :, None]].set(qb16)
    pad_rows = lambda a: jnp.pad(a, [(0, 0)] * (a.ndim - 2) + [(0, 8 - HPG), (0, 0)])
    qrow = pad_rows(qb16)

    a = jnp.asarray(np.pad(_sel_matrix(nsel, nc_valid, ncp), ((0, jp - nsel), (0, 0))), BF16)
    oc_t, isel = pl.pallas_call(
        functools.partial(_nsa_sample_cmp_kernel, nc_valid), grid=(s,),
        in_specs=[pl.BlockSpec((1, N_KV, LANES, LANES), lambda b: (b, 0, 0, 0)),
                  pl.BlockSpec((1, N_KV, ncp, LANES), lambda b: (b, 0, 0, 0)),
                  pl.BlockSpec((1, N_KV, HEAD_DIM, ncp), lambda b: (b, 0, 0, 0)),
                  pl.BlockSpec(a.shape, lambda b: (0, 0))],
        out_specs=(pl.BlockSpec((1, N_KV, HEAD_DIM, LANES), lambda b: (b, 0, 0, 0)),
                   pl.BlockSpec((1, jp, LANES), lambda b: (b, 0, 0))),
        out_shape=(jax.ShapeDtypeStruct((s, N_KV, HEAD_DIM, LANES), F32), jax.ShapeDtypeStruct((s, jp, LANES), F32)),
        compiler_params=_cparams(("parallel",)), name="nsa_sample_cmp",
    )(qbd, kc, vct, a)

    assert s * N_KV == LANES
    isel_t = isel[:, :, 0:N_KV].transpose(1, 0, 2).reshape(jp, s * N_KV)
    tri = jnp.asarray(np.tril(np.ones((jp, jp), np.float32)), BF16)
    idx = pl.pallas_call(
        functools.partial(_topk_sample_kernel, nsel, cur, k_sel), grid=(1,),
        in_specs=[pl.BlockSpec((jp, LANES), lambda i: (0, 0)), pl.BlockSpec((jp, jp), lambda i: (0, 0))],
        out_specs=pl.BlockSpec((k_sel, LANES), lambda i: (0, 0)),
        out_shape=jax.ShapeDtypeStruct((k_sel, LANES), jnp.int32),
        compiler_params=_cparams(("arbitrary",)), name="topk_sample",
    )(isel_t, tri)
    n_cache = k_sel - 1
    blk = idx[0:n_cache].T.reshape(s, N_KV, n_cache)
    page = jnp.take_along_axis(page_table, (blk // 2).reshape(s, -1), axis=1).reshape(s, N_KV, n_cache)
    tbl = (page * 2 + blk % 2).astype(jnp.int32).reshape(s * N_KV * n_cache)

    oc_g = oc_t[:, gidx[:, None, None], jnp.arange(HEAD_DIM)[None, None, :], lane_gh[:, :, None]]
    oc_row = pad_rows(oc_g)
    gate_in = pad_rows(jnp.broadcast_to(gn.transpose(0, 1, 3, 2)[..., None], (s, N_KV, 3, HPG, HEAD_DIM)))
    wrows = win_t.shape[4]

    def blk_spec(g, r):
        return pl.BlockSpec((1, 2, 1, HEAD_DIM, PAGE_SIZE),
                            lambda b, t: (t[(b * N_KV + g) * n_cache + r] // 2, 0, g, 0, 0))

    per_b = lambda shape: pl.BlockSpec((1,) + shape, lambda b, t: (b,) + (0,) * len(shape))
    win_shape = (2, N_KV, HEAD_DIM, wrows)
    n_blk = N_KV * n_cache
    grid_spec = pltpu.PrefetchScalarGridSpec(
        num_scalar_prefetch=1, grid=(s,),
        in_specs=[per_b((N_KV, 8, HEAD_DIM))] + [blk_spec(g, r) for g in range(N_KV) for r in range(n_cache)]
                 + [per_b((1, 512)), per_b(win_shape), per_b((1, 512)), per_b((2 * N_KV, HEAD_DIM, 1)),
                    per_b((N_KV, 8, HEAD_DIM)), per_b((N_KV, 3, 8, HEAD_DIM))],
        out_specs=(per_b((N_KV, 8, HEAD_DIM)), per_b(win_shape)),
    )
    o_row, win_out = pl.pallas_call(
        functools.partial(_nsa_sample_attn_kernel, n_cache), grid_spec=grid_spec,
        out_shape=(jax.ShapeDtypeStruct((s, N_KV, 8, HEAD_DIM), F32), jax.ShapeDtypeStruct((s,) + win_shape, F32)),
        compiler_params=_cparams(("parallel",)), name="nsa_sample_attn",
    )(tbl, qrow, *([slc_t] * n_blk), kvs_new.reshape(s, 1, 512), win_t,
      kvw_new.reshape(s, 1, 512), kvw_new.reshape(s, 2 * N_KV, HEAD_DIM, 1), oc_row, gate_in)
    o = o_row[:, :, 0:HPG, :]
    return o.reshape(s, D_Q).astype(BF16), kvc_new, kvs_new, win_out, u, gm


def prompt_attention(x_prompt, norm_mix_pre, w_row, w_t, wcmp, cmp_pe_k, cmp_w1_k, cmp_w2_k, cmp_pe_v, cmp_w1_v, cmp_w2_v):
    bsz, seq_len, _ = x_prompt.shape
    assert seq_len % (2 * SWEEP) == 0 and WIN_KEYS <= seq_len <= SEL_BLOCK * HEAD_DIM
    n = bsz * seq_len
    (u, gm, kas, kaw, slabs, qt, vts, vtw, gnt, kvct, kvst, kvwt) = _inproj_prompt(
        x_prompt.reshape(n, D_MODEL), norm_mix_pre, w_row, w_t, bsz, seq_len)
    y = _cmp_y_prompt(slabs, bsz, seq_len, wcmp)
    kc, vct = _cmp_combine(y, cmp_pe_k, cmp_w1_k, cmp_w2_k, cmp_pe_v, cmp_w1_v, cmp_w2_v)
    o = _nsa_prompt(qt, gnt, kc, vct, kas, vts, kaw, vtw, bsz, seq_len)
    return o, kvct, kvst, kvwt, u, gm


def kernel(x_prompt, x_sample, cache_kv_cmp, cache_kv_slc, state_kv_win, state_s5, page_table, norm_mix_pre, norm_mix_post, norm_mlp_pre, norm_mlp_post, w_in, cmp_pe_k, cmp_w1_k, cmp_w2_k, cmp_pe_v, cmp_w1_v, cmp_w2_v, s5_a_re, s5_a_im, s5_log_dt, s5_b_re, s5_b_im, s5_c_re, s5_c_im, s5_d, s5_w_glu, s5_b_glu, w_branch_nsa, w_branch_s5, w_out, w_mlp_up, w_mlp_down):
    bsz, seq_len, _ = x_prompt.shape
    s = x_sample.shape[0]
    w_row, w_t = _inproj_weights(w_in)
    wcmp = _cmp_weights(cmp_w1_k, cmp_w1_v)
    cmp_w = (cmp_pe_k, cmp_w1_k, cmp_w2_k, cmp_pe_v, cmp_w1_v, cmp_w2_v)
    s5_ops = _s5_prep(s5_a_re, s5_a_im, s5_log_dt, s5_b_re, s5_b_im, s5_c_re, s5_c_im)
    mlp_w = _merge_weights(s5_w_glu, s5_b_glu, w_branch_nsa, w_branch_s5, w_out, w_mlp_up, w_mlp_down,
                           norm_mix_post, norm_mlp_pre, norm_mlp_post)

    o_p, kvc_p, kvs_p, kvw_p, u_p, gm_p = prompt_attention(x_prompt, norm_mix_pre, w_row, w_t, wcmp, *cmp_w)
    ys5_p, s5_p = _s5_prompt(u_p, s5_ops, s5_d, bsz, seq_len)
    y_p = _merge_mlp(x_prompt.reshape(bsz * seq_len, D_MODEL), o_p, ys5_p, gm_p, mlp_w)

    feature_major = lambda c: jnp.transpose(c, (0, 2, 3, 4, 1))
    o_s, kvc_s, kvs_s, win_t, u_s, gm_s = sample_attention(
        x_sample, feature_major(cache_kv_cmp), feature_major(cache_kv_slc), feature_major(state_kv_win),
        page_table, norm_mix_pre, w_row, w_t, wcmp, *cmp_w)
    win_s = jnp.transpose(win_t, (0, 4, 1, 2, 3))
    ys5_s, s5_s = _s5_sample(u_s, state_s5, s5_ops, s5_d)
    ys5_s = ys5_s.reshape(s, S5_NOCT, LANES).transpose(1, 0, 2)
    y_s = _merge_mlp(x_sample.reshape(s, D_MODEL), o_s, ys5_s, gm_s, mlp_w)

    kv5 = lambda a, b, t: a.reshape(b, t, 2, N_KV, HEAD_DIM)
    token_major = lambda a: jnp.transpose(a.reshape(bsz, 2, N_KV, HEAD_DIM, -1), (0, 4, 1, 2, 3))
    win_rows = min(WINDOW, seq_len)
    win_p = token_major(kvw_p[:, :, seq_len - win_rows:])
    if win_rows < WINDOW:
        win_p = jnp.pad(win_p, ((0, 0), (WINDOW - win_rows, 0), (0, 0), (0, 0), (0, 0)))
    return (y_p.reshape(bsz, seq_len, D_MODEL), y_s.reshape(s, 1, D_MODEL),
            token_major(kvc_p), token_major(kvs_p), win_p, s5_p.astype(x_prompt.dtype),
            kv5(kvc_s, s, 1), kv5(kvs_s, s, 1), kv5(win_s, s, state_kv_win.shape[1]), s5_s.astype(state_s5.dtype))
```
